```python
import math
import jax
import jax.numpy as jnp
from jax import lax
import numpy as np

D_MODEL = 1024
BATCH = 1
SEQ = 16384
DEPTH = 4
DEC_BATCH = 4
DEC_SEQ = 8192
PAST_LEN = 128

GRID_W = 64
EPS = 1e-6

GROUP_W = D_MODEL // 4
MIX_W = 4 * GROUP_W
D_FF = 4 * D_MODEL

HY_C = GROUP_W
HY_ORDER = 2
HY_SHORT = 3
HY_BANDS = 8
HY_EMB = 1 + 2 * HY_BANDS
HY_HID = 64
HY_MIN_DECAY = math.log(1e-2) / 1.5
HY_MAX_DECAY = math.log(1e-2) / 0.3
HY_IN = (HY_ORDER + 1) * HY_C

SSD_HEAD_DIM = 64
SSD_HEADS = GROUP_W // SSD_HEAD_DIM
SSD_GROUPS = 2
SSD_STATE = 128
SSD_CONV = 3
SSD_CHUNK = 128
SSD_INNER = SSD_HEADS * SSD_HEAD_DIM
SSD_BC = SSD_GROUPS * SSD_STATE
SSD_XBC = SSD_INNER + 2 * SSD_BC
SSD_IN = SSD_INNER + SSD_XBC + 2 * SSD_HEADS

GLA_HEADS = 4
GLA_DV = GROUP_W // GLA_HEADS
GLA_DK = GLA_DV // 2
GLA_RANK = 16
GLA_NORMALIZER = 16.0
GLA_CHUNK = 64
GLA_QK = GLA_HEADS * GLA_DK
GLA_V = GLA_HEADS * GLA_DV
GLA_IN = 2 * GLA_QK + 2 * GLA_V + 2 * GLA_RANK

ATT_HEAD_DIM = 64
ATT_HEADS = GROUP_W // ATT_HEAD_DIM
ATT_KV_HEADS = 2
ATT_Q_BLOCK = 128
ROPE_THETA = 10000.0
ROPE_AXIS_DIM = ATT_HEAD_DIM // 2
ATT_IN = (ATT_HEADS + 2 * ATT_KV_HEADS) * ATT_HEAD_DIM

IN_COLS = HY_IN + SSD_IN + GLA_IN + ATT_IN

kernel_name = 'hymba_style_bidir_hybrid_encoder'


def rmsnorm(x, g):
    x32 = x.astype(jnp.float32)
    return x32 * lax.rsqrt(jnp.mean(x32 * x32, axis=-1, keepdims=True) + EPS) * g.astype(jnp.float32)


def split_cols(u, sizes):
    return jnp.split(u, [int(i) for i in np.cumsum(sizes)[:-1]], axis=-1)


def rev(a):
    return jnp.flip(a, axis=1)


def dwconv_centred(u, w, b):
    k, c = w.shape
    y = lax.conv_general_dilated(u, w[:, None, :].astype(u.dtype), window_strides=(1,),
                                 padding=[(k // 2, k // 2)],
                                 dimension_numbers=('NWC', 'WIO', 'NWC'),
                                 feature_group_count=c)
    return y + b.astype(y.dtype)


def hyena_filter_freq(L, w1, b1, w2, b2, w3, freq):
    f32 = jnp.float32
    t = jnp.linspace(0.0, 1.0, L, dtype=f32)[:, None]
    w = 2.0 * math.pi * jnp.arange(L, dtype=f32) / L
    bands = jnp.linspace(1e-4, HY_BANDS - 1, HY_BANDS, dtype=f32)
    ang = w[:, None] * bands
    z = jnp.concatenate([t, jnp.cos(ang), -jnp.sin(ang)], axis=-1)
    fr = freq.astype(f32)
    hid = jnp.sin(fr * (z @ w1.astype(f32) + b1.astype(f32)))
    hid = jnp.sin(fr * (hid @ w2.astype(f32) + b2.astype(f32)))
    h = (hid @ w3.astype(f32)).reshape(L, HY_ORDER, 2, HY_C)
    deltas = jnp.linspace(HY_MIN_DECAY, HY_MAX_DECAY, HY_C, dtype=f32)
    h = h * jnp.exp(-t * jnp.abs(deltas))[:, None, None, :]
    h = h / jnp.sum(jnp.abs(h), axis=(0, 2), keepdims=True)
    h_f, h_b = h[:, :, 0], h[:, :, 1]
    two_sided = jnp.concatenate([h_f[:1] + h_b[:1], h_f[1:], jnp.zeros_like(h_f[:1]),
                                 jnp.flip(h_b[1:], axis=0)], axis=0)
    return jnp.fft.rfft(two_sided, axis=0)


def fft_long_conv(u, hf):
    L = u.shape[1]
    U = jnp.fft.rfft(u, n=2 * L, axis=1)
    return jnp.fft.irfft(U * hf[None], n=2 * L, axis=1)[:, :L]


def hyena_mixer(u, conv_w, conv_b, hf, skip):
    uc = dwconv_centred(u, conv_w, conv_b)
    v, g1, g2 = jnp.split(uc, 3, axis=-1)
    z = v.astype(jnp.float32)
    sk = skip.astype(jnp.float32)
    for o, gate in enumerate((g1, g2)):
        z = gate.astype(jnp.float32) * (fft_long_conv(z, hf[:, o]) + z * sk[o])
    return z


def ssd_chunked(x, dt, A, Bm, Cm):
    b, l, h, p = x.shape
    g, n = Bm.shape[2], Bm.shape[3]
    r = h // g
    Q = SSD_CHUNK
    c = l // Q
    xc = x.reshape(b, c, Q, g, r, p)
    dtc = dt.reshape(b, c, Q, g, r)
    Bc = Bm.reshape(b, c, Q, g, n)
    Cc = Cm.reshape(b, c, Q, g, n)
    cum = jnp.cumsum(dtc * A.reshape(g, r), axis=2)
    causal = jnp.tril(jnp.ones((Q, Q), dtype=bool))[:, :, None, None]
    seg = cum[:, :, :, None] - cum[:, :, None, :]
    decay = jnp.exp(jnp.where(causal, seg, -jnp.inf))
    cb = jnp.einsum('bcign,bcjgn->bcijg', Cc, Bc)
    wgt = cb[..., None] * decay * dtc[:, :, None]
    y_diag = jnp.einsum('bcijgr,bcjgrp->bcigrp', wgt, xc)
    wx = (jnp.exp(cum[:, :, -1:] - cum) * dtc)[..., None] * xc
    states = jnp.einsum('bcjgn,bcjgrp->bcgrpn', Bc, wx)
    chunk_decay = jnp.exp(cum[:, :, -1])

    def step(hs, inp):
        s, d = inp
        return hs * d[..., None, None] + s, hs

    h0 = jnp.zeros((b, g, r, p, n), x.dtype)
    _, h_in = lax.scan(step, h0, (jnp.moveaxis(states, 1, 0), jnp.moveaxis(chunk_decay, 1, 0)))
    h_in = jnp.moveaxis(h_in, 0, 1)
    y_off = jnp.einsum('bcign,bcgrpn->bcigrp', Cc, h_in) * jnp.exp(cum)[..., None]
    return (y_diag + y_off).reshape(b, l, h, p)


def ssd_mixer(u, conv_w, conv_b, A_log, dt_bias, D, norm_g):
    f32 = jnp.float32
    b, L, _ = u.shape
    z, xbc, dt = split_cols(u, [SSD_INNER, SSD_XBC, 2 * SSD_HEADS])
    xbc = jax.nn.silu(dwconv_centred(xbc, conv_w, conv_b))
    xs, Bm, Cm = split_cols(xbc, [SSD_INNER, SSD_BC, SSD_BC])
    xs = xs.reshape(b, L, SSD_HEADS, SSD_HEAD_DIM).astype(f32)
    Bm = Bm.reshape(b, L, SSD_GROUPS, SSD_STATE).astype(f32)
    Cm = Cm.reshape(b, L, SSD_GROUPS, SSD_STATE).astype(f32)
    dt = jax.nn.softplus(dt.astype(f32).reshape(b, L, 2, SSD_HEADS) + dt_bias.astype(f32))
    A = -jnp.exp(A_log.astype(f32))
    y_f = ssd_chunked(xs, dt[:, :, 0], A[0], Bm, Cm)
    y_b = rev(ssd_chunked(rev(xs), rev(dt[:, :, 1]), A[1], rev(Bm), rev(Cm)))
    y = y_f + y_b + D.astype(f32)[:, None] * xs
    y = y.reshape(b, L, SSD_INNER) * jax.nn.silu(z.astype(f32))
    return rmsnorm(y, norm_g)


def gla_chunked(q, k, v, gk):
    b, l, h, dk = q.shape
    dv = v.shape[-1]
    Q = GLA_CHUNK
    c = l // Q
    qc = q.reshape(b, c, Q, h, dk)
    kc = k.reshape(b, c, Q, h, dk)
    vc = v.reshape(b, c, Q, h, dv)
    G = jnp.cumsum(gk.reshape(b, c, Q, h, dk), axis=2)
    q_in = qc * jnp.exp(G)
    k_in = kc * jnp.exp(-G)
    causal = jnp.tril(jnp.ones((Q, Q), dtype=bool))
    att = jnp.where(causal, jnp.einsum('bcihd,bcjhd->bchij', q_in, k_in), 0.0)
    o_intra = jnp.einsum('bchij,bcjhv->bcihv', att, vc)
    k_end = kc * jnp.exp(G[:, :, -1:] - G)
    states = jnp.einsum('bcjhd,bcjhv->bchdv', k_end, vc)
    chunk_decay = jnp.exp(G[:, :, -1])

    def step(S, inp):
        s, d = inp
        return S * d[..., None] + s, S

    S0 = jnp.zeros((b, h, dk, dv), q.dtype)
    _, S_in = lax.scan(step, S0, (jnp.moveaxis(states, 1, 0), jnp.moveaxis(chunk_decay, 1, 0)))
    S_in = jnp.moveaxis(S_in, 0, 1)
    o_inter = jnp.einsum('bcihd,bchdv->bcihv', q_in, S_in)
    return (o_intra + o_inter).reshape(b, l, h, dv)


def gla_mixer(u, gk_w, gk_b, norm_g):
    f32 = jnp.float32
    b, L, _ = u.shape
    q, k, v, go, lr = split_cols(u, [GLA_QK, GLA_QK, GLA_V, GLA_V, 2 * GLA_RANK])
    q = q.reshape(b, L, GLA_HEADS, GLA_DK).astype(f32) * (GLA_DK ** -0.5)
    k = k.reshape(b, L, GLA_HEADS, GLA_DK).astype(f32)
    v = v.reshape(b, L, GLA_HEADS, GLA_DV).astype(f32)
    lr = lr.astype(f32).reshape(b, L, 2, GLA_RANK)
    gk = jax.nn.log_sigmoid(jnp.einsum('bldr,drk->bldk', lr, gk_w.astype(f32)) + gk_b.astype(f32)) / GLA_NORMALIZER
    gk = gk.reshape(b, L, 2, GLA_HEADS, GLA_DK)
    o_f = gla_chunked(q, k, v, gk[:, :, 0])
    o_b = rev(gla_chunked(rev(q), rev(k), rev(v), rev(gk[:, :, 1])))
    o = rmsnorm(o_f + o_b, norm_g).reshape(b, L, GLA_V)
    return o * jax.nn.silu(go.astype(f32))


def axial_rope(seq_len):
    rows = seq_len // GRID_W
    row = jnp.repeat(jnp.arange(rows, dtype=jnp.float32), GRID_W)
    col = jnp.tile(jnp.arange(GRID_W, dtype=jnp.float32), rows)
    inv_freq = 1.0 / (ROPE_THETA ** (jnp.arange(0, ROPE_AXIS_DIM, 2, dtype=jnp.float32) / ROPE_AXIS_DIM))
    ang = jnp.concatenate([row[:, None] * inv_freq, col[:, None] * inv_freq], axis=-1)
    return jnp.cos(ang), jnp.sin(ang)


def apply_rope(x, cos, sin):
    xr = x.reshape(*x.shape[:-1], -1, 2)
    x0, x1 = xr[..., 0], xr[..., 1]
    c, s = cos[None, :, None, :], sin[None, :, None, :]
    return jnp.stack([x0 * c - x1 * s, x0 * s + x1 * c], axis=-1).reshape(x.shape)


def block_attention(q, k, v):
    b, L, hq, dh = q.shape
    hkv = k.shape[2]
    r = hq // hkv
    nb = L // ATT_Q_BLOCK
    qb = jnp.moveaxis(q.reshape(b, nb, ATT_Q_BLOCK, hkv, r, dh), 1, 0)
    scale = dh ** -0.5

    def attend(qblk):
        s = jnp.einsum('bqgrd,bkgd->bgrqk', qblk, k) * scale
        p = jax.nn.softmax(s, axis=-1)
        return jnp.einsum('bgrqk,bkgd->bqgrd', p, v)

    o = lax.map(attend, qb)
    return jnp.moveaxis(o, 0, 1).reshape(b, L, hq * dh)


def attn_mixer(u, q_norm_g, k_norm_g, cos, sin):
    b, L, _ = u.shape
    q, k, v = split_cols(u, [ATT_HEADS * ATT_HEAD_DIM, ATT_KV_HEADS * ATT_HEAD_DIM, ATT_KV_HEADS * ATT_HEAD_DIM])
    q = apply_rope(rmsnorm(q.reshape(b, L, ATT_HEADS, ATT_HEAD_DIM), q_norm_g), cos, sin)
    k = apply_rope(rmsnorm(k.reshape(b, L, ATT_KV_HEADS, ATT_HEAD_DIM), k_norm_g), cos, sin)
    v = v.reshape(b, L, ATT_KV_HEADS, ATT_HEAD_DIM).astype(jnp.float32)
    return block_attention(q, k, v)


def encoder_layer(x, cos, sin, ln1_g, w_in, hy_conv_w, hy_conv_b, hy_ffn_w1, hy_ffn_b1, hy_ffn_w2,
                  hy_ffn_b2, hy_ffn_w3, hy_sin_freq, hy_skip, hy_out_g, ssd_conv_w, ssd_conv_b,
                  ssd_A_log, ssd_dt_bias, ssd_D, ssd_norm_g, gla_gk_w, gla_gk_b, gla_norm_g,
                  att_q_norm_g, att_k_norm_g, att_out_g, w_out, ln2_g, w_up, w_down):
    dt = x.dtype
    L = x.shape[1]
    h = rmsnorm(x, ln1_g).astype(dt)
    u = h @ w_in
    u_hy, u_ssd, u_gla, u_att = split_cols(u, [HY_IN, SSD_IN, GLA_IN, ATT_IN])
    hf = hyena_filter_freq(L, hy_ffn_w1, hy_ffn_b1, hy_ffn_w2, hy_ffn_b2, hy_ffn_w3, hy_sin_freq)
    y_hy = rmsnorm(hyena_mixer(u_hy, hy_conv_w, hy_conv_b, hf, hy_skip), hy_out_g)
    y_ssd = ssd_mixer(u_ssd, ssd_conv_w, ssd_conv_b, ssd_A_log, ssd_dt_bias, ssd_D, ssd_norm_g)
    y_gla = gla_mixer(u_gla, gla_gk_w, gla_gk_b, gla_norm_g)
    y_att = rmsnorm(attn_mixer(u_att, att_q_norm_g, att_k_norm_g, cos, sin), att_out_g)
    mixed = jnp.concatenate([y_hy.astype(dt), y_ssd.astype(dt), y_gla.astype(dt), y_att.astype(dt)], axis=-1)
    x = x + (mixed @ w_out).astype(dt)
    h2 = rmsnorm(x, ln2_g).astype(dt)
    x = x + (jnp.square(jax.nn.relu(h2 @ w_up)) @ w_down).astype(dt)
    return x


def setup_inputs(seed: int = 0) -> dict:
    key = jax.random.key(seed)
    ks = iter(jax.random.split(key, 40))
    f32 = jnp.float32

    def normal(shape, scale):
        return jax.random.normal(next(ks), shape, f32) * scale

    def gain(shape):
        return 1.0 + 0.05 * jax.random.normal(next(ks), shape, f32)

    x_prompt = normal((BATCH, SEQ, D_MODEL), 1.0)
    x_sample = normal((DEC_BATCH, DEC_SEQ, D_MODEL), 1.0)
    ln1_g = gain((DEPTH, D_MODEL))
    w_in = normal((DEPTH, D_MODEL, IN_COLS), D_MODEL ** -0.5)
    hy_conv_w = normal((DEPTH, HY_SHORT, HY_IN), HY_SHORT ** -0.5)
    hy_conv_b = normal((DEPTH, HY_IN), 0.02)
    hy_ffn_w1 = normal((DEPTH, HY_EMB, HY_HID), HY_EMB ** -0.5)
    hy_ffn_b1 = normal((DEPTH, HY_HID), 0.1)
    hy_ffn_w2 = normal((DEPTH, HY_HID, HY_HID), HY_HID ** -0.5)
    hy_ffn_b2 = normal((DEPTH, HY_HID), 0.1)
    hy_ffn_w3 = normal((DEPTH, HY_HID, HY_ORDER * 2 * HY_C), HY_HID ** -0.5)
    hy_sin_freq = gain((DEPTH, HY_HID))
    hy_skip = normal((DEPTH, HY_ORDER, HY_C), 0.5)
    hy_out_g = gain((DEPTH, HY_C))
    ssd_conv_w = normal((DEPTH, SSD_CONV, SSD_XBC), SSD_CONV ** -0.5)
    ssd_conv_b = normal((DEPTH, SSD_XBC), 0.02)
    ssd_A_log = jnp.log(jax.random.uniform(next(ks), (DEPTH, 2, SSD_HEADS), f32, 1.0, 16.0))
    dt0 = jnp.exp(jax.random.uniform(next(ks), (DEPTH, 2, SSD_HEADS), f32, math.log(1e-3), math.log(1e-1)))
    ssd_dt_bias = dt0 + jnp.log(-jnp.expm1(-dt0))
    ssd_D = gain((DEPTH, SSD_HEADS))
    ssd_norm_g = gain((DEPTH, SSD_INNER))
    gla_gk_w = normal((DEPTH, 2, GLA_RANK, GLA_QK), GLA_RANK ** -0.5)
    gla_gk_b = normal((DEPTH, 2, GLA_QK), 0.1)
    gla_norm_g = gain((DEPTH, GLA_DV))
    att_q_norm_g = gain((DEPTH, ATT_HEAD_DIM))
    att_k_norm_g = gain((DEPTH, ATT_HEAD_DIM))
    att_out_g = gain((DEPTH, ATT_HEADS * ATT_HEAD_DIM))
    w_out = normal((DEPTH, MIX_W, D_MODEL), MIX_W ** -0.5)
    ln2_g = gain((DEPTH, D_MODEL))
    w_up = normal((DEPTH, D_MODEL, D_FF), D_MODEL ** -0.5)
    w_down = normal((DEPTH, D_FF, D_MODEL), D_FF ** -0.5)
    return {
        'x_prompt': x_prompt, 'x_sample': x_sample, 'ln1_g': ln1_g, 'w_in': w_in,
        'hy_conv_w': hy_conv_w, 'hy_conv_b': hy_conv_b, 'hy_ffn_w1': hy_ffn_w1, 'hy_ffn_b1': hy_ffn_b1,
        'hy_ffn_w2': hy_ffn_w2, 'hy_ffn_b2': hy_ffn_b2, 'hy_ffn_w3': hy_ffn_w3, 'hy_sin_freq': hy_sin_freq,
        'hy_skip': hy_skip, 'hy_out_g': hy_out_g, 'ssd_conv_w': ssd_conv_w, 'ssd_conv_b': ssd_conv_b,
        'ssd_A_log': ssd_A_log, 'ssd_dt_bias': ssd_dt_bias, 'ssd_D': ssd_D, 'ssd_norm_g': ssd_norm_g,
        'gla_gk_w': gla_gk_w, 'gla_gk_b': gla_gk_b, 'gla_norm_g': gla_norm_g,
        'att_q_norm_g': att_q_norm_g, 'att_k_norm_g': att_k_norm_g, 'att_out_g': att_out_g,
        'w_out': w_out, 'ln2_g': ln2_g, 'w_up': w_up, 'w_down': w_down,
    }


def reference(x_prompt, x_sample, ln1_g, w_in, hy_conv_w, hy_conv_b, hy_ffn_w1, hy_ffn_b1, hy_ffn_w2,
              hy_ffn_b2, hy_ffn_w3, hy_sin_freq, hy_skip, hy_out_g, ssd_conv_w, ssd_conv_b, ssd_A_log,
              ssd_dt_bias, ssd_D, ssd_norm_g, gla_gk_w, gla_gk_b, gla_norm_g, att_q_norm_g,
              att_k_norm_g, att_out_g, w_out, ln2_g, w_up, w_down):
    cos_p, sin_p = axial_rope(x_prompt.shape[1])
    cos_s, sin_s = axial_rope(x_sample.shape[1])
    y_prompt = x_prompt
    y_sample = x_sample
    for i in range(DEPTH):
        lp = dict(ln1_g=ln1_g[i], w_in=w_in[i], hy_conv_w=hy_conv_w[i], hy_conv_b=hy_conv_b[i],
                  hy_ffn_w1=hy_ffn_w1[i], hy_ffn_b1=hy_ffn_b1[i], hy_ffn_w2=hy_ffn_w2[i],
                  hy_ffn_b2=hy_ffn_b2[i], hy_ffn_w3=hy_ffn_w3[i], hy_sin_freq=hy_sin_freq[i],
                  hy_skip=hy_skip[i], hy_out_g=hy_out_g[i], ssd_conv_w=ssd_conv_w[i],
                  ssd_conv_b=ssd_conv_b[i], ssd_A_log=ssd_A_log[i], ssd_dt_bias=ssd_dt_bias[i],
                  ssd_D=ssd_D[i], ssd_norm_g=ssd_norm_g[i], gla_gk_w=gla_gk_w[i], gla_gk_b=gla_gk_b[i],
                  gla_norm_g=gla_norm_g[i], att_q_norm_g=att_q_norm_g[i], att_k_norm_g=att_k_norm_g[i],
                  att_out_g=att_out_g[i], w_out=w_out[i], ln2_g=ln2_g[i], w_up=w_up[i], w_down=w_down[i])
        y_prompt = encoder_layer(y_prompt, cos_p, sin_p, **lp)
        y_sample = encoder_layer(y_sample, cos_s, sin_s, **lp)
    return (y_prompt, y_sample)
```

```python
import functools
import math

import jax
import jax.numpy as jnp
import numpy as np
from jax import lax
from jax.experimental import pallas as pl
from jax.experimental.pallas import tpu as pltpu

D_MODEL = 1024
DEPTH = 4
GRID_W = 64
EPS = 1e-6
GROUP_W = 256
D_FF = 4096

HY_C = 256
HY_ORDER = 2
HY_BANDS = 8
HY_MIN_DECAY = math.log(1e-2) / 1.5
HY_MAX_DECAY = math.log(1e-2) / 0.3
HY_IN = 768

SSD_HEAD_DIM = 64
SSD_HEADS = 4
SSD_GROUPS = 2
SSD_STATE = 128
SSD_CHUNK = 128
SSD_INNER = 256
SSD_BC = 256
SSD_XBC = 768
SSD_IN = 1032

GLA_HEADS = 4
GLA_DV = 64
GLA_DK = 32
GLA_RANK = 16
GLA_NORMALIZER = 16.0
GLA_CHUNK = 64
GLA_QK = 128
GLA_V = 256
GLA_IN = 800

ATT_HEAD_DIM = 64
ATT_HEADS = 4
ATT_KV_HEADS = 2
ROPE_THETA = 10000.0
ROPE_AXIS_DIM = 32
ATT_IN = 512

LANES = 128
SSD_PAD = 1152
GLA_PAD = 896
ATT_Q_PAD = ATT_HEADS * LANES
ATT_PAD = ATT_Q_PAD + 2 * LANES
IN_PAD = HY_IN + SSD_PAD + GLA_PAD + ATT_PAD

VMEM_LIMIT = 48 * 1024 * 1024


def _cparams(n_grid):
    return pltpu.CompilerParams(dimension_semantics=("arbitrary",) * n_grid,
                                vmem_limit_bytes=VMEM_LIMIT)


def _in_proj_kernel(x_ref, g_ref, w_ref, hy_ref, ssd_ref, gla_ref, att_ref):
    x = x_ref[...]
    ms = jnp.mean(x * x, axis=-1, keepdims=True)
    h = (x * lax.rsqrt(ms + EPS) * g_ref[...]).astype(jnp.bfloat16)
    c0 = 0
    for ref in (hy_ref, ssd_ref, gla_ref, att_ref):
        n = ref.shape[-1]
        ref[...] = jnp.dot(h, w_ref[:, c0:c0 + n], preferred_element_type=jnp.float32)
        c0 += n


def in_proj(x2d, g, w_p, tm=256):
    t = x2d.shape[0]
    widths = (HY_IN, SSD_PAD, GLA_PAD, ATT_PAD)
    return pl.pallas_call(
        _in_proj_kernel,
        grid=(t // tm,),
        in_specs=[pl.BlockSpec((tm, D_MODEL), lambda i: (i, 0)),
                  pl.BlockSpec((1, D_MODEL), lambda i: (0, 0)),
                  pl.BlockSpec((D_MODEL, IN_PAD), lambda i: (0, 0))],
        out_specs=[pl.BlockSpec((tm, n), lambda i: (i, 0)) for n in widths],
        out_shape=[jax.ShapeDtypeStruct((t, n), jnp.float32) for n in widths],
        compiler_params=_cparams(1),
        name="in_proj",
    )(x2d, g.reshape(1, D_MODEL), w_p)


def _rope(x, cos_t, sin_t):
    lane = lax.broadcasted_iota(jnp.int32, x.shape, 1)
    nxt = pltpu.roll(x, LANES - 32, axis=1)
    prv = pltpu.roll(x, 32, axis=1)
    sw = jnp.where((lane % 64) < 32, nxt, prv)
    return x * cos_t + sw * sin_t


def _att_prep_kernel(u_ref, cos_ref, sin_ref, gq_ref, gk_ref, q_ref, k_ref, v_ref, vs_ref):
    cos_t = cos_ref[...]
    sin_t = sin_ref[...]
    inv = 1.0 / ATT_HEAD_DIM
    for h in range(ATT_HEADS):
        x = u_ref[:, h * LANES:(h + 1) * LANES]
        ms = jnp.sum(x * x, axis=-1, keepdims=True) * inv
        xn = x * lax.rsqrt(ms + EPS) * gq_ref[...]
        q_ref[:, h * LANES:(h + 1) * LANES] = (
            _rope(xn, cos_t, sin_t) * (ATT_HEAD_DIM ** -0.5)).astype(jnp.bfloat16)
    k = u_ref[:, ATT_Q_PAD:ATT_Q_PAD + LANES]
    lane = lax.broadcasted_iota(jnp.int32, k.shape, 1)
    k2 = k * k
    s_all = jnp.sum(k2, axis=-1, keepdims=True)
    s_lo = jnp.sum(jnp.where(lane < 64, k2, 0.0), axis=-1, keepdims=True)
    ms = jnp.where(lane < 64, s_lo, s_all - s_lo) * inv
    kn = k * lax.rsqrt(ms + EPS) * gk_ref[...]
    k_ref[...] = _rope(kn, cos_t, sin_t).astype(jnp.bfloat16)
    v = u_ref[:, ATT_Q_PAD + LANES:ATT_PAD]
    v_ref[...] = v.astype(jnp.bfloat16)
    vs_ref[...] = pltpu.roll(v, 64, axis=1).astype(jnp.bfloat16)


def att_prep(u_att, cos_t, sin_t, gq, gk, seq_len, tm=512):
    t = u_att.shape[0]
    nl = seq_len // tm
    return pl.pallas_call(
        _att_prep_kernel,
        grid=(t // tm,),
        in_specs=[pl.BlockSpec((tm, ATT_PAD), lambda i: (i, 0)),
                  pl.BlockSpec((tm, LANES), lambda i: (i % nl, 0)),
                  pl.BlockSpec((tm, LANES), lambda i: (i % nl, 0)),
                  pl.BlockSpec((1, LANES), lambda i: (0, 0)),
                  pl.BlockSpec((1, LANES), lambda i: (0, 0))],
        out_specs=[pl.BlockSpec((tm, ATT_Q_PAD), lambda i: (i, 0)),
                   pl.BlockSpec((tm, LANES), lambda i: (i, 0)),
                   pl.BlockSpec((tm, LANES), lambda i: (i, 0)),
                   pl.BlockSpec((tm, LANES), lambda i: (i, 0))],
        out_shape=[jax.ShapeDtypeStruct((t, ATT_Q_PAD), jnp.bfloat16),
                   jax.ShapeDtypeStruct((t, LANES), jnp.bfloat16),
                   jax.ShapeDtypeStruct((t, LANES), jnp.bfloat16),
                   jax.ShapeDtypeStruct((t, LANES), jnp.bfloat16)],
        compiler_params=_cparams(1),
        name="att_prep",
    )(u_att, cos_t, sin_t, gq, gk)


def _flash_kernel(q_ref, k_ref, v_ref, vs_ref, o_ref, m_ref, l_ref, acc_ref, *, tk, nk):
    m_ref[...] = jnp.full(m_ref.shape, -jnp.inf, jnp.float32)
    l_ref[...] = jnp.zeros(l_ref.shape, jnp.float32)
    acc_ref[...] = jnp.zeros(acc_ref.shape, jnp.float32)

    def body(j, carry):
        off = pl.multiple_of(j * tk, tk)
        k = k_ref[0, pl.ds(off, tk), :]
        v = v_ref[0, pl.ds(off, tk), :]
        vs = vs_ref[0, pl.ds(off, tk), :]
        for h in range(ATT_HEADS):
            q = q_ref[0, :, h * LANES:(h + 1) * LANES]
            s = lax.dot_general(q, k, (((1,), (1,)), ((), ())),
                                preferred_element_type=jnp.float32)
            m_prev = m_ref[h]
            m_new = jnp.maximum(m_prev, jnp.max(s, axis=-1, keepdims=True))
            alpha = jnp.exp(m_prev - m_new)
            p = jnp.exp(s - m_new)
            l_ref[h] = alpha * l_ref[h] + jnp.sum(p, axis=-1, keepdims=True)
            vv = v if h in (0, 3) else vs
            acc_ref[h] = alpha * acc_ref[h] + jnp.dot(p.astype(jnp.bfloat16), vv,
                                                      preferred_element_type=jnp.float32)
            m_ref[h] = m_new
        return carry

    lax.fori_loop(0, nk, body, 0)
    lane = lax.broadcasted_iota(jnp.int32, acc_ref.shape[1:], 1)
    o = [acc_ref[h] / l_ref[h] for h in range(ATT_HEADS)]
    o_ref[0, :, 0:LANES] = jnp.where(lane < 64, o[0], o[1])
    o_ref[0, :, LANES:2 * LANES] = jnp.where(lane < 64, o[2], o[3])


def flash_attention(q, k, v, vs, tq=256, tk=512):
    b, seq_len, _ = q.shape
    kv_spec = pl.BlockSpec((1, seq_len, LANES), lambda bi, i: (bi, 0, 0))
    return pl.pallas_call(
        functools.partial(_flash_kernel, tk=tk, nk=seq_len // tk),
        grid=(b, seq_len // tq),
        in_specs=[pl.BlockSpec((1, tq, ATT_Q_PAD), lambda bi, i: (bi, i, 0)),
                  kv_spec, kv_spec, kv_spec],
        out_specs=pl.BlockSpec((1, tq, GROUP_W), lambda bi, i: (bi, i, 0)),
        out_shape=jax.ShapeDtypeStruct((b, seq_len, GROUP_W), jnp.float32),
        scratch_shapes=[pltpu.VMEM((ATT_HEADS, tq, 1), jnp.float32),
                        pltpu.VMEM((ATT_HEADS, tq, 1), jnp.float32),
                        pltpu.VMEM((ATT_HEADS, tq, LANES), jnp.float32)],
        compiler_params=_cparams(2),
        name="flash_attention",
    )(q, k, v, vs)


def _out_ffn_kernel(x_ref, mix_ref, wo_ref, g2_ref, wu_ref, wd_ref, o_ref, *, tf):
    x1 = x_ref[...] + jnp.dot(mix_ref[...], wo_ref[...], preferred_element_type=jnp.float32)
    ms = jnp.mean(x1 * x1, axis=-1, keepdims=True)
    h2 = (x1 * lax.rsqrt(ms + EPS) * g2_ref[...]).astype(jnp.bfloat16)
    acc = x1
    for c in range(D_FF // tf):
        up = jnp.dot(h2, wu_ref[:, c * tf:(c + 1) * tf], preferred_element_type=jnp.float32)
        a = jnp.square(jnp.maximum(up, 0.0)).astype(jnp.bfloat16)
        acc = acc + jnp.dot(a, wd_ref[c * tf:(c + 1) * tf, :], preferred_element_type=jnp.float32)
    o_ref[...] = acc


def out_ffn(x2d, mixed, wo, g2, wu, wd, tm=512, tf=1024):
    t = x2d.shape[0]
    const = lambda i: (0, 0)
    single = pl.Buffered(1)
    return pl.pallas_call(
        functools.partial(_out_ffn_kernel, tf=tf),
        grid=(t // tm,),
        in_specs=[pl.BlockSpec((tm, D_MODEL), lambda i: (i, 0)),
                  pl.BlockSpec((tm, D_MODEL), lambda i: (i, 0)),
                  pl.BlockSpec((D_MODEL, D_MODEL), const, pipeline_mode=single),
                  pl.BlockSpec((1, D_MODEL), const),
                  pl.BlockSpec((D_MODEL, D_FF), const, pipeline_mode=single),
                  pl.BlockSpec((D_FF, D_MODEL), const, pipeline_mode=single)],
        out_specs=pl.BlockSpec((tm, D_MODEL), lambda i: (i, 0)),
        out_shape=jax.ShapeDtypeStruct((t, D_MODEL), jnp.float32),
        compiler_params=_cparams(1),
        name="out_ffn",
    )(x2d, mixed, wo, g2.reshape(1, D_MODEL), wu, wd)


def _rmsnorm(x, g):
    return x * lax.rsqrt(jnp.mean(x * x, axis=-1, keepdims=True) + EPS) * g


def _rev(a):
    return jnp.flip(a, axis=1)


def _dwconv(u, w, b):
    k, c = w.shape
    y = lax.conv_general_dilated(u, w[:, None, :], window_strides=(1,), padding=[(k // 2, k // 2)],
                                 dimension_numbers=('NWC', 'WIO', 'NWC'), feature_group_count=c)
    return y + b


def _hyena_filter_freq(L, w1, b1, w2, b2, w3, freq):
    f32 = jnp.float32
    t = jnp.linspace(0.0, 1.0, L, dtype=f32)[:, None]
    w = 2.0 * math.pi * jnp.arange(L, dtype=f32) / L
    bands = jnp.linspace(1e-4, HY_BANDS - 1, HY_BANDS, dtype=f32)
    ang = w[:, None] * bands
    z = jnp.concatenate([t, jnp.cos(ang), -jnp.sin(ang)], axis=-1)
    hid = jnp.sin(freq * (z @ w1 + b1))
    hid = jnp.sin(freq * (hid @ w2 + b2))
    h = (hid @ w3).reshape(L, HY_ORDER, 2, HY_C)
    deltas = jnp.linspace(HY_MIN_DECAY, HY_MAX_DECAY, HY_C, dtype=f32)
    h = h * jnp.exp(-t * jnp.abs(deltas))[:, None, None, :]
    h = h / jnp.sum(jnp.abs(h), axis=(0, 2), keepdims=True)
    h_f, h_b = h[:, :, 0], h[:, :, 1]
    two_sided = jnp.concatenate([h_f[:1] + h_b[:1], h_f[1:], jnp.zeros_like(h_f[:1]),
                                 jnp.flip(h_b[1:], axis=0)], axis=0)
    return jnp.fft.rfft(two_sided, axis=0)


def _fft_long_conv(u, hf):
    L = u.shape[1]
    U = jnp.fft.rfft(u, n=2 * L, axis=1)
    return jnp.fft.irfft(U * hf[None], n=2 * L, axis=1)[:, :L]


def _hyena_mixer(u, conv_w, conv_b, hf, skip):
    uc = _dwconv(u, conv_w, conv_b)
    v, g1, g2 = jnp.split(uc, 3, axis=-1)
    z = v
    for o, gate in enumerate((g1, g2)):
        z = gate * (_fft_long_conv(z, hf[:, o]) + z * skip[o])
    return z


def _ssd_chunked(x, dt, A, Bm, Cm):
    b, l, h, p = x.shape
    g, n = Bm.shape[2], Bm.shape[3]
    r = h // g
    Q = SSD_CHUNK
    c = l // Q
    xc = x.reshape(b, c, Q, g, r, p)
    dtc = dt.reshape(b, c, Q, g, r)
    Bc = Bm.reshape(b, c, Q, g, n)
    Cc = Cm.reshape(b, c, Q, g, n)
    cum = jnp.cumsum(dtc * A.reshape(g, r), axis=2)
    causal = jnp.tril(jnp.ones((Q, Q), dtype=bool))[:, :, None, None]
    seg = cum[:, :, :, None] - cum[:, :, None, :]
    decay = jnp.exp(jnp.where(causal, seg, -jnp.inf))
    cb = jnp.einsum('bcign,bcjgn->bcijg', Cc, Bc)
    wgt = cb[..., None] * decay * dtc[:, :, None]
    y_diag = jnp.einsum('bcijgr,bcjgrp->bcigrp', wgt, xc)
    wx = (jnp.exp(cum[:, :, -1:] - cum) * dtc)[..., None] * xc
    states = jnp.einsum('bcjgn,bcjgrp->bcgrpn', Bc, wx)
    chunk_decay = jnp.exp(cum[:, :, -1])

    def step(hs, inp):
        s, d = inp
        return hs * d[..., None, None] + s, hs

    h0 = jnp.zeros((b, g, r, p, n), x.dtype)
    _, h_in = lax.scan(step, h0, (jnp.moveaxis(states, 1, 0), jnp.moveaxis(chunk_decay, 1, 0)))
    h_in = jnp.moveaxis(h_in, 0, 1)
    y_off = jnp.einsum('bcign,bcgrpn->bcigrp', Cc, h_in) * jnp.exp(cum)[..., None]
    return (y_diag + y_off).reshape(b, l, h, p)


def _ssd_mixer(u, conv_w, conv_b, A_log, dt_bias, D, norm_g):
    b, L, _ = u.shape
    z = u[..., :SSD_INNER]
    xbc = u[..., SSD_INNER:SSD_INNER + SSD_XBC]
    dt = u[..., SSD_INNER + SSD_XBC:SSD_IN]
    xbc = jax.nn.silu(_dwconv(xbc, conv_w, conv_b))
    xs = xbc[..., :SSD_INNER].reshape(b, L, SSD_HEADS, SSD_HEAD_DIM)
    Bm = xbc[..., SSD_INNER:SSD_INNER + SSD_BC].reshape(b, L, SSD_GROUPS, SSD_STATE)
    Cm = xbc[..., SSD_INNER + SSD_BC:].reshape(b, L, SSD_GROUPS, SSD_STATE)
    dt = jax.nn.softplus(dt.reshape(b, L, 2, SSD_HEADS) + dt_bias)
    A = -jnp.exp(A_log)
    y_f = _ssd_chunked(xs, dt[:, :, 0], A[0], Bm, Cm)
    y_b = _rev(_ssd_chunked(_rev(xs), _rev(dt[:, :, 1]), A[1], _rev(Bm), _rev(Cm)))
    y = y_f + y_b + D[:, None] * xs
    y = y.reshape(b, L, SSD_INNER) * jax.nn.silu(z)
    return _rmsnorm(y, norm_g)


def _gla_chunked(q, k, v, gk):
    b, l, h, dk = q.shape
    dv = v.shape[-1]
    Q = GLA_CHUNK
    c = l // Q
    qc = q.reshape(b, c, Q, h, dk)
    kc = k.reshape(b, c, Q, h, dk)
    vc = v.reshape(b, c, Q, h, dv)
    G = jnp.cumsum(gk.reshape(b, c, Q, h, dk), axis=2)
    q_in = qc * jnp.exp(G)
    k_in = kc * jnp.exp(-G)
    causal = jnp.tril(jnp.ones((Q, Q), dtype=bool))
    att = jnp.where(causal, jnp.einsum('bcihd,bcjhd->bchij', q_in, k_in), 0.0)
    o_intra = jnp.einsum('bchij,bcjhv->bcihv', att, vc)
    k_end = kc * jnp.exp(G[:, :, -1:] - G)
    states = jnp.einsum('bcjhd,bcjhv->bchdv', k_end, vc)
    chunk_decay = jnp.exp(G[:, :, -1])

    def step(S, inp):
        s, d = inp
        return S * d[..., None] + s, S

    S0 = jnp.zeros((b, h, dk, dv), q.dtype)
    _, S_in = lax.scan(step, S0, (jnp.moveaxis(states, 1, 0), jnp.moveaxis(chunk_decay, 1, 0)))
    S_in = jnp.moveaxis(S_in, 0, 1)
    o_inter = jnp.einsum('bcihd,bchdv->bcihv', q_in, S_in)
    return (o_intra + o_inter).reshape(b, l, h, dv)


def _gla_mixer(u, gk_w, gk_b, norm_g):
    b, L, _ = u.shape
    q = u[..., :GLA_QK].reshape(b, L, GLA_HEADS, GLA_DK) * (GLA_DK ** -0.5)
    k = u[..., GLA_QK:2 * GLA_QK].reshape(b, L, GLA_HEADS, GLA_DK)
    v = u[..., 2 * GLA_QK:2 * GLA_QK + GLA_V].reshape(b, L, GLA_HEADS, GLA_DV)
    go = u[..., 2 * GLA_QK + GLA_V:2 * GLA_QK + 2 * GLA_V]
    lr = u[..., 2 * GLA_QK + 2 * GLA_V:GLA_IN].reshape(b, L, 2, GLA_RANK)
    gk = jax.nn.log_sigmoid(jnp.einsum('bldr,drk->bldk', lr, gk_w) + gk_b) / GLA_NORMALIZER
    gk = gk.reshape(b, L, 2, GLA_HEADS, GLA_DK)
    o_f = _gla_chunked(q, k, v, gk[:, :, 0])
    o_b = _rev(_gla_chunked(_rev(q), _rev(k), _rev(v), _rev(gk[:, :, 1])))
    o = _rmsnorm(o_f + o_b, norm_g).reshape(b, L, GLA_V)
    return o * jax.nn.silu(go)


_DEINTERLEAVE = np.concatenate([np.arange(0, ATT_HEAD_DIM, 2), np.arange(1, ATT_HEAD_DIM, 2)])


def _pack_w_in(w):
    c0 = 0
    w_hy = w[:, c0:c0 + HY_IN]; c0 += HY_IN
    w_ssd = w[:, c0:c0 + SSD_IN]; c0 += SSD_IN
    w_gla = w[:, c0:c0 + GLA_IN]; c0 += GLA_IN
    w_att = w[:, c0:c0 + ATT_IN]
    zeros = lambda n: jnp.zeros((D_MODEL, n), w.dtype)
    cols = [w_hy, w_ssd, zeros(SSD_PAD - SSD_IN), w_gla, zeros(GLA_PAD - GLA_IN)]
    for h in range(ATT_HEADS):
        wh = w_att[:, h * ATT_HEAD_DIM:(h + 1) * ATT_HEAD_DIM][:, _DEINTERLEAVE]
        cols += [wh, zeros(64)] if h < 2 else [zeros(64), wh]
    kq = ATT_HEADS * ATT_HEAD_DIM
    for g in range(ATT_KV_HEADS):
        cols.append(w_att[:, kq + g * ATT_HEAD_DIM:kq + (g + 1) * ATT_HEAD_DIM][:, _DEINTERLEAVE])
    cols.append(w_att[:, kq + ATT_KV_HEADS * ATT_HEAD_DIM:])
    return jnp.concatenate(cols, axis=1).astype(jnp.bfloat16)


def _rope_tables(seq_len):
    rows = seq_len // GRID_W
    row = jnp.repeat(jnp.arange(rows, dtype=jnp.float32), GRID_W)
    col = jnp.tile(jnp.arange(GRID_W, dtype=jnp.float32), rows)
    inv_freq = 1.0 / (ROPE_THETA ** (jnp.arange(0, ROPE_AXIS_DIM, 2, dtype=jnp.float32) / ROPE_AXIS_DIM))
    ang = jnp.concatenate([row[:, None] * inv_freq, col[:, None] * inv_freq], axis=-1)
    cos, sin = jnp.cos(ang), jnp.sin(ang)
    return jnp.tile(cos, (1, 4)), jnp.tile(jnp.concatenate([-sin, sin], axis=-1), (1, 2))


def _layer(x, tables, p):
    b, L, _ = x.shape
    x2d = x.reshape(b * L, D_MODEL)
    u_hy, u_ssd, u_gla, u_att = in_proj(x2d, p['ln1_g'], p['w_in_p'])

    hf = _hyena_filter_freq(L, p['hy_ffn_w1'], p['hy_ffn_b1'], p['hy_ffn_w2'], p['hy_ffn_b2'],
                            p['hy_ffn_w3'], p['hy_sin_freq'])
    y_hy = _rmsnorm(_hyena_mixer(u_hy.reshape(b, L, HY_IN), p['hy_conv_w'], p['hy_conv_b'], hf,
                                 p['hy_skip']), p['hy_out_g'])
    y_ssd = _ssd_mixer(u_ssd.reshape(b, L, SSD_PAD), p['ssd_conv_w'], p['ssd_conv_b'], p['ssd_A_log'],
                       p['ssd_dt_bias'], p['ssd_D'], p['ssd_norm_g'])
    y_gla = _gla_mixer(u_gla.reshape(b, L, GLA_PAD), p['gla_gk_w'], p['gla_gk_b'], p['gla_norm_g'])

    gq = jnp.tile(p['att_q_norm_g'][_DEINTERLEAVE], 2).reshape(1, LANES)
    gk = jnp.tile(p['att_k_norm_g'][_DEINTERLEAVE], 2).reshape(1, LANES)
    q, k, v, vs = att_prep(u_att, tables[0], tables[1], gq, gk, L)
    o = flash_attention(q.reshape(b, L, ATT_Q_PAD), k.reshape(b, L, LANES), v.reshape(b, L, LANES),
                        vs.reshape(b, L, LANES))
    y_att = _rmsnorm(o, p['att_out_g'])

    mixed = jnp.concatenate([y_hy, y_ssd, y_gla, y_att], axis=-1).astype(jnp.bfloat16)
    y = out_ffn(x2d, mixed.reshape(b * L, D_MODEL), p['w_out_b'], p['ln2_g'], p['w_up_b'], p['w_down_b'])
    return y.reshape(b, L, D_MODEL)


def kernel(x_prompt, x_sample, ln1_g, w_in, hy_conv_w, hy_conv_b, hy_ffn_w1, hy_ffn_b1, hy_ffn_w2, hy_ffn_b2, hy_ffn_w3, hy_sin_freq, hy_skip, hy_out_g, ssd_conv_w, ssd_conv_b, ssd_A_log, ssd_dt_bias, ssd_D, ssd_norm_g, gla_gk_w, gla_gk_b, gla_norm_g, att_q_norm_g, att_k_norm_g, att_out_g, w_out, ln2_g, w_up, w_down):
    stacked = dict(ln1_g=ln1_g, hy_conv_w=hy_conv_w, hy_conv_b=hy_conv_b, hy_ffn_w1=hy_ffn_w1,
                   hy_ffn_b1=hy_ffn_b1, hy_ffn_w2=hy_ffn_w2, hy_ffn_b2=hy_ffn_b2, hy_ffn_w3=hy_ffn_w3,
                   hy_sin_freq=hy_sin_freq, hy_skip=hy_skip, hy_out_g=hy_out_g, ssd_conv_w=ssd_conv_w,
                   ssd_conv_b=ssd_conv_b, ssd_A_log=ssd_A_log, ssd_dt_bias=ssd_dt_bias, ssd_D=ssd_D,
                   ssd_norm_g=ssd_norm_g, gla_gk_w=gla_gk_w, gla_gk_b=gla_gk_b, gla_norm_g=gla_norm_g,
                   att_q_norm_g=att_q_norm_g, att_k_norm_g=att_k_norm_g, att_out_g=att_out_g, ln2_g=ln2_g)
    tables_p = _rope_tables(x_prompt.shape[1])
    tables_s = _rope_tables(x_sample.shape[1])
    y_prompt, y_sample = x_prompt, x_sample
    for i in range(DEPTH):
        p = {name: val[i] for name, val in stacked.items()}
        p['w_in_p'] = _pack_w_in(w_in[i])
        p['w_out_b'] = w_out[i].astype(jnp.bfloat16)
        p['w_up_b'] = w_up[i].astype(jnp.bfloat16)
        p['w_down_b'] = w_down[i].astype(jnp.bfloat16)
        y_prompt = _layer(y_prompt, tables_p, p)
        y_sample = _layer(y_sample, tables_s, p)
    return (y_prompt, y_sample)
```

```python
import functools
import math

import jax
import jax.numpy as jnp
import numpy as np
from jax import lax
from jax.experimental import pallas as pl
from jax.experimental.pallas import tpu as pltpu

D_MODEL = 1024
DEPTH = 4
GRID_W = 64
EPS = 1e-6
GROUP_W = 256
D_FF = 4096

HY_C = 256
HY_ORDER = 2
HY_BANDS = 8
HY_MIN_DECAY = math.log(1e-2) / 1.5
HY_MAX_DECAY = math.log(1e-2) / 0.3
HY_IN = 768

SSD_HEAD_DIM = 64
SSD_HEADS = 4
SSD_GROUPS = 2
SSD_STATE = 128
SSD_CHUNK = 128
SSD_INNER = 256
SSD_BC = 256
SSD_XBC = 768
SSD_IN = 1032

GLA_HEADS = 4
GLA_DV = 64
GLA_DK = 32
GLA_RANK = 16
GLA_NORMALIZER = 16.0
GLA_CHUNK = 64
GLA_QK = 128
GLA_V = 256
GLA_IN = 800

ATT_HEAD_DIM = 64
ATT_HEADS = 4
ATT_KV_HEADS = 2
ROPE_THETA = 10000.0
ROPE_AXIS_DIM = 32
ATT_IN = 512

LANES = 128
SSD_DTX = 2 * SSD_HEADS * SSD_HEAD_DIM
SSD_PAD = SSD_INNER + SSD_XBC + SSD_DTX
GLA_PAD = 896
GLA_LR0 = 2 * GLA_QK + 2 * GLA_V
ATT_Q_PAD = ATT_HEADS * LANES
ATT_PAD = ATT_Q_PAD + 2 * LANES
IN_PAD = HY_IN + SSD_PAD + GLA_PAD + ATT_PAD

VMEM_LIMIT = 48 * 1024 * 1024
F32 = jnp.float32
BF16 = jnp.bfloat16
HIGHEST = lax.Precision.HIGHEST
NT = (((1,), (1,)), ((), ()))
TN = (((0,), (0,)), ((), ()))


def _cparams(n_grid):
    return pltpu.CompilerParams(dimension_semantics=("arbitrary",) * n_grid,
                                vmem_limit_bytes=VMEM_LIMIT)


def _softplus(x):
    return jnp.maximum(x, 0.0) + jnp.log1p(jnp.exp(-jnp.abs(x)))


def _silu(x):
    return x * jax.nn.sigmoid(x)


def _in_proj_kernel(x_ref, g_ref, w_ref, hy_ref, ssd_ref, gla_ref, att_ref):
    x = x_ref[...]
    ms = jnp.mean(x * x, axis=-1, keepdims=True)
    h = (x * lax.rsqrt(ms + EPS) * g_ref[...]).astype(BF16)
    c0 = 0
    for ref in (hy_ref, ssd_ref, gla_ref, att_ref):
        n = ref.shape[-1]
        ref[...] = jnp.dot(h, w_ref[:, c0:c0 + n], preferred_element_type=F32)
        c0 += n


def in_proj(x2d, g, w_p, tm=256):
    t = x2d.shape[0]
    widths = (HY_IN, SSD_PAD, GLA_PAD, ATT_PAD)
    return pl.pallas_call(
        _in_proj_kernel,
        grid=(t // tm,),
        in_specs=[pl.BlockSpec((tm, D_MODEL), lambda i: (i, 0)),
                  pl.BlockSpec((1, D_MODEL), lambda i: (0, 0)),
                  pl.BlockSpec((D_MODEL, IN_PAD), lambda i: (0, 0), pipeline_mode=pl.Buffered(1))],
        out_specs=[pl.BlockSpec((tm, n), lambda i: (i, 0)) for n in widths],
        out_shape=[jax.ShapeDtypeStruct((t, n), F32) for n in widths],
        compiler_params=_cparams(1),
        name="in_proj",
    )(x2d, g.reshape(1, D_MODEL), w_p)


def _rope(x, cos_t, sin_t):
    lane = lax.broadcasted_iota(jnp.int32, x.shape, 1)
    nxt = pltpu.roll(x, LANES - 32, axis=1)
    prv = pltpu.roll(x, 32, axis=1)
    sw = jnp.where((lane % 64) < 32, nxt, prv)
    return x * cos_t + sw * sin_t


def _att_prep_kernel(u_ref, cos_ref, sin_ref, gq_ref, gk_ref, q_ref, k_ref, vt_ref):
    cos_t = cos_ref[...]
    sin_t = sin_ref[...]
    inv = 1.0 / ATT_HEAD_DIM
    qscale = (ATT_HEAD_DIM ** -0.5) * math.log2(math.e)
    for h in range(ATT_HEADS):
        x = u_ref[:, h * LANES:(h + 1) * LANES]
        ms = jnp.sum(x * x, axis=-1, keepdims=True) * inv
        xn = x * lax.rsqrt(ms + EPS) * gq_ref[...]
        q_ref[:, h * LANES:(h + 1) * LANES] = (_rope(xn, cos_t, sin_t) * qscale).astype(BF16)
    k = u_ref[:, ATT_Q_PAD:ATT_Q_PAD + LANES]
    lane = lax.broadcasted_iota(jnp.int32, k.shape, 1)
    k2 = k * k
    s_all = jnp.sum(k2, axis=-1, keepdims=True)
    s_lo = jnp.sum(jnp.where(lane < 64, k2, 0.0), axis=-1, keepdims=True)
    ms = jnp.where(lane < 64, s_lo, s_all - s_lo) * inv
    kn = k * lax.rsqrt(ms + EPS) * gk_ref[...]
    k_ref[...] = _rope(kn, cos_t, sin_t).astype(BF16)
    vt = u_ref[:, ATT_Q_PAD + LANES:ATT_PAD].T
    ones = jnp.ones((ATT_HEAD_DIM, vt.shape[1]), F32)
    for g in range(ATT_KV_HEADS):
        vt_ref[0, g] = jnp.concatenate(
            [vt[g * ATT_HEAD_DIM:(g + 1) * ATT_HEAD_DIM], ones], axis=0).astype(BF16)


def att_prep(u_att, cos_t, sin_t, gq, gk, batch, seq_len, tm=512):
    t = u_att.shape[0]
    nl = seq_len // tm
    return pl.pallas_call(
        _att_prep_kernel,
        grid=(t // tm,),
        in_specs=[pl.BlockSpec((tm, ATT_PAD), lambda i: (i, 0)),
                  pl.BlockSpec((tm, LANES), lambda i: (i % nl, 0)),
                  pl.BlockSpec((tm, LANES), lambda i: (i % nl, 0)),
                  pl.BlockSpec((1, LANES), lambda i: (0, 0)),
                  pl.BlockSpec((1, LANES), lambda i: (0, 0))],
        out_specs=[pl.BlockSpec((tm, ATT_Q_PAD), lambda i: (i, 0)),
                   pl.BlockSpec((tm, LANES), lambda i: (i, 0)),
                   pl.BlockSpec((1, ATT_KV_HEADS, LANES, tm), lambda i: (i // nl, 0, 0, i % nl))],
        out_shape=[jax.ShapeDtypeStruct((t, ATT_Q_PAD), BF16),
                   jax.ShapeDtypeStruct((t, LANES), BF16),
                   jax.ShapeDtypeStruct((batch, ATT_KV_HEADS, LANES, seq_len), BF16)],
        compiler_params=_cparams(1),
        name="att_prep",
    )(u_att, cos_t, sin_t, gq, gk)


def _flash_kernel(q_ref, k_ref, vt_ref, g_ref, o_ref, m_ref, acc_ref, st_ref, *, tk, nk):
    m_ref[...] = jnp.full(m_ref.shape, -jnp.inf, F32)
    acc_ref[...] = jnp.zeros(acc_ref.shape, F32)

    def scores(j, slot):
        off = pl.multiple_of(j * tk, tk)
        k = k_ref[0, pl.ds(off, tk), :]
        for h in range(ATT_HEADS):
            q = q_ref[0, :, h * LANES:(h + 1) * LANES]
            st_ref[slot, h] = lax.dot_general(k, q, NT, preferred_element_type=F32)

    def consume(j, slot):
        off = pl.multiple_of(j * tk, tk)
        for h in range(ATT_HEADS):
            st = st_ref[slot, h]
            m_prev = m_ref[h]
            m_new = jnp.maximum(m_prev, jnp.max(st, axis=0, keepdims=True))
            alpha = jnp.exp2(m_prev - m_new)
            p = jnp.exp2((st - m_new).astype(BF16))
            vt = vt_ref[0, h // 2, :, pl.ds(off, tk)]
            acc_ref[h] = alpha * acc_ref[h] + jnp.dot(vt, p, preferred_element_type=F32)
            m_ref[h] = m_new

    scores(0, 0)

    def body(jj, carry):
        j = 2 * jj
        scores(j + 1, 1)
        consume(j, 0)
        scores(jnp.minimum(j + 2, nk - 1), 0)
        consume(j + 1, 1)
        return carry

    lax.fori_loop(0, nk // 2, body, 0)
    ot = jnp.concatenate([acc_ref[h, 0:ATT_HEAD_DIM] / acc_ref[h, ATT_HEAD_DIM:LANES]
                          for h in range(ATT_HEADS)], axis=0)
    o = ot.T
    ms = jnp.mean(o * o, axis=-1, keepdims=True)
    o_ref[0] = (o * lax.rsqrt(ms + EPS) * g_ref[...]).astype(BF16)


def flash_attention(q, k, vt, g_out, tq=256, tk=512):
    b, seq_len, _ = q.shape
    return pl.pallas_call(
        functools.partial(_flash_kernel, tk=tk, nk=seq_len // tk),
        grid=(b, seq_len // tq),
        in_specs=[pl.BlockSpec((1, tq, ATT_Q_PAD), lambda bi, i: (bi, i, 0)),
                  pl.BlockSpec((1, seq_len, LANES), lambda bi, i: (bi, 0, 0)),
                  pl.BlockSpec((1, ATT_KV_HEADS, LANES, seq_len), lambda bi, i: (bi, 0, 0, 0)),
                  pl.BlockSpec((1, GROUP_W), lambda bi, i: (0, 0))],
        out_specs=pl.BlockSpec((1, tq, GROUP_W), lambda bi, i: (bi, i, 0)),
        out_shape=jax.ShapeDtypeStruct((b, seq_len, GROUP_W), BF16),
        scratch_shapes=[pltpu.VMEM((ATT_HEADS, 1, tq), F32),
                        pltpu.VMEM((ATT_HEADS, LANES, tq), F32),
                        pltpu.VMEM((2, ATT_HEADS, tk, tq), F32)],
        compiler_params=_cparams(2),
        name="flash_attention",
    )(q, k, vt, g_out.reshape(1, GROUP_W))


def _tri(n, upper, block=None):
    ii = lax.broadcasted_iota(jnp.int32, (n, n), 0)
    jj = lax.broadcasted_iota(jnp.int32, (n, n), 1)
    m = (jj >= ii) if upper else (jj <= ii)
    if block is not None:
        m = m & ((ii // block) == (jj // block))
    return m


def _ssd_fwd_kernel(u_ref, up_ref, un_ref, cw_ref, cb_ref, dtb_ref, ax_ref, dx_ref,
                    yp_ref, cm_ref, bm_ref, wxb_ref, eb_ref, s_ref, *, tm):
    Q = SSD_CHUNK
    i = pl.program_id(1)
    nb = pl.num_programs(1)

    @pl.when(i == 0)
    def _():
        s_ref[...] = jnp.zeros(s_ref.shape, F32)

    x = u_ref[:, SSD_INNER:SSD_INNER + SSD_XBC]
    prev_row = jnp.where(i > 0, up_ref[7:8, SSD_INNER:SSD_INNER + SSD_XBC], 0.0)
    next_row = jnp.where(i < nb - 1, un_ref[0:1, SSD_INNER:SSD_INNER + SSD_XBC], 0.0)
    rid = lax.broadcasted_iota(jnp.int32, x.shape, 0)
    xm = jnp.where(rid == 0, prev_row, pltpu.roll(x, 1, axis=0))
    xp = jnp.where(rid == tm - 1, next_row, pltpu.roll(x, tm - 1, axis=0))
    conv = xm * cw_ref[0:1] + x * cw_ref[1:2] + xp * cw_ref[2:3] + cb_ref[...]
    xbc = _silu(conv)
    xs = xbc[:, 0:SSD_INNER]
    bm = xbc[:, SSD_INNER:SSD_INNER + SSD_BC].astype(BF16)
    cm = xbc[:, SSD_INNER + SSD_BC:].astype(BF16)
    bm_ref[...] = bm
    cm_ref[...] = cm
    dt = _softplus(u_ref[:, SSD_INNER + SSD_XBC:] + dtb_ref[...])
    da = dt * ax_ref[...]

    tl = _tri(Q, False).astype(F32)
    tu = _tri(Q, True).astype(F32)
    ii = lax.broadcasted_iota(jnp.int32, (Q, Q), 0)
    jj = lax.broadcasted_iota(jnp.int32, (Q, Q), 1)
    lo_half = lax.broadcasted_iota(jnp.int32, (Q, LANES), 1) < SSD_HEAD_DIM
    ninf = jnp.float32(-jnp.inf)

    for c in range(tm // Q):
        r = slice(c * Q, (c + 1) * Q)
        cum = jnp.dot(tl, da[r, 0:SSD_INNER], precision=HIGHEST, preferred_element_type=F32)
        rc = jnp.dot(tu, da[r, SSD_INNER:], precision=HIGHEST, preferred_element_type=F32)
        for g in range(SSD_GROUPS):
            ln = slice(g * LANES, (g + 1) * LANES)
            cumg, rcg = cum[:, ln], rc[:, ln]
            dtf, dtbk = dt[r, ln], dt[r, SSD_INNER + g * LANES:SSD_INNER + (g + 1) * LANES]
            xsg = xs[r, ln]
            xsg_b = xsg.astype(BF16)
            bg, cg = bm[r, ln], cm[r, ln]
            cb = lax.dot_general(cg, bg, NT, preferred_element_type=F32)
            cum_t, rc_t, dtf_t, dtb_t = cumg.T, rcg.T, dtf.T, dtbk.T
            cum_sw, rc_sw = pltpu.roll(cumg, SSD_HEAD_DIM, axis=1), pltpu.roll(rcg, SSD_HEAD_DIM, axis=1)
            yd = []
            for hh in range(2):
                row = slice(hh * SSD_HEAD_DIM, hh * SSD_HEAD_DIM + 1)
                col_c = jnp.where(lo_half, cumg, cum_sw) if hh == 0 else jnp.where(lo_half, cum_sw, cumg)
                col_r = jnp.where(lo_half, rcg, rc_sw) if hh == 0 else jnp.where(lo_half, rc_sw, rcg)
                ef = jnp.exp(jnp.where(jj <= ii, col_c - cum_t[row], ninf)) * dtf_t[row]
                eb = jnp.exp(jnp.where(jj >= ii, col_r - rc_t[row], ninf)) * dtb_t[row]
                w = (cb * (ef + eb)).astype(BF16)
                yd.append(jnp.dot(w, xsg_b, preferred_element_type=F32))
            y = jnp.where(lo_half, yd[0], yd[1])
            last = cumg[Q - 1:Q]
            wxf = (xsg * jnp.exp(last - cumg) * dtf).astype(BF16)
            s_prev = s_ref[g]
            y = y + jnp.dot(cg, s_prev.astype(BF16), preferred_element_type=F32) * jnp.exp(cumg)
            s_ref[g] = s_prev * jnp.exp(last) + lax.dot_general(bg, wxf, TN, preferred_element_type=F32)
            yp_ref[r, ln] = y + dx_ref[:, ln] * xsg
            wxb_ref[r, ln] = (xsg * jnp.exp(rcg[0:1] - rcg) * dtbk).astype(BF16)
            eb_ref[r, ln] = jnp.exp(rcg)


def _ssd_bwd_kernel(yp_ref, cm_ref, bm_ref, wxb_ref, eb_ref, z_ref, g_ref, o_ref, s_ref, y_ref, *, tm):
    Q = SSD_CHUNK

    @pl.when(pl.program_id(1) == 0)
    def _():
        s_ref[...] = jnp.zeros(s_ref.shape, F32)

    for c in reversed(range(tm // Q)):
        r = slice(c * Q, (c + 1) * Q)
        for g in range(SSD_GROUPS):
            ln = slice(g * LANES, (g + 1) * LANES)
            ebg = eb_ref[r, ln]
            s_prev = s_ref[g]
            y_ref[r, ln] = yp_ref[r, ln] + jnp.dot(cm_ref[r, ln], s_prev.astype(BF16),
                                                   preferred_element_type=F32) * ebg
            s_ref[g] = s_prev * ebg[0:1] + lax.dot_general(bm_ref[r, ln], wxb_ref[r, ln], TN,
                                                           preferred_element_type=F32)
    y = y_ref[...] * _silu(z_ref[...])
    ms = jnp.mean(y * y, axis=-1, keepdims=True)
    o_ref[...] = (y * lax.rsqrt(ms + EPS) * g_ref[...]).astype(BF16)


def ssd_mixer(u_ssd, batch, seq_len, conv_w, conv_b, dt_bias_x, a_x, d_x, norm_g, tm=512):
    t = u_ssd.shape[0]
    nb = seq_len // tm
    hb = tm // 8
    last_halo = t // 8 - 1
    row = lambda b, i: (b * nb + i, 0)
    const = lambda b, i: (0, 0)
    outs = pl.pallas_call(
        functools.partial(_ssd_fwd_kernel, tm=tm),
        grid=(batch, nb),
        in_specs=[pl.BlockSpec((tm, SSD_PAD), row),
                  pl.BlockSpec((8, SSD_PAD), lambda b, i: (jnp.maximum((b * nb + i) * hb - 1, 0), 0)),
                  pl.BlockSpec((8, SSD_PAD), lambda b, i: (jnp.minimum((b * nb + i + 1) * hb, last_halo), 0)),
                  pl.BlockSpec((3, SSD_XBC), const),
                  pl.BlockSpec((1, SSD_XBC), const),
                  pl.BlockSpec((1, SSD_DTX), const),
                  pl.BlockSpec((1, SSD_DTX), const),
                  pl.BlockSpec((1, SSD_INNER), const)],
        out_specs=[pl.BlockSpec((tm, SSD_INNER), row)] * 5,
        out_shape=[jax.ShapeDtypeStruct((t, SSD_INNER), F32),
                   jax.ShapeDtypeStruct((t, SSD_INNER), BF16),
                   jax.ShapeDtypeStruct((t, SSD_INNER), BF16),
                   jax.ShapeDtypeStruct((t, SSD_INNER), BF16),
                   jax.ShapeDtypeStruct((t, SSD_INNER), F32)],
        scratch_shapes=[pltpu.VMEM((SSD_GROUPS, SSD_STATE, LANES), F32)],
        compiler_params=_cparams(2),
        name="ssd_fwd",
    )(u_ssd, u_ssd, u_ssd, conv_w, conv_b.reshape(1, SSD_XBC), dt_bias_x, a_x, d_x)
    rrow = lambda b, i: (b * nb + nb - 1 - i, 0)
    return pl.pallas_call(
        functools.partial(_ssd_bwd_kernel, tm=tm),
        grid=(batch, nb),
        in_specs=[pl.BlockSpec((tm, SSD_INNER), rrow)] * 6 + [pl.BlockSpec((1, SSD_INNER), const)],
        out_specs=pl.BlockSpec((tm, SSD_INNER), rrow),
        out_shape=jax.ShapeDtypeStruct((t, SSD_INNER), BF16),
        scratch_shapes=[pltpu.VMEM((SSD_GROUPS, SSD_STATE, LANES), F32),
                        pltpu.VMEM((tm, SSD_INNER), F32)],
        compiler_params=_cparams(2),
        name="ssd_bwd",
    )(*outs, u_ssd, norm_g.reshape(1, SSD_INNER))


def _gla_fwd_kernel(u_ref, wg_ref, bg_ref, op_ref, qb_ref, kb_ref, db_ref, st_ref, *, tm):
    Q = GLA_CHUNK
    P = 2 * Q

    @pl.when(pl.program_id(1) == 0)
    def _():
        st_ref[...] = jnp.zeros(st_ref.shape, F32)

    pre = jnp.dot(u_ref[:, GLA_LR0:GLA_PAD], wg_ref[...], precision=HIGHEST,
                  preferred_element_type=F32) + bg_ref[...]
    gk = (jnp.minimum(pre, 0.0) - jnp.log1p(jnp.exp(-jnp.abs(pre)))) * (1.0 / GLA_NORMALIZER)

    mask_f = _tri(P, False, Q)
    mask_b = _tri(P, True, Q)
    tl = mask_f.astype(F32)
    tu = mask_b.astype(F32)
    lane_k = lax.broadcasted_iota(jnp.int32, (P, GLA_QK), 1) // GLA_DK
    lane_v = lax.broadcasted_iota(jnp.int32, (P, GLA_V), 1) // GLA_DV
    bd_t = (lax.broadcasted_iota(jnp.int32, (GLA_V, GLA_QK), 0) // GLA_DV ==
            lax.broadcasted_iota(jnp.int32, (GLA_V, GLA_QK), 1) // GLA_DK)

    for c in range(tm // P):
        r = slice(c * P, (c + 1) * P)
        gf = jnp.dot(tl, gk[r, 0:GLA_QK], precision=HIGHEST, preferred_element_type=F32)
        gb = jnp.dot(tu, gk[r, GLA_QK:], precision=HIGHEST, preferred_element_type=F32)
        qs = u_ref[r, 0:GLA_QK] * (GLA_DK ** -0.5)
        k = u_ref[r, GLA_QK:2 * GLA_QK]
        v = u_ref[r, 2 * GLA_QK:2 * GLA_QK + GLA_V]
        vb = v.astype(BF16)
        q_f = qs * jnp.exp(gf)
        k_f = (k * jnp.exp(-gf)).astype(BF16)
        q_b = qs * jnp.exp(gb)
        k_b = (k * jnp.exp(-gb)).astype(BF16)
        o = jnp.zeros((P, GLA_V), F32)
        for h in range(GLA_HEADS):
            hm = lane_k == h
            af = lax.dot_general(jnp.where(hm, q_f, 0.0).astype(BF16), k_f, NT, preferred_element_type=F32)
            ab = lax.dot_general(jnp.where(hm, q_b, 0.0).astype(BF16), k_b, NT, preferred_element_type=F32)
            att = jnp.where(mask_f, af, 0.0) + jnp.where(mask_b, ab, 0.0)
            vh = jnp.where(lane_v == h, v, 0.0).astype(BF16)
            o = o + jnp.dot(att.astype(BF16), vh, preferred_element_type=F32)
        q_fb = q_f.astype(BF16)
        for cc in range(2):
            rr = slice(cc * Q, (cc + 1) * Q)
            ro = slice(c * P + cc * Q, c * P + (cc + 1) * Q)
            st = st_ref[...]
            g_last = gf[cc * Q + Q - 1:cc * Q + Q]
            k_end = (k[rr] * jnp.exp(g_last - gf[rr])).astype(BF16)
            op_ref[ro] = o[rr] + lax.dot_general(q_fb[rr], st.astype(BF16), NT, preferred_element_type=F32)
            upd = lax.dot_general(vb[rr], k_end, TN, preferred_element_type=F32)
            st_ref[...] = st * jnp.exp(g_last) + jnp.where(bd_t, upd, 0.0)
            g_first = gb[cc * Q:cc * Q + 1]
            kb_ref[ro] = (k[rr] * jnp.exp(g_first - gb[rr])).astype(BF16)
        qb_ref[r] = q_b.astype(BF16)
        db_ref[r] = jnp.exp(gb)


def _gla_bwd_kernel(op_ref, qb_ref, kb_ref, db_ref, v_ref, go_ref, g_ref, o_ref, st_ref, y_ref, *, tm):
    Q = GLA_CHUNK

    @pl.when(pl.program_id(1) == 0)
    def _():
        st_ref[...] = jnp.zeros(st_ref.shape, F32)

    bd_t = (lax.broadcasted_iota(jnp.int32, (GLA_V, GLA_QK), 0) // GLA_DV ==
            lax.broadcasted_iota(jnp.int32, (GLA_V, GLA_QK), 1) // GLA_DK)
    for c in reversed(range(tm // Q)):
        r = slice(c * Q, (c + 1) * Q)
        st = st_ref[...]
        y_ref[r] = op_ref[r] + lax.dot_general(qb_ref[r], st.astype(BF16), NT, preferred_element_type=F32)
        upd = lax.dot_general(v_ref[r].astype(BF16), kb_ref[r], TN, preferred_element_type=F32)
        st_ref[...] = st * db_ref[c * Q:c * Q + 1] + jnp.where(bd_t, upd, 0.0)
    y = y_ref[...]
    seg = (lax.broadcasted_iota(jnp.int32, (GLA_V, GLA_V), 0) // GLA_DV ==
           lax.broadcasted_iota(jnp.int32, (GLA_V, GLA_V), 1) // GLA_DV).astype(F32)
    ms = jnp.dot(y * y, seg, precision=HIGHEST, preferred_element_type=F32) * (1.0 / GLA_DV)
    o_ref[...] = (y * lax.rsqrt(ms + EPS) * g_ref[...] * _silu(go_ref[...])).astype(BF16)


def gla_mixer(u_gla, batch, seq_len, wg, bg, norm_g_x, tm=512):
    t = u_gla.shape[0]
    nb = seq_len // tm
    row = lambda b, i: (b * nb + i, 0)
    const = lambda b, i: (0, 0)
    outs = pl.pallas_call(
        functools.partial(_gla_fwd_kernel, tm=tm),
        grid=(batch, nb),
        in_specs=[pl.BlockSpec((tm, GLA_PAD), row),
                  pl.BlockSpec((GLA_PAD - GLA_LR0, 2 * GLA_QK), const),
                  pl.BlockSpec((1, 2 * GLA_QK), const)],
        out_specs=[pl.BlockSpec((tm, GLA_V), row), pl.BlockSpec((tm, GLA_QK), row),
                   pl.BlockSpec((tm, GLA_QK), row), pl.BlockSpec((tm, GLA_QK), row)],
        out_shape=[jax.ShapeDtypeStruct((t, GLA_V), F32),
                   jax.ShapeDtypeStruct((t, GLA_QK), BF16),
                   jax.ShapeDtypeStruct((t, GLA_QK), BF16),
                   jax.ShapeDtypeStruct((t, GLA_QK), F32)],
        scratch_shapes=[pltpu.VMEM((GLA_V, GLA_QK), F32)],
        compiler_params=_cparams(2),
        name="gla_fwd",
    )(u_gla, wg, bg)
    rrow = lambda b, i: (b * nb + nb - 1 - i, 0)
    return pl.pallas_call(
        functools.partial(_gla_bwd_kernel, tm=tm),
        grid=(batch, nb),
        in_specs=[pl.BlockSpec((tm, GLA_V), rrow), pl.BlockSpec((tm, GLA_QK), rrow),
                  pl.BlockSpec((tm, GLA_QK), rrow), pl.BlockSpec((tm, GLA_QK), rrow),
                  pl.BlockSpec((tm, GLA_V), lambda b, i: (b * nb + nb - 1 - i, 1)),
                  pl.BlockSpec((tm, GLA_V), lambda b, i: (b * nb + nb - 1 - i, 2)),
                  pl.BlockSpec((1, GLA_V), const)],
        out_specs=pl.BlockSpec((tm, GLA_V), rrow),
        out_shape=jax.ShapeDtypeStruct((t, GLA_V), BF16),
        scratch_shapes=[pltpu.VMEM((GLA_V, GLA_QK), F32), pltpu.VMEM((tm, GLA_V), F32)],
        compiler_params=_cparams(2),
        name="gla_bwd",
    )(*outs, u_gla, u_gla, norm_g_x)


def _out_ffn_kernel(x_ref, y0_ref, y1_ref, y2_ref, y3_ref, wo_ref, g2_ref, wu_ref, wd_ref, o_ref, *, tf):
    mixed = jnp.concatenate([y0_ref[...], y1_ref[...], y2_ref[...], y3_ref[...]], axis=1)
    x1 = x_ref[...] + jnp.dot(mixed, wo_ref[...], preferred_element_type=F32)
    ms = jnp.mean(x1 * x1, axis=-1, keepdims=True)
    h2 = (x1 * lax.rsqrt(ms + EPS) * g2_ref[...]).astype(BF16)
    acc = x1
    for c in range(D_FF // tf):
        up = jnp.dot(h2, wu_ref[:, c * tf:(c + 1) * tf], preferred_element_type=F32)
        a = jnp.square(jnp.maximum(up, 0.0)).astype(BF16)
        acc = acc + jnp.dot(a, wd_ref[c * tf:(c + 1) * tf, :], preferred_element_type=F32)
    o_ref[...] = acc


def out_ffn(x2d, ys, wo, g2, wu, wd, tm=512, tf=1024):
    t = x2d.shape[0]
    const = lambda i: (0, 0)
    single = pl.Buffered(1)
    return pl.pallas_call(
        functools.partial(_out_ffn_kernel, tf=tf),
        grid=(t // tm,),
        in_specs=[pl.BlockSpec((tm, D_MODEL), lambda i: (i, 0))] +
                 [pl.BlockSpec((tm, GROUP_W), lambda i: (i, 0))] * 4 +
                 [pl.BlockSpec((D_MODEL, D_MODEL), const, pipeline_mode=single),
                  pl.BlockSpec((1, D_MODEL), const),
                  pl.BlockSpec((D_MODEL, D_FF), const, pipeline_mode=single),
                  pl.BlockSpec((D_FF, D_MODEL), const, pipeline_mode=single)],
        out_specs=pl.BlockSpec((tm, D_MODEL), lambda i: (i, 0)),
        out_shape=jax.ShapeDtypeStruct((t, D_MODEL), F32),
        compiler_params=_cparams(1),
        name="out_ffn",
    )(x2d, *ys, wo, g2.reshape(1, D_MODEL), wu, wd)


def _rmsnorm(x, g):
    return x * lax.rsqrt(jnp.mean(x * x, axis=-1, keepdims=True) + EPS) * g


def _dwconv(u, w, b):
    k, c = w.shape
    y = lax.conv_general_dilated(u, w[:, None, :], window_strides=(1,), padding=[(k // 2, k // 2)],
                                 dimension_numbers=('NWC', 'WIO', 'NWC'), feature_group_count=c)
    return y + b


def _hyena_filter_freq(L, w1, b1, w2, b2, w3, freq):
    t = jnp.linspace(0.0, 1.0, L, dtype=F32)[:, None]
    w = 2.0 * math.pi * jnp.arange(L, dtype=F32) / L
    bands = jnp.linspace(1e-4, HY_BANDS - 1, HY_BANDS, dtype=F32)
    ang = w[:, None] * bands
    z = jnp.concatenate([t, jnp.cos(ang), -jnp.sin(ang)], axis=-1)
    hid = jnp.sin(freq * (z @ w1 + b1))
    hid = jnp.sin(freq * (hid @ w2 + b2))
    h = (hid @ w3).reshape(L, HY_ORDER, 2, HY_C)
    deltas = jnp.linspace(HY_MIN_DECAY, HY_MAX_DECAY, HY_C, dtype=F32)
    h = h * jnp.exp(-t * jnp.abs(deltas))[:, None, None, :]
    h = h / jnp.sum(jnp.abs(h), axis=(0, 2), keepdims=True)
    h_f, h_b = h[:, :, 0], h[:, :, 1]
    two_sided = jnp.concatenate([h_f[:1] + h_b[:1], h_f[1:], jnp.zeros_like(h_f[:1]),
                                 jnp.flip(h_b[1:], axis=0)], axis=0)
    return jnp.fft.rfft(two_sided, axis=0)


def _fft_long_conv(u, hf):
    L = u.shape[1]
    U = jnp.fft.rfft(u, n=2 * L, axis=1)
    return jnp.fft.irfft(U * hf[None], n=2 * L, axis=1)[:, :L]


def _hyena_mixer(u, conv_w, conv_b, hf, skip):
    uc = _dwconv(u, conv_w, conv_b)
    v, g1, g2 = jnp.split(uc, 3, axis=-1)
    z = v
    for o, gate in enumerate((g1, g2)):
        z = gate * (_fft_long_conv(z, hf[:, o]) + z * skip[o])
    return z


_DEINTERLEAVE = np.concatenate([np.arange(0, ATT_HEAD_DIM, 2), np.arange(1, ATT_HEAD_DIM, 2)])


def _pack_w_in(w):
    c0 = 0
    w_hy = w[:, c0:c0 + HY_IN]; c0 += HY_IN
    w_ssd = w[:, c0:c0 + SSD_IN]; c0 += SSD_IN
    w_gla = w[:, c0:c0 + GLA_IN]; c0 += GLA_IN
    w_att = w[:, c0:c0 + ATT_IN]
    zeros = lambda n: jnp.zeros((D_MODEL, n), w.dtype)
    cols = [w_hy, w_ssd[:, :SSD_INNER + SSD_XBC],
            jnp.repeat(w_ssd[:, SSD_INNER + SSD_XBC:], SSD_HEAD_DIM, axis=1),
            w_gla, zeros(GLA_PAD - GLA_IN)]
    for h in range(ATT_HEADS):
        wh = w_att[:, h * ATT_HEAD_DIM:(h + 1) * ATT_HEAD_DIM][:, _DEINTERLEAVE]
        cols += [wh, zeros(64)] if h < 2 else [zeros(64), wh]
    kq = ATT_HEADS * ATT_HEAD_DIM
    for g in range(ATT_KV_HEADS):
        cols.append(w_att[:, kq + g * ATT_HEAD_DIM:kq + (g + 1) * ATT_HEAD_DIM][:, _DEINTERLEAVE])
    cols.append(w_att[:, kq + ATT_KV_HEADS * ATT_HEAD_DIM:])
    return jnp.concatenate(cols, axis=1).astype(BF16)


def _rope_tables(seq_len):
    rows = seq_len // GRID_W
    row = jnp.repeat(jnp.arange(rows, dtype=F32), GRID_W)
    col = jnp.tile(jnp.arange(GRID_W, dtype=F32), rows)
    inv_freq = 1.0 / (ROPE_THETA ** (jnp.arange(0, ROPE_AXIS_DIM, 2, dtype=F32) / ROPE_AXIS_DIM))
    ang = jnp.concatenate([row[:, None] * inv_freq, col[:, None] * inv_freq], axis=-1)
    cos, sin = jnp.cos(ang), jnp.sin(ang)
    return jnp.tile(cos, (1, 4)), jnp.tile(jnp.concatenate([-sin, sin], axis=-1), (1, 2))


def _layer_params(i, s):
    p = {name: val[i] for name, val in s.items()}
    p['w_in_p'] = _pack_w_in(p['w_in'])
    p['w_out_b'] = p['w_out'].astype(BF16)
    p['w_up_b'] = p['w_up'].astype(BF16)
    p['w_down_b'] = p['w_down'].astype(BF16)
    p['ssd_dt_bias_x'] = jnp.repeat(p['ssd_dt_bias'].reshape(-1), SSD_HEAD_DIM).reshape(1, SSD_DTX)
    p['ssd_a_x'] = jnp.repeat(-jnp.exp(p['ssd_A_log']).reshape(-1), SSD_HEAD_DIM).reshape(1, SSD_DTX)
    p['ssd_d_x'] = jnp.repeat(p['ssd_D'], SSD_HEAD_DIM).reshape(1, SSD_INNER)
    wg = jnp.zeros((GLA_PAD - GLA_LR0, 2 * GLA_QK), F32)
    wg = wg.at[0:GLA_RANK, 0:GLA_QK].set(p['gla_gk_w'][0])
    wg = wg.at[GLA_RANK:2 * GLA_RANK, GLA_QK:].set(p['gla_gk_w'][1])
    p['gla_wg'] = wg
    p['gla_bg'] = p['gla_gk_b'].reshape(1, 2 * GLA_QK)
    p['gla_norm_g_x'] = jnp.tile(p['gla_norm_g'], GLA_HEADS).reshape(1, GLA_V)
    p['att_gq'] = jnp.tile(p['att_q_norm_g'][_DEINTERLEAVE], 2).reshape(1, LANES)
    p['att_gk'] = jnp.tile(p['att_k_norm_g'][_DEINTERLEAVE], 2).reshape(1, LANES)
    return p


def _layer(x, tables, p):
    b, L, _ = x.shape
    x2d = x.reshape(b * L, D_MODEL)
    u_hy, u_ssd, u_gla, u_att = in_proj(x2d, p['ln1_g'], p['w_in_p'])

    hf = _hyena_filter_freq(L, p['hy_ffn_w1'], p['hy_ffn_b1'], p['hy_ffn_w2'], p['hy_ffn_b2'],
                            p['hy_ffn_w3'], p['hy_sin_freq'])
    y_hy = _rmsnorm(_hyena_mixer(u_hy.reshape(b, L, HY_IN), p['hy_conv_w'], p['hy_conv_b'], hf,
                                 p['hy_skip']), p['hy_out_g']).astype(BF16).reshape(b * L, GROUP_W)
    y_ssd = ssd_mixer(u_ssd, b, L, p['ssd_conv_w'], p['ssd_conv_b'], p['ssd_dt_bias_x'], p['ssd_a_x'],
                      p['ssd_d_x'], p['ssd_norm_g'])
    y_gla = gla_mixer(u_gla, b, L, p['gla_wg'], p['gla_bg'], p['gla_norm_g_x'])
    q, k, vt = att_prep(u_att, tables[0], tables[1], p['att_gq'], p['att_gk'], b, L)
    y_att = flash_attention(q.reshape(b, L, ATT_Q_PAD), k.reshape(b, L, LANES), vt, p['att_out_g'])

    y = out_ffn(x2d, (y_hy, y_ssd, y_gla, y_att.reshape(b * L, GROUP_W)), p['w_out_b'], p['ln2_g'],
                p['w_up_b'], p['w_down_b'])
    return y.reshape(b, L, D_MODEL)


def kernel(x_prompt, x_sample, ln1_g, w_in, hy_conv_w, hy_conv_b, hy_ffn_w1, hy_ffn_b1, hy_ffn_w2, hy_ffn_b2, hy_ffn_w3, hy_sin_freq, hy_skip, hy_out_g, ssd_conv_w, ssd_conv_b, ssd_A_log, ssd_dt_bias, ssd_D, ssd_norm_g, gla_gk_w, gla_gk_b, gla_norm_g, att_q_norm_g, att_k_norm_g, att_out_g, w_out, ln2_g, w_up, w_down):
    stacked = dict(ln1_g=ln1_g, w_in=w_in, hy_conv_w=hy_conv_w, hy_conv_b=hy_conv_b, hy_ffn_w1=hy_ffn_w1,
                   hy_ffn_b1=hy_ffn_b1, hy_ffn_w2=hy_ffn_w2, hy_ffn_b2=hy_ffn_b2, hy_ffn_w3=hy_ffn_w3,
                   hy_sin_freq=hy_sin_freq, hy_skip=hy_skip, hy_out_g=hy_out_g, ssd_conv_w=ssd_conv_w,
                   ssd_conv_b=ssd_conv_b, ssd_A_log=ssd_A_log, ssd_dt_bias=ssd_dt_bias, ssd_D=ssd_D,
                   ssd_norm_g=ssd_norm_g, gla_gk_w=gla_gk_w, gla_gk_b=gla_gk_b, gla_norm_g=gla_norm_g,
                   att_q_norm_g=att_q_norm_g, att_k_norm_g=att_k_norm_g, att_out_g=att_out_g,
                   w_out=w_out, ln2_g=ln2_g, w_up=w_up, w_down=w_down)
    tables_p = _rope_tables(x_prompt.shape[1])
    tables_s = _rope_tables(x_sample.shape[1])
    y_prompt, y_sample = x_prompt, x_sample
    for i in range(DEPTH):
        p = _layer_params(i, stacked)
        y_prompt = _layer(y_prompt, tables_p, p)
        y_sample = _layer(y_sample, tables_s, p)
    return (y_prompt, y_sample)
```

```python
import functools
import math

import jax
import jax.numpy as jnp
import numpy as np
from jax import lax
from jax.experimental import pallas as pl
from jax.experimental.pallas import tpu as pltpu

D_MODEL = 1024
DEPTH = 4
GRID_W = 64
EPS = 1e-6
GROUP_W = 256
D_FF = 4096

HY_C = 256
HY_ORDER = 2
HY_BANDS = 8
HY_MIN_DECAY = math.log(1e-2) / 1.5
HY_MAX_DECAY = math.log(1e-2) / 0.3
HY_IN = 768

SSD_HEAD_DIM = 64
SSD_HEADS = 4
SSD_GROUPS = 2
SSD_STATE = 128
SSD_CHUNK = 128
SSD_INNER = 256
SSD_BC = 256
SSD_XBC = 768
SSD_IN = 1032

GLA_HEADS = 4
GLA_DV = 64
GLA_DK = 32
GLA_RANK = 16
GLA_NORMALIZER = 16.0
GLA_CHUNK = 64
GLA_QK = 128
GLA_V = 256
GLA_IN = 800

ATT_HEAD_DIM = 64
ATT_HEADS = 4
ATT_KV_HEADS = 2
ROPE_THETA = 10000.0
ROPE_AXIS_DIM = 32
ATT_IN = 512

LANES = 128
SSD_DTX = 2 * SSD_HEADS * SSD_HEAD_DIM
SSD_PAD = SSD_INNER + SSD_XBC + SSD_DTX
GLA_PAD = 896
GLA_LR0 = 2 * GLA_QK + 2 * GLA_V
ATT_Q_PAD = ATT_HEADS * LANES
ATT_PAD = ATT_Q_PAD + 2 * LANES
IN_PAD = HY_IN + SSD_PAD + GLA_PAD + ATT_PAD

VMEM_LIMIT = 48 * 1024 * 1024
F32 = jnp.float32
BF16 = jnp.bfloat16
HIGHEST = lax.Precision.HIGHEST
NT = (((1,), (1,)), ((), ()))
TN = (((0,), (0,)), ((), ()))


def _cparams(n_grid):
    return pltpu.CompilerParams(dimension_semantics=("arbitrary",) * n_grid,
                                vmem_limit_bytes=VMEM_LIMIT)


def _softplus(x):
    return jnp.maximum(x, 0.0) + jnp.log1p(jnp.exp(-jnp.abs(x)))


def _silu(x):
    return x * jax.nn.sigmoid(x)


def _in_proj_kernel(x_ref, g_ref, w_ref, hy_ref, ssd_ref, gla_ref, att_ref):
    x = x_ref[...]
    ms = jnp.mean(x * x, axis=-1, keepdims=True)
    h = (x * lax.rsqrt(ms + EPS) * g_ref[...]).astype(BF16)
    c0 = 0
    for ref in (hy_ref, ssd_ref, gla_ref, att_ref):
        n = ref.shape[-1]
        ref[...] = jnp.dot(h, w_ref[:, c0:c0 + n], preferred_element_type=F32)
        c0 += n


def in_proj(x2d, g, w_p, tm=256):
    t = x2d.shape[0]
    widths = (HY_IN, SSD_PAD, GLA_PAD, ATT_PAD)
    return pl.pallas_call(
        _in_proj_kernel,
        grid=(t // tm,),
        in_specs=[pl.BlockSpec((tm, D_MODEL), lambda i: (i, 0)),
                  pl.BlockSpec((1, D_MODEL), lambda i: (0, 0)),
                  pl.BlockSpec((D_MODEL, IN_PAD), lambda i: (0, 0), pipeline_mode=pl.Buffered(1))],
        out_specs=[pl.BlockSpec((tm, n), lambda i: (i, 0)) for n in widths],
        out_shape=[jax.ShapeDtypeStruct((t, n), F32) for n in widths],
        compiler_params=_cparams(1),
        name="in_proj",
    )(x2d, g.reshape(1, D_MODEL), w_p)


def _rope(x, cos_t, sin_t):
    lane = lax.broadcasted_iota(jnp.int32, x.shape, 1)
    nxt = pltpu.roll(x, LANES - 32, axis=1)
    prv = pltpu.roll(x, 32, axis=1)
    sw = jnp.where((lane % 64) < 32, nxt, prv)
    return x * cos_t + sw * sin_t


def _att_prep_kernel(u_ref, cos_ref, sin_ref, gq_ref, gk_ref, q_ref, k_ref, vt_ref):
    cos_t = cos_ref[...]
    sin_t = sin_ref[...]
    inv = 1.0 / ATT_HEAD_DIM
    qscale = (ATT_HEAD_DIM ** -0.5) * math.log2(math.e)
    for h in range(ATT_HEADS):
        x = u_ref[:, h * LANES:(h + 1) * LANES]
        ms = jnp.sum(x * x, axis=-1, keepdims=True) * inv
        xn = x * lax.rsqrt(ms + EPS) * gq_ref[...]
        q_ref[:, h * LANES:(h + 1) * LANES] = (_rope(xn, cos_t, sin_t) * qscale).astype(BF16)
    k = u_ref[:, ATT_Q_PAD:ATT_Q_PAD + LANES]
    lane = lax.broadcasted_iota(jnp.int32, k.shape, 1)
    k2 = k * k
    s_all = jnp.sum(k2, axis=-1, keepdims=True)
    s_lo = jnp.sum(jnp.where(lane < 64, k2, 0.0), axis=-1, keepdims=True)
    ms = jnp.where(lane < 64, s_lo, s_all - s_lo) * inv
    kn = k * lax.rsqrt(ms + EPS) * gk_ref[...]
    k_ref[...] = _rope(kn, cos_t, sin_t).astype(BF16)
    vt = u_ref[:, ATT_Q_PAD + LANES:ATT_PAD].T
    ones = jnp.ones((ATT_HEAD_DIM, vt.shape[1]), F32)
    for g in range(ATT_KV_HEADS):
        vt_ref[0, g] = jnp.concatenate(
            [vt[g * ATT_HEAD_DIM:(g + 1) * ATT_HEAD_DIM], ones], axis=0).astype(BF16)


def att_prep(u_att, cos_t, sin_t, gq, gk, batch, seq_len, tm=512):
    t = u_att.shape[0]
    nl = seq_len // tm
    return pl.pallas_call(
        _att_prep_kernel,
        grid=(t // tm,),
        in_specs=[pl.BlockSpec((tm, ATT_PAD), lambda i: (i, 0)),
                  pl.BlockSpec((tm, LANES), lambda i: (i % nl, 0)),
                  pl.BlockSpec((tm, LANES), lambda i: (i % nl, 0)),
                  pl.BlockSpec((1, LANES), lambda i: (0, 0)),
                  pl.BlockSpec((1, LANES), lambda i: (0, 0))],
        out_specs=[pl.BlockSpec((tm, ATT_Q_PAD), lambda i: (i, 0)),
                   pl.BlockSpec((tm, LANES), lambda i: (i, 0)),
                   pl.BlockSpec((1, ATT_KV_HEADS, LANES, tm), lambda i: (i // nl, 0, 0, i % nl))],
        out_shape=[jax.ShapeDtypeStruct((t, ATT_Q_PAD), BF16),
                   jax.ShapeDtypeStruct((t, LANES), BF16),
                   jax.ShapeDtypeStruct((batch, ATT_KV_HEADS, LANES, seq_len), BF16)],
        compiler_params=_cparams(1),
        name="att_prep",
    )(u_att, cos_t, sin_t, gq, gk)


def _flash_kernel(q_ref, k_ref, vt_ref, g_ref, o_ref, m_ref, acc_ref, st_ref, *, tk, nk):
    m_ref[...] = jnp.full(m_ref.shape, -jnp.inf, F32)
    acc_ref[...] = jnp.zeros(acc_ref.shape, F32)

    def scores(j, slot):
        off = pl.multiple_of(j * tk, tk)
        k = k_ref[0, pl.ds(off, tk), :]
        for h in range(ATT_HEADS):
            q = q_ref[0, :, h * LANES:(h + 1) * LANES]
            st_ref[slot, h] = lax.dot_general(k, q, NT, preferred_element_type=F32)

    def consume(j, slot):
        off = pl.multiple_of(j * tk, tk)
        for h in range(ATT_HEADS):
            st = st_ref[slot, h]
            m_prev = m_ref[h]
            m_new = jnp.maximum(m_prev, jnp.max(st, axis=0, keepdims=True))
            alpha = jnp.exp2(m_prev - m_new)
            p = jnp.exp2((st - m_new).astype(BF16))
            vt = vt_ref[0, h // 2, :, pl.ds(off, tk)]
            acc_ref[h] = alpha * acc_ref[h] + jnp.dot(vt, p, preferred_element_type=F32)
            m_ref[h] = m_new

    scores(0, 0)

    def body(jj, carry):
        j = 2 * jj
        scores(j + 1, 1)
        consume(j, 0)
        scores(jnp.minimum(j + 2, nk - 1), 0)
        consume(j + 1, 1)
        return carry

    lax.fori_loop(0, nk // 2, body, 0)
    ot = jnp.concatenate([acc_ref[h, 0:ATT_HEAD_DIM] / acc_ref[h, ATT_HEAD_DIM:LANES]
                          for h in range(ATT_HEADS)], axis=0)
    o = ot.T
    ms = jnp.mean(o * o, axis=-1, keepdims=True)
    o_ref[0] = (o * lax.rsqrt(ms + EPS) * g_ref[...]).astype(BF16)


def flash_attention(q, k, vt, g_out, tq=256, tk=512):
    b, seq_len, _ = q.shape
    return pl.pallas_call(
        functools.partial(_flash_kernel, tk=tk, nk=seq_len // tk),
        grid=(b, seq_len // tq),
        in_specs=[pl.BlockSpec((1, tq, ATT_Q_PAD), lambda bi, i: (bi, i, 0)),
                  pl.BlockSpec((1, seq_len, LANES), lambda bi, i: (bi, 0, 0)),
                  pl.BlockSpec((1, ATT_KV_HEADS, LANES, seq_len), lambda bi, i: (bi, 0, 0, 0)),
                  pl.BlockSpec((1, GROUP_W), lambda bi, i: (0, 0))],
        out_specs=pl.BlockSpec((1, tq, GROUP_W), lambda bi, i: (bi, i, 0)),
        out_shape=jax.ShapeDtypeStruct((b, seq_len, GROUP_W), BF16),
        scratch_shapes=[pltpu.VMEM((ATT_HEADS, 1, tq), F32),
                        pltpu.VMEM((ATT_HEADS, LANES, tq), F32),
                        pltpu.VMEM((2, ATT_HEADS, tk, tq), F32)],
        compiler_params=_cparams(2),
        name="flash_attention",
    )(q, k, vt, g_out.reshape(1, GROUP_W))


def _tri(n, upper, block=None):
    ii = lax.broadcasted_iota(jnp.int32, (n, n), 0)
    jj = lax.broadcasted_iota(jnp.int32, (n, n), 1)
    m = (jj >= ii) if upper else (jj <= ii)
    if block is not None:
        m = m & ((ii // block) == (jj // block))
    return m


def _ssd_fwd_kernel(u_ref, up_ref, un_ref, cw_ref, cb_ref, dtb_ref, ax_ref, dx_ref,
                    yp_ref, cm_ref, bm_ref, wxb_ref, eb_ref, s_ref, *, tm):
    Q = SSD_CHUNK
    i = pl.program_id(1)
    nb = pl.num_programs(1)

    @pl.when(i == 0)
    def _():
        s_ref[...] = jnp.zeros(s_ref.shape, F32)

    x = u_ref[:, SSD_INNER:SSD_INNER + SSD_XBC]
    prev_row = jnp.where(i > 0, up_ref[7:8, SSD_INNER:SSD_INNER + SSD_XBC], 0.0)
    next_row = jnp.where(i < nb - 1, un_ref[0:1, SSD_INNER:SSD_INNER + SSD_XBC], 0.0)
    rid = lax.broadcasted_iota(jnp.int32, x.shape, 0)
    xm = jnp.where(rid == 0, prev_row, pltpu.roll(x, 1, axis=0))
    xp = jnp.where(rid == tm - 1, next_row, pltpu.roll(x, tm - 1, axis=0))
    conv = xm * cw_ref[0:1] + x * cw_ref[1:2] + xp * cw_ref[2:3] + cb_ref[...]
    xbc = _silu(conv)
    xs = xbc[:, 0:SSD_INNER]
    bm = xbc[:, SSD_INNER:SSD_INNER + SSD_BC].astype(BF16)
    cm = xbc[:, SSD_INNER + SSD_BC:].astype(BF16)
    bm_ref[...] = bm
    cm_ref[...] = cm
    dt = _softplus(u_ref[:, SSD_INNER + SSD_XBC:] + dtb_ref[...])
    da = dt * ax_ref[...]

    tl = _tri(Q, False).astype(F32)
    tu = _tri(Q, True).astype(F32)
    ii = lax.broadcasted_iota(jnp.int32, (Q, Q), 0)
    jj = lax.broadcasted_iota(jnp.int32, (Q, Q), 1)
    lo_half = lax.broadcasted_iota(jnp.int32, (Q, LANES), 1) < SSD_HEAD_DIM
    ninf = jnp.float32(-jnp.inf)

    for c in range(tm // Q):
        r = slice(c * Q, (c + 1) * Q)
        cum = jnp.dot(tl, da[r, 0:SSD_INNER], precision=HIGHEST, preferred_element_type=F32)
        rc = jnp.dot(tu, da[r, SSD_INNER:], precision=HIGHEST, preferred_element_type=F32)
        for g in range(SSD_GROUPS):
            ln = slice(g * LANES, (g + 1) * LANES)
            cumg, rcg = cum[:, ln], rc[:, ln]
            dtf, dtbk = dt[r, ln], dt[r, SSD_INNER + g * LANES:SSD_INNER + (g + 1) * LANES]
            xsg = xs[r, ln]
            xsg_b = xsg.astype(BF16)
            bg, cg = bm[r, ln], cm[r, ln]
            cb = lax.dot_general(cg, bg, NT, preferred_element_type=F32)
            cum_t, rc_t, dtf_t, dtb_t = cumg.T, rcg.T, dtf.T, dtbk.T
            cum_sw, rc_sw = pltpu.roll(cumg, SSD_HEAD_DIM, axis=1), pltpu.roll(rcg, SSD_HEAD_DIM, axis=1)
            yd = []
            for hh in range(2):
                row = slice(hh * SSD_HEAD_DIM, hh * SSD_HEAD_DIM + 1)
                col_c = jnp.where(lo_half, cumg, cum_sw) if hh == 0 else jnp.where(lo_half, cum_sw, cumg)
                col_r = jnp.where(lo_half, rcg, rc_sw) if hh == 0 else jnp.where(lo_half, rc_sw, rcg)
                ef = jnp.exp(jnp.where(jj <= ii, col_c - cum_t[row], ninf)) * dtf_t[row]
                eb = jnp.exp(jnp.where(jj >= ii, col_r - rc_t[row], ninf)) * dtb_t[row]
                w = (cb * (ef + eb)).astype(BF16)
                yd.append(jnp.dot(w, xsg_b, preferred_element_type=F32))
            y = jnp.where(lo_half, yd[0], yd[1])
            last = cumg[Q - 1:Q]
            wxf = (xsg * jnp.exp(last - cumg) * dtf).astype(BF16)
            s_prev = s_ref[g]
            y = y + jnp.dot(cg, s_prev.astype(BF16), preferred_element_type=F32) * jnp.exp(cumg)
            s_ref[g] = s_prev * jnp.exp(last) + lax.dot_general(bg, wxf, TN, preferred_element_type=F32)
            yp_ref[r, ln] = y + dx_ref[:, ln] * xsg
            wxb_ref[r, ln] = (xsg * jnp.exp(rcg[0:1] - rcg) * dtbk).astype(BF16)
            eb_ref[r, ln] = jnp.exp(rcg)


def _ssd_bwd_kernel(yp_ref, cm_ref, bm_ref, wxb_ref, eb_ref, z_ref, g_ref, o_ref, s_ref, y_ref, *, tm):
    Q = SSD_CHUNK

    @pl.when(pl.program_id(1) == 0)
    def _():
        s_ref[...] = jnp.zeros(s_ref.shape, F32)

    for c in reversed(range(tm // Q)):
        r = slice(c * Q, (c + 1) * Q)
        for g in range(SSD_GROUPS):
            ln = slice(g * LANES, (g + 1) * LANES)
            ebg = eb_ref[r, ln]
            s_prev = s_ref[g]
            y_ref[r, ln] = yp_ref[r, ln] + jnp.dot(cm_ref[r, ln], s_prev.astype(BF16),
                                                   preferred_element_type=F32) * ebg
            s_ref[g] = s_prev * ebg[0:1] + lax.dot_general(bm_ref[r, ln], wxb_ref[r, ln], TN,
                                                           preferred_element_type=F32)
    y = y_ref[...] * _silu(z_ref[...])
    ms = jnp.mean(y * y, axis=-1, keepdims=True)
    o_ref[...] = (y * lax.rsqrt(ms + EPS) * g_ref[...]).astype(BF16)


def ssd_mixer(u_ssd, batch, seq_len, conv_w, conv_b, dt_bias_x, a_x, d_x, norm_g, tm=512):
    t = u_ssd.shape[0]
    nb = seq_len // tm
    hb = tm // 8
    last_halo = t // 8 - 1
    row = lambda b, i: (b * nb + i, 0)
    const = lambda b, i: (0, 0)
    outs = pl.pallas_call(
        functools.partial(_ssd_fwd_kernel, tm=tm),
        grid=(batch, nb),
        in_specs=[pl.BlockSpec((tm, SSD_PAD), row),
                  pl.BlockSpec((8, SSD_PAD), lambda b, i: (jnp.maximum((b * nb + i) * hb - 1, 0), 0)),
                  pl.BlockSpec((8, SSD_PAD), lambda b, i: (jnp.minimum((b * nb + i + 1) * hb, last_halo), 0)),
                  pl.BlockSpec((3, SSD_XBC), const),
                  pl.BlockSpec((1, SSD_XBC), const),
                  pl.BlockSpec((1, SSD_DTX), const),
                  pl.BlockSpec((1, SSD_DTX), const),
                  pl.BlockSpec((1, SSD_INNER), const)],
        out_specs=[pl.BlockSpec((tm, SSD_INNER), row)] * 5,
        out_shape=[jax.ShapeDtypeStruct((t, SSD_INNER), F32),
                   jax.ShapeDtypeStruct((t, SSD_INNER), BF16),
                   jax.ShapeDtypeStruct((t, SSD_INNER), BF16),
                   jax.ShapeDtypeStruct((t, SSD_INNER), BF16),
                   jax.ShapeDtypeStruct((t, SSD_INNER), F32)],
        scratch_shapes=[pltpu.VMEM((SSD_GROUPS, SSD_STATE, LANES), F32)],
        compiler_params=_cparams(2),
        name="ssd_fwd",
    )(u_ssd, u_ssd, u_ssd, conv_w, conv_b.reshape(1, SSD_XBC), dt_bias_x, a_x, d_x)
    rrow = lambda b, i: (b * nb + nb - 1 - i, 0)
    return pl.pallas_call(
        functools.partial(_ssd_bwd_kernel, tm=tm),
        grid=(batch, nb),
        in_specs=[pl.BlockSpec((tm, SSD_INNER), rrow)] * 6 + [pl.BlockSpec((1, SSD_INNER), const)],
        out_specs=pl.BlockSpec((tm, SSD_INNER), rrow),
        out_shape=jax.ShapeDtypeStruct((t, SSD_INNER), BF16),
        scratch_shapes=[pltpu.VMEM((SSD_GROUPS, SSD_STATE, LANES), F32),
                        pltpu.VMEM((tm, SSD_INNER), F32)],
        compiler_params=_cparams(2),
        name="ssd_bwd",
    )(*outs, u_ssd, norm_g.reshape(1, SSD_INNER))


def _gla_fwd_kernel(u_ref, wg_ref, bg_ref, op_ref, qb_ref, kb_ref, db_ref, st_ref, *, tm):
    Q = GLA_CHUNK
    P = 2 * Q

    @pl.when(pl.program_id(1) == 0)
    def _():
        st_ref[...] = jnp.zeros(st_ref.shape, F32)

    pre = jnp.dot(u_ref[:, GLA_LR0:GLA_PAD], wg_ref[...], precision=HIGHEST,
                  preferred_element_type=F32) + bg_ref[...]
    gk = (jnp.minimum(pre, 0.0) - jnp.log1p(jnp.exp(-jnp.abs(pre)))) * (1.0 / GLA_NORMALIZER)

    mask_f = _tri(P, False, Q)
    mask_b = _tri(P, True, Q)
    tl = mask_f.astype(F32)
    tu = mask_b.astype(F32)
    lane_k = lax.broadcasted_iota(jnp.int32, (P, GLA_QK), 1) // GLA_DK
    lane_v = lax.broadcasted_iota(jnp.int32, (P, GLA_V), 1) // GLA_DV
    bd_t = (lax.broadcasted_iota(jnp.int32, (GLA_V, GLA_QK), 0) // GLA_DV ==
            lax.broadcasted_iota(jnp.int32, (GLA_V, GLA_QK), 1) // GLA_DK)

    for c in range(tm // P):
        r = slice(c * P, (c + 1) * P)
        gf = jnp.dot(tl, gk[r, 0:GLA_QK], precision=HIGHEST, preferred_element_type=F32)
        gb = jnp.dot(tu, gk[r, GLA_QK:], precision=HIGHEST, preferred_element_type=F32)
        qs = u_ref[r, 0:GLA_QK] * (GLA_DK ** -0.5)
        k = u_ref[r, GLA_QK:2 * GLA_QK]
        v = u_ref[r, 2 * GLA_QK:2 * GLA_QK + GLA_V]
        vb = v.astype(BF16)
        q_f = qs * jnp.exp(gf)
        k_f = (k * jnp.exp(-gf)).astype(BF16)
        q_b = qs * jnp.exp(gb)
        k_b = (k * jnp.exp(-gb)).astype(BF16)
        o = jnp.zeros((P, GLA_V), F32)
        for h in range(GLA_HEADS):
            hm = lane_k == h
            af = lax.dot_general(jnp.where(hm, q_f, 0.0).astype(BF16), k_f, NT, preferred_element_type=F32)
            ab = lax.dot_general(jnp.where(hm, q_b, 0.0).astype(BF16), k_b, NT, preferred_element_type=F32)
            att = jnp.where(mask_f, af, 0.0) + jnp.where(mask_b, ab, 0.0)
            vh = jnp.where(lane_v == h, v, 0.0).astype(BF16)
            o = o + jnp.dot(att.astype(BF16), vh, preferred_element_type=F32)
        q_fb = q_f.astype(BF16)
        for cc in range(2):
            rr = slice(cc * Q, (cc + 1) * Q)
            ro = slice(c * P + cc * Q, c * P + (cc + 1) * Q)
            st = st_ref[...]
            g_last = gf[cc * Q + Q - 1:cc * Q + Q]
            k_end = (k[rr] * jnp.exp(g_last - gf[rr])).astype(BF16)
            op_ref[ro] = o[rr] + lax.dot_general(q_fb[rr], st.astype(BF16), NT, preferred_element_type=F32)
            upd = lax.dot_general(vb[rr], k_end, TN, preferred_element_type=F32)
            st_ref[...] = st * jnp.exp(g_last) + jnp.where(bd_t, upd, 0.0)
            g_first = gb[cc * Q:cc * Q + 1]
            kb_ref[ro] = (k[rr] * jnp.exp(g_first - gb[rr])).astype(BF16)
        qb_ref[r] = q_b.astype(BF16)
        db_ref[r] = jnp.exp(gb)


def _gla_bwd_kernel(op_ref, qb_ref, kb_ref, db_ref, v_ref, go_ref, g_ref, o_ref, st_ref, y_ref, *, tm):
    Q = GLA_CHUNK

    @pl.when(pl.program_id(1) == 0)
    def _():
        st_ref[...] = jnp.zeros(st_ref.shape, F32)

    bd_t = (lax.broadcasted_iota(jnp.int32, (GLA_V, GLA_QK), 0) // GLA_DV ==
            lax.broadcasted_iota(jnp.int32, (GLA_V, GLA_QK), 1) // GLA_DK)
    for c in reversed(range(tm // Q)):
        r = slice(c * Q, (c + 1) * Q)
        st = st_ref[...]
        y_ref[r] = op_ref[r] + lax.dot_general(qb_ref[r], st.astype(BF16), NT, preferred_element_type=F32)
        upd = lax.dot_general(v_ref[r].astype(BF16), kb_ref[r], TN, preferred_element_type=F32)
        st_ref[...] = st * db_ref[c * Q:c * Q + 1] + jnp.where(bd_t, upd, 0.0)
    y = y_ref[...]
    seg = (lax.broadcasted_iota(jnp.int32, (GLA_V, GLA_V), 0) // GLA_DV ==
           lax.broadcasted_iota(jnp.int32, (GLA_V, GLA_V), 1) // GLA_DV).astype(F32)
    ms = jnp.dot(y * y, seg, precision=HIGHEST, preferred_element_type=F32) * (1.0 / GLA_DV)
    o_ref[...] = (y * lax.rsqrt(ms + EPS) * g_ref[...] * _silu(go_ref[...])).astype(BF16)


def gla_mixer(u_gla, batch, seq_len, wg, bg, norm_g_x, tm=512):
    t = u_gla.shape[0]
    nb = seq_len // tm
    row = lambda b, i: (b * nb + i, 0)
    const = lambda b, i: (0, 0)
    outs = pl.pallas_call(
        functools.partial(_gla_fwd_kernel, tm=tm),
        grid=(batch, nb),
        in_specs=[pl.BlockSpec((tm, GLA_PAD), row),
                  pl.BlockSpec((GLA_PAD - GLA_LR0, 2 * GLA_QK), const),
                  pl.BlockSpec((1, 2 * GLA_QK), const)],
        out_specs=[pl.BlockSpec((tm, GLA_V), row), pl.BlockSpec((tm, GLA_QK), row),
                   pl.BlockSpec((tm, GLA_QK), row), pl.BlockSpec((tm, GLA_QK), row)],
        out_shape=[jax.ShapeDtypeStruct((t, GLA_V), F32),
                   jax.ShapeDtypeStruct((t, GLA_QK), BF16),
                   jax.ShapeDtypeStruct((t, GLA_QK), BF16),
                   jax.ShapeDtypeStruct((t, GLA_QK), F32)],
        scratch_shapes=[pltpu.VMEM((GLA_V, GLA_QK), F32)],
        compiler_params=_cparams(2),
        name="gla_fwd",
    )(u_gla, wg, bg)
    rrow = lambda b, i: (b * nb + nb - 1 - i, 0)
    return pl.pallas_call(
        functools.partial(_gla_bwd_kernel, tm=tm),
        grid=(batch, nb),
        in_specs=[pl.BlockSpec((tm, GLA_V), rrow), pl.BlockSpec((tm, GLA_QK), rrow),
                  pl.BlockSpec((tm, GLA_QK), rrow), pl.BlockSpec((tm, GLA_QK), rrow),
                  pl.BlockSpec((tm, GLA_V), lambda b, i: (b * nb + nb - 1 - i, 1)),
                  pl.BlockSpec((tm, GLA_V), lambda b, i: (b * nb + nb - 1 - i, 2)),
                  pl.BlockSpec((1, GLA_V), const)],
        out_specs=pl.BlockSpec((tm, GLA_V), rrow),
        out_shape=jax.ShapeDtypeStruct((t, GLA_V), BF16),
        scratch_shapes=[pltpu.VMEM((GLA_V, GLA_QK), F32), pltpu.VMEM((tm, GLA_V), F32)],
        compiler_params=_cparams(2),
        name="gla_bwd",
    )(*outs, u_gla, u_gla, norm_g_x)


def _out_ffn_kernel(x_ref, y0_ref, y1_ref, y2_ref, y3_ref, wo_ref, g2_ref, wu_ref, wd_ref, o_ref, *, tf):
    mixed = jnp.concatenate([y0_ref[...], y1_ref[...], y2_ref[...], y3_ref[...]], axis=1)
    x1 = x_ref[...] + jnp.dot(mixed, wo_ref[...], preferred_element_type=F32)
    ms = jnp.mean(x1 * x1, axis=-1, keepdims=True)
    h2 = (x1 * lax.rsqrt(ms + EPS) * g2_ref[...]).astype(BF16)
    acc = x1
    for c in range(D_FF // tf):
        up = jnp.dot(h2, wu_ref[:, c * tf:(c + 1) * tf], preferred_element_type=F32)
        a = jnp.square(jnp.maximum(up, 0.0)).astype(BF16)
        acc = acc + jnp.dot(a, wd_ref[c * tf:(c + 1) * tf, :], preferred_element_type=F32)
    o_ref[...] = acc


def out_ffn(x2d, ys, wo, g2, wu, wd, tm=512, tf=1024):
    t = x2d.shape[0]
    const = lambda i: (0, 0)
    single = pl.Buffered(1)
    return pl.pallas_call(
        functools.partial(_out_ffn_kernel, tf=tf),
        grid=(t // tm,),
        in_specs=[pl.BlockSpec((tm, D_MODEL), lambda i: (i, 0))] +
                 [pl.BlockSpec((tm, GROUP_W), lambda i: (i, 0))] * 4 +
                 [pl.BlockSpec((D_MODEL, D_MODEL), const, pipeline_mode=single),
                  pl.BlockSpec((1, D_MODEL), const),
                  pl.BlockSpec((D_MODEL, D_FF), const, pipeline_mode=single),
                  pl.BlockSpec((D_FF, D_MODEL), const, pipeline_mode=single)],
        out_specs=pl.BlockSpec((tm, D_MODEL), lambda i: (i, 0)),
        out_shape=jax.ShapeDtypeStruct((t, D_MODEL), F32),
        compiler_params=_cparams(1),
        name="out_ffn",
    )(x2d, *ys, wo, g2.reshape(1, D_MODEL), wu, wd)


HY_N1 = 128
HY_ROWS = HY_N1 // 2


def _hy_pre_kernel(u_ref, up_ref, un_ref, cw_ref, cb_ref, o_ref, *, tm):
    i = pl.program_id(1)
    nb = pl.num_programs(1)
    x = u_ref[...]
    prev_row = jnp.where(i > 0, up_ref[7:8, :], 0.0)
    next_row = jnp.where(i < nb - 1, un_ref[0:1, :], 0.0)
    rid = lax.broadcasted_iota(jnp.int32, x.shape, 0)
    xm = jnp.where(rid == 0, prev_row, pltpu.roll(x, 1, axis=0))
    xp = jnp.where(rid == tm - 1, next_row, pltpu.roll(x, tm - 1, axis=0))
    uc = xm * cw_ref[0:1] + x * cw_ref[1:2] + xp * cw_ref[2:3] + cb_ref[...]
    o_ref[0] = uc.T


def hy_pre(u_hy, batch, seq_len, conv_w, conv_b, tm=512):
    t = u_hy.shape[0]
    nb = seq_len // tm
    hb = tm // 8
    last_halo = t // 8 - 1
    const = lambda b, i: (0, 0)
    return pl.pallas_call(
        functools.partial(_hy_pre_kernel, tm=tm),
        grid=(batch, nb),
        in_specs=[pl.BlockSpec((tm, HY_IN), lambda b, i: (b * nb + i, 0)),
                  pl.BlockSpec((8, HY_IN), lambda b, i: (jnp.maximum((b * nb + i) * hb - 1, 0), 0)),
                  pl.BlockSpec((8, HY_IN), lambda b, i: (jnp.minimum((b * nb + i + 1) * hb, last_halo), 0)),
                  pl.BlockSpec((3, HY_IN), const),
                  pl.BlockSpec((1, HY_IN), const)],
        out_specs=pl.BlockSpec((1, HY_IN, tm), lambda b, i: (b, 0, i)),
        out_shape=jax.ShapeDtypeStruct((batch, HY_IN, seq_len), F32),
        compiler_params=_cparams(2),
        name="hy_pre",
    )(u_hy, u_hy, u_hy, conv_w, conv_b.reshape(1, HY_IN))


def _hy_conv_kernel(z_ref, g_ref, sk_ref, h_ref, f1_ref, w2_ref, t_ref, gi_ref, o_ref, *, cb, n2):
    ct = z_ref.shape[1]
    f1 = f1_ref[...]
    w2 = w2_ref[...]
    tre, tim = t_ref[0:HY_N1], t_ref[HY_N1:]
    gi = gi_ref[...]

    def quads(blk):
        return blk[0:HY_N1, 0:n2], blk[0:HY_N1, n2:], blk[HY_N1:, 0:n2], blk[HY_N1:, n2:]

    def body(it, carry):
        c0 = it * cb
        planes = []
        for u in range(cb):
            a = jnp.dot(f1, z_ref[0, c0 + u].astype(BF16), preferred_element_type=F32)
            are, aim = a[0:HY_N1], a[HY_N1:]
            planes += [(are * tre - aim * tim).astype(BF16), (are * tim + aim * tre).astype(BF16)]
        spec = jnp.dot(jnp.concatenate(planes, axis=0), w2, preferred_element_type=F32)
        planes = []
        for u in range(cb):
            tl, tr, bl, br = quads(spec[u * 2 * HY_N1:(u + 1) * 2 * HY_N1])
            bre, bim = tl - br, tr + bl
            hre, him = h_ref[c0 + u, 0:HY_N1], h_ref[c0 + u, HY_N1:]
            planes += [(bre * hre - bim * him).astype(BF16), (bre * him + bim * hre).astype(BF16)]
        back = jnp.dot(jnp.concatenate(planes, axis=0), w2, preferred_element_type=F32)
        for u in range(cb):
            tl, tr, bl, br = quads(back[u * 2 * HY_N1:(u + 1) * 2 * HY_N1])
            are, aim = tl + br, bl - tr
            st = jnp.concatenate([are * tre + aim * tim, aim * tre - are * tim], axis=0).astype(BF16)
            y = jnp.dot(gi, st, preferred_element_type=F32)
            o_ref[0, c0 + u] = g_ref[0, c0 + u] * (y + z_ref[0, c0 + u] * sk_ref[c0 + u])
        return carry

    lax.fori_loop(0, ct // cb, body, 0)


def hy_conv(z4, z_off, gate4, gate_off, skip, h_spec, consts, ct=32, cb=2):
    b, _, _, n2 = z4.shape
    f1, w2, tw, gi = consts
    nct = HY_C // ct
    seq = lambda off: (lambda bi, j: (bi, off // ct + j, 0, 0))
    const = lambda bi, j: (0, 0)
    return pl.pallas_call(
        functools.partial(_hy_conv_kernel, cb=cb, n2=n2),
        grid=(b, nct),
        in_specs=[pl.BlockSpec((1, ct, HY_ROWS, n2), seq(z_off)),
                  pl.BlockSpec((1, ct, HY_ROWS, n2), seq(gate_off)),
                  pl.BlockSpec((ct, 1, 1), lambda bi, j: (j, 0, 0)),
                  pl.BlockSpec((ct, 2 * HY_N1, n2), lambda bi, j: (j, 0, 0)),
                  pl.BlockSpec((2 * HY_N1, HY_ROWS), const),
                  pl.BlockSpec((n2, 2 * n2), const),
                  pl.BlockSpec((2 * HY_N1, n2), const),
                  pl.BlockSpec((HY_ROWS, 2 * HY_N1), const)],
        out_specs=pl.BlockSpec((1, ct, HY_ROWS, n2), lambda bi, j: (bi, j, 0, 0)),
        out_shape=jax.ShapeDtypeStruct((b, HY_C, HY_ROWS, n2), F32),
        compiler_params=_cparams(2),
        name="hy_conv",
    )(z4, gate4, skip.reshape(HY_C, 1, 1), h_spec, f1, w2, tw, gi)


def _hy_post_kernel(z_ref, g_ref, o_ref):
    x = z_ref[0]
    ms = jnp.mean(x * x, axis=0, keepdims=True)
    o_ref[...] = (x * lax.rsqrt(ms + EPS) * g_ref[...]).T.astype(BF16)


def hy_post(z3, g, tm=512):
    b, _, seq_len = z3.shape
    nb = seq_len // tm
    return pl.pallas_call(
        _hy_post_kernel,
        grid=(b, nb),
        in_specs=[pl.BlockSpec((1, HY_C, tm), lambda bi, i: (bi, 0, i)),
                  pl.BlockSpec((HY_C, 1), lambda bi, i: (0, 0))],
        out_specs=pl.BlockSpec((tm, HY_C), lambda bi, i: (bi * nb + i, 0)),
        out_shape=jax.ShapeDtypeStruct((b * seq_len, HY_C), BF16),
        compiler_params=_cparams(2),
        name="hy_post",
    )(z3, g.reshape(HY_C, 1))


def _dft_consts(n2):
    n = HY_N1 * n2
    k1 = np.arange(HY_N1, dtype=np.float64)
    a1 = 2.0 * np.pi * np.outer(k1, np.arange(HY_ROWS)) / HY_N1
    f1 = np.concatenate([np.cos(a1), -np.sin(a1)], axis=0)
    m2 = np.arange(n2, dtype=np.float64)
    a2 = 2.0 * np.pi * np.outer(m2, m2) / n2
    w2 = np.concatenate([np.cos(a2), -np.sin(a2)], axis=1)
    at = 2.0 * np.pi * np.outer(k1, m2) / n
    tw = np.concatenate([np.cos(at), -np.sin(at)], axis=0)
    return (jnp.asarray(f1, BF16), jnp.asarray(w2, BF16), jnp.asarray(tw, F32), jnp.asarray(f1.T, BF16))


def _hyena_filter_freq(L, w1, b1, w2, b2, w3, freq):
    t = jnp.linspace(0.0, 1.0, L, dtype=F32)[:, None]
    w = 2.0 * math.pi * jnp.arange(L, dtype=F32) / L
    bands = jnp.linspace(1e-4, HY_BANDS - 1, HY_BANDS, dtype=F32)
    ang = w[:, None] * bands
    z = jnp.concatenate([t, jnp.cos(ang), -jnp.sin(ang)], axis=-1)
    hid = jnp.sin(freq * (z @ w1 + b1))
    hid = jnp.sin(freq * (hid @ w2 + b2))
    h = (hid @ w3).reshape(L, HY_ORDER, 2, HY_C)
    deltas = jnp.linspace(HY_MIN_DECAY, HY_MAX_DECAY, HY_C, dtype=F32)
    h = h * jnp.exp(-t * jnp.abs(deltas))[:, None, None, :]
    h = h / jnp.sum(jnp.abs(h), axis=(0, 2), keepdims=True)
    h_f, h_b = h[:, :, 0], h[:, :, 1]
    two_sided = jnp.concatenate([h_f[:1] + h_b[:1], h_f[1:], jnp.zeros_like(h_f[:1]),
                                 jnp.flip(h_b[1:], axis=0)], axis=0)
    n = 2 * L
    hf = jnp.fft.fft(two_sided, axis=0) * (1.0 / n)
    hf = hf.reshape(n // HY_N1, HY_N1, HY_ORDER, HY_C).transpose(2, 3, 1, 0)
    return jnp.concatenate([jnp.real(hf), jnp.imag(hf)], axis=2).astype(F32)


def hyena_mixer(u_hy, batch, seq_len, p, h_spec):
    n2 = 2 * seq_len // HY_N1
    consts = _dft_consts(n2)
    uct = hy_pre(u_hy, batch, seq_len, p['hy_conv_w'], p['hy_conv_b'])
    uc4 = uct.reshape(batch, HY_IN, HY_ROWS, n2)
    z = hy_conv(uc4, 0, uc4, HY_C, p['hy_skip'][0], h_spec[0], consts)
    z = hy_conv(z, 0, uc4, 2 * HY_C, p['hy_skip'][1], h_spec[1], consts)
    return hy_post(z.reshape(batch, HY_C, seq_len), p['hy_out_g'])


_DEINTERLEAVE = np.concatenate([np.arange(0, ATT_HEAD_DIM, 2), np.arange(1, ATT_HEAD_DIM, 2)])


def _pack_w_in(w):
    c0 = 0
    w_hy = w[:, c0:c0 + HY_IN]; c0 += HY_IN
    w_ssd = w[:, c0:c0 + SSD_IN]; c0 += SSD_IN
    w_gla = w[:, c0:c0 + GLA_IN]; c0 += GLA_IN
    w_att = w[:, c0:c0 + ATT_IN]
    zeros = lambda n: jnp.zeros((D_MODEL, n), w.dtype)
    cols = [w_hy, w_ssd[:, :SSD_INNER + SSD_XBC],
            jnp.repeat(w_ssd[:, SSD_INNER + SSD_XBC:], SSD_HEAD_DIM, axis=1),
            w_gla, zeros(GLA_PAD - GLA_IN)]
    for h in range(ATT_HEADS):
        wh = w_att[:, h * ATT_HEAD_DIM:(h + 1) * ATT_HEAD_DIM][:, _DEINTERLEAVE]
        cols += [wh, zeros(64)] if h < 2 else [zeros(64), wh]
    kq = ATT_HEADS * ATT_HEAD_DIM
    for g in range(ATT_KV_HEADS):
        cols.append(w_att[:, kq + g * ATT_HEAD_DIM:kq + (g + 1) * ATT_HEAD_DIM][:, _DEINTERLEAVE])
    cols.append(w_att[:, kq + ATT_KV_HEADS * ATT_HEAD_DIM:])
    return jnp.concatenate(cols, axis=1).astype(BF16)


def _rope_tables(seq_len):
    rows = seq_len // GRID_W
    row = jnp.repeat(jnp.arange(rows, dtype=F32), GRID_W)
    col = jnp.tile(jnp.arange(GRID_W, dtype=F32), rows)
    inv_freq = 1.0 / (ROPE_THETA ** (jnp.arange(0, ROPE_AXIS_DIM, 2, dtype=F32) / ROPE_AXIS_DIM))
    ang = jnp.concatenate([row[:, None] * inv_freq, col[:, None] * inv_freq], axis=-1)
    cos, sin = jnp.cos(ang), jnp.sin(ang)
    return jnp.tile(cos, (1, 4)), jnp.tile(jnp.concatenate([-sin, sin], axis=-1), (1, 2))


def _layer_params(i, s):
    p = {name: val[i] for name, val in s.items()}
    p['w_in_p'] = _pack_w_in(p['w_in'])
    p['w_out_b'] = p['w_out'].astype(BF16)
    p['w_up_b'] = p['w_up'].astype(BF16)
    p['w_down_b'] = p['w_down'].astype(BF16)
    p['ssd_dt_bias_x'] = jnp.repeat(p['ssd_dt_bias'].reshape(-1), SSD_HEAD_DIM).reshape(1, SSD_DTX)
    p['ssd_a_x'] = jnp.repeat(-jnp.exp(p['ssd_A_log']).reshape(-1), SSD_HEAD_DIM).reshape(1, SSD_DTX)
    p['ssd_d_x'] = jnp.repeat(p['ssd_D'], SSD_HEAD_DIM).reshape(1, SSD_INNER)
    wg = jnp.zeros((GLA_PAD - GLA_LR0, 2 * GLA_QK), F32)
    wg = wg.at[0:GLA_RANK, 0:GLA_QK].set(p['gla_gk_w'][0])
    wg = wg.at[GLA_RANK:2 * GLA_RANK, GLA_QK:].set(p['gla_gk_w'][1])
    p['gla_wg'] = wg
    p['gla_bg'] = p['gla_gk_b'].reshape(1, 2 * GLA_QK)
    p['gla_norm_g_x'] = jnp.tile(p['gla_norm_g'], GLA_HEADS).reshape(1, GLA_V)
    p['att_gq'] = jnp.tile(p['att_q_norm_g'][_DEINTERLEAVE], 2).reshape(1, LANES)
    p['att_gk'] = jnp.tile(p['att_k_norm_g'][_DEINTERLEAVE], 2).reshape(1, LANES)
    return p


def _layer(x, tables, p):
    b, L, _ = x.shape
    x2d = x.reshape(b * L, D_MODEL)
    u_hy, u_ssd, u_gla, u_att = in_proj(x2d, p['ln1_g'], p['w_in_p'])

    hf = _hyena_filter_freq(L, p['hy_ffn_w1'], p['hy_ffn_b1'], p['hy_ffn_w2'], p['hy_ffn_b2'],
                            p['hy_ffn_w3'], p['hy_sin_freq'])
    y_hy = hyena_mixer(u_hy, b, L, p, hf)
    y_ssd = ssd_mixer(u_ssd, b, L, p['ssd_conv_w'], p['ssd_conv_b'], p['ssd_dt_bias_x'], p['ssd_a_x'],
                      p['ssd_d_x'], p['ssd_norm_g'])
    y_gla = gla_mixer(u_gla, b, L, p['gla_wg'], p['gla_bg'], p['gla_norm_g_x'])
    q, k, vt = att_prep(u_att, tables[0], tables[1], p['att_gq'], p['att_gk'], b, L)
    y_att = flash_attention(q.reshape(b, L, ATT_Q_PAD), k.reshape(b, L, LANES), vt, p['att_out_g'])

    y = out_ffn(x2d, (y_hy, y_ssd, y_gla, y_att.reshape(b * L, GROUP_W)), p['w_out_b'], p['ln2_g'],
                p['w_up_b'], p['w_down_b'])
    return y.reshape(b, L, D_MODEL)


def kernel(x_prompt, x_sample, ln1_g, w_in, hy_conv_w, hy_conv_b, hy_ffn_w1, hy_ffn_b1, hy_ffn_w2, hy_ffn_b2, hy_ffn_w3, hy_sin_freq, hy_skip, hy_out_g, ssd_conv_w, ssd_conv_b, ssd_A_log, ssd_dt_bias, ssd_D, ssd_norm_g, gla_gk_w, gla_gk_b, gla_norm_g, att_q_norm_g, att_k_norm_g, att_out_g, w_out, ln2_g, w_up, w_down):
    stacked = dict(ln1_g=ln1_g, w_in=w_in, hy_conv_w=hy_conv_w, hy_conv_b=hy_conv_b, hy_ffn_w1=hy_ffn_w1,
                   hy_ffn_b1=hy_ffn_b1, hy_ffn_w2=hy_ffn_w2, hy_ffn_b2=hy_ffn_b2, hy_ffn_w3=hy_ffn_w3,
                   hy_sin_freq=hy_sin_freq, hy_skip=hy_skip, hy_out_g=hy_out_g, ssd_conv_w=ssd_conv_w,
                   ssd_conv_b=ssd_conv_b, ssd_A_log=ssd_A_log, ssd_dt_bias=ssd_dt_bias, ssd_D=ssd_D,
                   ssd_norm_g=ssd_norm_g, gla_gk_w=gla_gk_w, gla_gk_b=gla_gk_b, gla_norm_g=gla_norm_g,
                   att_q_norm_g=att_q_norm_g, att_k_norm_g=att_k_norm_g, att_out_g=att_out_g,
                   w_out=w_out, ln2_g=ln2_g, w_up=w_up, w_down=w_down)
    tables_p = _rope_tables(x_prompt.shape[1])
    tables_s = _rope_tables(x_sample.shape[1])
    y_prompt, y_sample = x_prompt, x_sample
    for i in range(DEPTH):
        p = _layer_params(i, stacked)
        y_prompt = _layer(y_prompt, tables_p, p)
        y_sample = _layer(y_sample, tables_s, p)
    return (y_prompt, y_sample)
```

```python
import functools
import math

import jax
import jax.numpy as jnp
import numpy as np
from jax import lax
from jax.experimental import pallas as pl
from jax.experimental.pallas import tpu as pltpu

D_MODEL = 1024
DEPTH = 4
GRID_W = 64
EPS = 1e-6
GROUP_W = 256
D_FF = 4096

HY_C = 256
HY_ORDER = 2
HY_BANDS = 8
HY_MIN_DECAY = math.log(1e-2) / 1.5
HY_MAX_DECAY = math.log(1e-2) / 0.3
HY_IN = 768

SSD_HEAD_DIM = 64
SSD_HEADS = 4
SSD_GROUPS = 2
SSD_STATE = 128
SSD_CHUNK = 128
SSD_INNER = 256
SSD_BC = 256
SSD_XBC = 768
SSD_IN = 1032

GLA_HEADS = 4
GLA_DV = 64
GLA_DK = 32
GLA_RANK = 16
GLA_NORMALIZER = 16.0
GLA_CHUNK = 64
GLA_QK = 128
GLA_V = 256
GLA_IN = 800

ATT_HEAD_DIM = 64
ATT_HEADS = 4
ATT_KV_HEADS = 2
ROPE_THETA = 10000.0
ROPE_AXIS_DIM = 32
ATT_IN = 512

LANES = 128
SSD_DTX = 2 * SSD_HEADS * SSD_HEAD_DIM
SSD_PAD = SSD_INNER + SSD_XBC + SSD_DTX
GLA_PAD = 896
GLA_LR0 = 2 * GLA_QK + 2 * GLA_V
ATT_Q_PAD = ATT_HEADS * LANES
ATT_PAD = ATT_Q_PAD + 2 * LANES
ATT_VT_ROWS = ATT_HEAD_DIM + 16
IN_PAD = HY_IN + SSD_PAD + GLA_PAD + ATT_PAD

VMEM_LIMIT = 48 * 1024 * 1024
F32 = jnp.float32
BF16 = jnp.bfloat16
HIGHEST = lax.Precision.HIGHEST
NT = (((1,), (1,)), ((), ()))
TN = (((0,), (0,)), ((), ()))


def _cparams(n_grid):
    return pltpu.CompilerParams(dimension_semantics=("arbitrary",) * n_grid,
                                vmem_limit_bytes=VMEM_LIMIT)


def _softplus(x):
    return jnp.maximum(x, 0.0) + jnp.log1p(jnp.exp(-jnp.abs(x)))


def _silu(x):
    return x * jax.nn.sigmoid(x)


def _in_proj_kernel(x_ref, g_ref, w_ref, hy_ref, ssd_ref, gla_ref, att_ref):
    x = x_ref[...]
    ms = jnp.mean(x * x, axis=-1, keepdims=True)
    h = (x * lax.rsqrt(ms + EPS) * g_ref[...]).astype(BF16)
    c0 = 0
    for ref in (hy_ref, ssd_ref, gla_ref, att_ref):
        n = ref.shape[-1]
        ref[...] = jnp.dot(h, w_ref[:, c0:c0 + n], preferred_element_type=F32)
        c0 += n


def in_proj(x2d, g, w_p, tm=256):
    t = x2d.shape[0]
    widths = (HY_IN, SSD_PAD, GLA_PAD, ATT_PAD)
    return pl.pallas_call(
        _in_proj_kernel,
        grid=(t // tm,),
        in_specs=[pl.BlockSpec((tm, D_MODEL), lambda i: (i, 0)),
                  pl.BlockSpec((1, D_MODEL), lambda i: (0, 0)),
                  pl.BlockSpec((D_MODEL, IN_PAD), lambda i: (0, 0), pipeline_mode=pl.Buffered(1))],
        out_specs=[pl.BlockSpec((tm, n), lambda i: (i, 0)) for n in widths],
        out_shape=[jax.ShapeDtypeStruct((t, n), F32) for n in widths],
        compiler_params=_cparams(1),
        name="in_proj",
    )(x2d, g.reshape(1, D_MODEL), w_p)


def _rope(x, cos_t, sin_t):
    lane = lax.broadcasted_iota(jnp.int32, x.shape, 1)
    nxt = pltpu.roll(x, LANES - 32, axis=1)
    prv = pltpu.roll(x, 32, axis=1)
    sw = jnp.where((lane % 64) < 32, nxt, prv)
    return x * cos_t + sw * sin_t


def _att_prep_kernel(u_ref, cos_ref, sin_ref, gq_ref, gk_ref, q_ref, k_ref, vt_ref):
    cos_t = cos_ref[...]
    sin_t = sin_ref[...]
    inv = 1.0 / ATT_HEAD_DIM
    qscale = (ATT_HEAD_DIM ** -0.5) * math.log2(math.e)
    for h in range(ATT_HEADS):
        x = u_ref[:, h * LANES:(h + 1) * LANES]
        ms = jnp.sum(x * x, axis=-1, keepdims=True) * inv
        xn = x * lax.rsqrt(ms + EPS) * gq_ref[...]
        q_ref[:, h * LANES:(h + 1) * LANES] = (_rope(xn, cos_t, sin_t) * qscale).astype(BF16)
    k = u_ref[:, ATT_Q_PAD:ATT_Q_PAD + LANES]
    lane = lax.broadcasted_iota(jnp.int32, k.shape, 1)
    k2 = k * k
    s_all = jnp.sum(k2, axis=-1, keepdims=True)
    s_lo = jnp.sum(jnp.where(lane < 64, k2, 0.0), axis=-1, keepdims=True)
    ms = jnp.where(lane < 64, s_lo, s_all - s_lo) * inv
    kn = k * lax.rsqrt(ms + EPS) * gk_ref[...]
    k_ref[...] = _rope(kn, cos_t, sin_t).astype(BF16)
    vt = u_ref[:, ATT_Q_PAD + LANES:ATT_PAD].T
    rid = lax.broadcasted_iota(jnp.int32, (ATT_VT_ROWS - ATT_HEAD_DIM, vt.shape[1]), 0)
    tail = jnp.where(rid == 0, 1.0, 0.0)
    for g in range(ATT_KV_HEADS):
        vt_ref[0, g] = jnp.concatenate(
            [vt[g * ATT_HEAD_DIM:(g + 1) * ATT_HEAD_DIM], tail], axis=0).astype(BF16)


def att_prep(u_att, cos_t, sin_t, gq, gk, batch, seq_len, tm=512):
    t = u_att.shape[0]
    nl = seq_len // tm
    return pl.pallas_call(
        _att_prep_kernel,
        grid=(t // tm,),
        in_specs=[pl.BlockSpec((tm, ATT_PAD), lambda i: (i, 0)),
                  pl.BlockSpec((tm, LANES), lambda i: (i % nl, 0)),
                  pl.BlockSpec((tm, LANES), lambda i: (i % nl, 0)),
                  pl.BlockSpec((1, LANES), lambda i: (0, 0)),
                  pl.BlockSpec((1, LANES), lambda i: (0, 0))],
        out_specs=[pl.BlockSpec((tm, ATT_Q_PAD), lambda i: (i, 0)),
                   pl.BlockSpec((tm, LANES), lambda i: (i, 0)),
                   pl.BlockSpec((1, ATT_KV_HEADS, ATT_VT_ROWS, tm), lambda i: (i // nl, 0, 0, i % nl))],
        out_shape=[jax.ShapeDtypeStruct((t, ATT_Q_PAD), BF16),
                   jax.ShapeDtypeStruct((t, LANES), BF16),
                   jax.ShapeDtypeStruct((batch, ATT_KV_HEADS, ATT_VT_ROWS, seq_len), BF16)],
        compiler_params=_cparams(1),
        name="att_prep",
    )(u_att, cos_t, sin_t, gq, gk)


def _flash_kernel(q_ref, k_ref, vt_ref, g_ref, o_ref, m_ref, acc_ref, st_ref, *, tk, nk):
    m_ref[...] = jnp.full(m_ref.shape, -jnp.inf, F32)
    acc_ref[...] = jnp.zeros(acc_ref.shape, F32)

    def scores(j, slot):
        off = pl.multiple_of(j * tk, tk)
        k = k_ref[0, pl.ds(off, tk), :]
        for h in range(ATT_HEADS):
            q = q_ref[0, :, h * LANES:(h + 1) * LANES]
            st_ref[slot, h] = lax.dot_general(k, q, NT, preferred_element_type=F32)

    def consume(j, slot):
        off = pl.multiple_of(j * tk, tk)
        for h in range(ATT_HEADS):
            st = st_ref[slot, h]
            m_prev = m_ref[h]
            m_new = jnp.maximum(m_prev, jnp.max(st, axis=0, keepdims=True))
            alpha = jnp.exp2(m_prev - m_new)
            p = jnp.exp2((st - m_new).astype(BF16))
            vt = vt_ref[0, h // 2, :, pl.ds(off, tk)]
            acc_ref[h] = alpha * acc_ref[h] + jnp.dot(vt, p, preferred_element_type=F32)
            m_ref[h] = m_new

    scores(0, 0)

    def body(jj, carry):
        j = 2 * jj
        scores(j + 1, 1)
        consume(j, 0)
        scores(jnp.minimum(j + 2, nk - 1), 0)
        consume(j + 1, 1)
        return carry

    lax.fori_loop(0, nk // 2, body, 0)
    ot = jnp.concatenate([acc_ref[h, 0:ATT_HEAD_DIM] / acc_ref[h, ATT_HEAD_DIM:ATT_HEAD_DIM + 1]
                          for h in range(ATT_HEADS)], axis=0)
    o = ot.T
    ms = jnp.mean(o * o, axis=-1, keepdims=True)
    o_ref[0] = (o * lax.rsqrt(ms + EPS) * g_ref[...]).astype(BF16)


def flash_attention(q, k, vt, g_out, tq=512, tk=512):
    b, seq_len, _ = q.shape
    return pl.pallas_call(
        functools.partial(_flash_kernel, tk=tk, nk=seq_len // tk),
        grid=(b, seq_len // tq),
        in_specs=[pl.BlockSpec((1, tq, ATT_Q_PAD), lambda bi, i: (bi, i, 0)),
                  pl.BlockSpec((1, seq_len, LANES), lambda bi, i: (bi, 0, 0)),
                  pl.BlockSpec((1, ATT_KV_HEADS, ATT_VT_ROWS, seq_len), lambda bi, i: (bi, 0, 0, 0)),
                  pl.BlockSpec((1, GROUP_W), lambda bi, i: (0, 0))],
        out_specs=pl.BlockSpec((1, tq, GROUP_W), lambda bi, i: (bi, i, 0)),
        out_shape=jax.ShapeDtypeStruct((b, seq_len, GROUP_W), BF16),
        scratch_shapes=[pltpu.VMEM((ATT_HEADS, 1, tq), F32),
                        pltpu.VMEM((ATT_HEADS, ATT_VT_ROWS, tq), F32),
                        pltpu.VMEM((2, ATT_HEADS, tk, tq), F32)],
        compiler_params=_cparams(2),
        name="flash_attention",
    )(q, k, vt, g_out.reshape(1, GROUP_W))


def _tri(n, upper, block=None):
    ii = lax.broadcasted_iota(jnp.int32, (n, n), 0)
    jj = lax.broadcasted_iota(jnp.int32, (n, n), 1)
    m = (jj >= ii) if upper else (jj <= ii)
    if block is not None:
        m = m & ((ii // block) == (jj // block))
    return m


def _ssd_fwd_kernel(u_ref, up_ref, un_ref, cw_ref, cb_ref, dtb_ref, ax_ref, dx_ref,
                    yp_ref, cm_ref, bm_ref, wxb_ref, eb_ref, s_ref, *, tm):
    Q = SSD_CHUNK
    i = pl.program_id(1)
    nb = pl.num_programs(1)

    @pl.when(i == 0)
    def _():
        s_ref[...] = jnp.zeros(s_ref.shape, F32)

    x = u_ref[:, SSD_INNER:SSD_INNER + SSD_XBC]
    prev_row = jnp.where(i > 0, up_ref[7:8, SSD_INNER:SSD_INNER + SSD_XBC], 0.0)
    next_row = jnp.where(i < nb - 1, un_ref[0:1, SSD_INNER:SSD_INNER + SSD_XBC], 0.0)
    rid = lax.broadcasted_iota(jnp.int32, x.shape, 0)
    xm = jnp.where(rid == 0, prev_row, pltpu.roll(x, 1, axis=0))
    xp = jnp.where(rid == tm - 1, next_row, pltpu.roll(x, tm - 1, axis=0))
    conv = xm * cw_ref[0:1] + x * cw_ref[1:2] + xp * cw_ref[2:3] + cb_ref[...]
    xbc = _silu(conv)
    xs = xbc[:, 0:SSD_INNER]
    bm = xbc[:, SSD_INNER:SSD_INNER + SSD_BC].astype(BF16)
    cm = xbc[:, SSD_INNER + SSD_BC:].astype(BF16)
    bm_ref[...] = bm
    cm_ref[...] = cm
    dt = _softplus(u_ref[:, SSD_INNER + SSD_XBC:] + dtb_ref[...])
    da = dt * ax_ref[...]

    tl = _tri(Q, False).astype(F32)
    tu = _tri(Q, True).astype(F32)
    ii = lax.broadcasted_iota(jnp.int32, (Q, Q), 0)
    jj = lax.broadcasted_iota(jnp.int32, (Q, Q), 1)
    lo_half = lax.broadcasted_iota(jnp.int32, (Q, LANES), 1) < SSD_HEAD_DIM
    ninf = jnp.float32(-jnp.inf)

    for c in range(tm // Q):
        r = slice(c * Q, (c + 1) * Q)
        cum = jnp.dot(tl, da[r, 0:SSD_INNER], precision=HIGHEST, preferred_element_type=F32)
        rc = jnp.dot(tu, da[r, SSD_INNER:], precision=HIGHEST, preferred_element_type=F32)
        for g in range(SSD_GROUPS):
            ln = slice(g * LANES, (g + 1) * LANES)
            cumg, rcg = cum[:, ln], rc[:, ln]
            dtf, dtbk = dt[r, ln], dt[r, SSD_INNER + g * LANES:SSD_INNER + (g + 1) * LANES]
            xsg = xs[r, ln]
            xsg_b = xsg.astype(BF16)
            bg, cg = bm[r, ln], cm[r, ln]
            cb = lax.dot_general(cg, bg, NT, preferred_element_type=F32)
            cum_t, rc_t, dtf_t, dtb_t = cumg.T, rcg.T, dtf.T, dtbk.T
            cum_sw, rc_sw = pltpu.roll(cumg, SSD_HEAD_DIM, axis=1), pltpu.roll(rcg, SSD_HEAD_DIM, axis=1)
            yd = []
            for hh in range(2):
                row = slice(hh * SSD_HEAD_DIM, hh * SSD_HEAD_DIM + 1)
                col_c = jnp.where(lo_half, cumg, cum_sw) if hh == 0 else jnp.where(lo_half, cum_sw, cumg)
                col_r = jnp.where(lo_half, rcg, rc_sw) if hh == 0 else jnp.where(lo_half, rc_sw, rcg)
                ef = jnp.exp(jnp.where(jj <= ii, col_c - cum_t[row], ninf)) * dtf_t[row]
                eb = jnp.exp(jnp.where(jj >= ii, col_r - rc_t[row], ninf)) * dtb_t[row]
                w = (cb * (ef + eb)).astype(BF16)
                yd.append(jnp.dot(w, xsg_b, preferred_element_type=F32))
            y = jnp.where(lo_half, yd[0], yd[1])
            last = cumg[Q - 1:Q]
            wxf = (xsg * jnp.exp(last - cumg) * dtf).astype(BF16)
            s_prev = s_ref[g]
            y = y + jnp.dot(cg, s_prev.astype(BF16), preferred_element_type=F32) * jnp.exp(cumg)
            s_ref[g] = s_prev * jnp.exp(last) + lax.dot_general(bg, wxf, TN, preferred_element_type=F32)
            yp_ref[r, ln] = y + dx_ref[:, ln] * xsg
            wxb_ref[r, ln] = (xsg * jnp.exp(rcg[0:1] - rcg) * dtbk).astype(BF16)
            eb_ref[r, ln] = jnp.exp(rcg)


def _ssd_bwd_kernel(yp_ref, cm_ref, bm_ref, wxb_ref, eb_ref, z_ref, g_ref, o_ref, s_ref, y_ref, *, tm):
    Q = SSD_CHUNK

    @pl.when(pl.program_id(1) == 0)
    def _():
        s_ref[...] = jnp.zeros(s_ref.shape, F32)

    for c in reversed(range(tm // Q)):
        r = slice(c * Q, (c + 1) * Q)
        for g in range(SSD_GROUPS):
            ln = slice(g * LANES, (g + 1) * LANES)
            ebg = eb_ref[r, ln]
            s_prev = s_ref[g]
            y_ref[r, ln] = yp_ref[r, ln] + jnp.dot(cm_ref[r, ln], s_prev.astype(BF16),
                                                   preferred_element_type=F32) * ebg
            s_ref[g] = s_prev * ebg[0:1] + lax.dot_general(bm_ref[r, ln], wxb_ref[r, ln], TN,
                                                           preferred_element_type=F32)
    y = y_ref[...] * _silu(z_ref[...])
    ms = jnp.mean(y * y, axis=-1, keepdims=True)
    o_ref[...] = (y * lax.rsqrt(ms + EPS) * g_ref[...]).astype(BF16)


def ssd_mixer(u_ssd, batch, seq_len, conv_w, conv_b, dt_bias_x, a_x, d_x, norm_g, tm=512):
    t = u_ssd.shape[0]
    nb = seq_len // tm
    hb = tm // 8
    last_halo = t // 8 - 1
    row = lambda b, i: (b * nb + i, 0)
    const = lambda b, i: (0, 0)
    outs = pl.pallas_call(
        functools.partial(_ssd_fwd_kernel, tm=tm),
        grid=(batch, nb),
        in_specs=[pl.BlockSpec((tm, SSD_PAD), row),
                  pl.BlockSpec((8, SSD_PAD), lambda b, i: (jnp.maximum((b * nb + i) * hb - 1, 0), 0)),
                  pl.BlockSpec((8, SSD_PAD), lambda b, i: (jnp.minimum((b * nb + i + 1) * hb, last_halo), 0)),
                  pl.BlockSpec((3, SSD_XBC), const),
                  pl.BlockSpec((1, SSD_XBC), const),
                  pl.BlockSpec((1, SSD_DTX), const),
                  pl.BlockSpec((1, SSD_DTX), const),
                  pl.BlockSpec((1, SSD_INNER), const)],
        out_specs=[pl.BlockSpec((tm, SSD_INNER), row)] * 5,
        out_shape=[jax.ShapeDtypeStruct((t, SSD_INNER), F32),
                   jax.ShapeDtypeStruct((t, SSD_INNER), BF16),
                   jax.ShapeDtypeStruct((t, SSD_INNER), BF16),
                   jax.ShapeDtypeStruct((t, SSD_INNER), BF16),
                   jax.ShapeDtypeStruct((t, SSD_INNER), F32)],
        scratch_shapes=[pltpu.VMEM((SSD_GROUPS, SSD_STATE, LANES), F32)],
        compiler_params=_cparams(2),
        name="ssd_fwd",
    )(u_ssd, u_ssd, u_ssd, conv_w, conv_b.reshape(1, SSD_XBC), dt_bias_x, a_x, d_x)
    rrow = lambda b, i: (b * nb + nb - 1 - i, 0)
    return pl.pallas_call(
        functools.partial(_ssd_bwd_kernel, tm=tm),
        grid=(batch, nb),
        in_specs=[pl.BlockSpec((tm, SSD_INNER), rrow)] * 6 + [pl.BlockSpec((1, SSD_INNER), const)],
        out_specs=pl.BlockSpec((tm, SSD_INNER), rrow),
        out_shape=jax.ShapeDtypeStruct((t, SSD_INNER), BF16),
        scratch_shapes=[pltpu.VMEM((SSD_GROUPS, SSD_STATE, LANES), F32),
                        pltpu.VMEM((tm, SSD_INNER), F32)],
        compiler_params=_cparams(2),
        name="ssd_bwd",
    )(*outs, u_ssd, norm_g.reshape(1, SSD_INNER))


def _gla_fwd_kernel(u_ref, wg_ref, bg_ref, op_ref, qb_ref, kb_ref, db_ref, st_ref, *, tm):
    Q = GLA_CHUNK
    P = 2 * Q

    @pl.when(pl.program_id(1) == 0)
    def _():
        st_ref[...] = jnp.zeros(st_ref.shape, F32)

    pre = jnp.dot(u_ref[:, GLA_LR0:GLA_PAD], wg_ref[...], precision=HIGHEST,
                  preferred_element_type=F32) + bg_ref[...]
    gk = (jnp.minimum(pre, 0.0) - jnp.log1p(jnp.exp(-jnp.abs(pre)))) * (1.0 / GLA_NORMALIZER)

    mask_f = _tri(P, False, Q)
    mask_b = _tri(P, True, Q)
    tl = mask_f.astype(F32)
    tu = mask_b.astype(F32)
    lane_k = lax.broadcasted_iota(jnp.int32, (P, GLA_QK), 1) // GLA_DK
    lane_v = lax.broadcasted_iota(jnp.int32, (P, GLA_V), 1) // GLA_DV
    bd_t = (lax.broadcasted_iota(jnp.int32, (GLA_V, GLA_QK), 0) // GLA_DV ==
            lax.broadcasted_iota(jnp.int32, (GLA_V, GLA_QK), 1) // GLA_DK)

    for c in range(tm // P):
        r = slice(c * P, (c + 1) * P)
        gf = jnp.dot(tl, gk[r, 0:GLA_QK], precision=HIGHEST, preferred_element_type=F32)
        gb = jnp.dot(tu, gk[r, GLA_QK:], precision=HIGHEST, preferred_element_type=F32)
        qs = u_ref[r, 0:GLA_QK] * (GLA_DK ** -0.5)
        k = u_ref[r, GLA_QK:2 * GLA_QK]
        v = u_ref[r, 2 * GLA_QK:2 * GLA_QK + GLA_V]
        vb = v.astype(BF16)
        q_f = qs * jnp.exp(gf)
        k_f = (k * jnp.exp(-gf)).astype(BF16)
        q_b = qs * jnp.exp(gb)
        k_b = (k * jnp.exp(-gb)).astype(BF16)
        o = jnp.zeros((P, GLA_V), F32)
        for h in range(GLA_HEADS):
            hm = lane_k == h
            af = lax.dot_general(jnp.where(hm, q_f, 0.0).astype(BF16), k_f, NT, preferred_element_type=F32)
            ab = lax.dot_general(jnp.where(hm, q_b, 0.0).astype(BF16), k_b, NT, preferred_element_type=F32)
            att = jnp.where(mask_f, af, 0.0) + jnp.where(mask_b, ab, 0.0)
            vh = jnp.where(lane_v == h, v, 0.0).astype(BF16)
            o = o + jnp.dot(att.astype(BF16), vh, preferred_element_type=F32)
        q_fb = q_f.astype(BF16)
        for cc in range(2):
            rr = slice(cc * Q, (cc + 1) * Q)
            ro = slice(c * P + cc * Q, c * P + (cc + 1) * Q)
            st = st_ref[...]
            g_last = gf[cc * Q + Q - 1:cc * Q + Q]
            k_end = (k[rr] * jnp.exp(g_last - gf[rr])).astype(BF16)
            op_ref[ro] = o[rr] + lax.dot_general(q_fb[rr], st.astype(BF16), NT, preferred_element_type=F32)
            upd = lax.dot_general(vb[rr], k_end, TN, preferred_element_type=F32)
            st_ref[...] = st * jnp.exp(g_last) + jnp.where(bd_t, upd, 0.0)
            g_first = gb[cc * Q:cc * Q + 1]
            kb_ref[ro] = (k[rr] * jnp.exp(g_first - gb[rr])).astype(BF16)
        qb_ref[r] = q_b.astype(BF16)
        db_ref[r] = jnp.exp(gb)


def _gla_bwd_kernel(op_ref, qb_ref, kb_ref, db_ref, v_ref, go_ref, g_ref, o_ref, st_ref, y_ref, *, tm):
    Q = GLA_CHUNK

    @pl.when(pl.program_id(1) == 0)
    def _():
        st_ref[...] = jnp.zeros(st_ref.shape, F32)

    bd_t = (lax.broadcasted_iota(jnp.int32, (GLA_V, GLA_QK), 0) // GLA_DV ==
            lax.broadcasted_iota(jnp.int32, (GLA_V, GLA_QK), 1) // GLA_DK)
    for c in reversed(range(tm // Q)):
        r = slice(c * Q, (c + 1) * Q)
        st = st_ref[...]
        y_ref[r] = op_ref[r] + lax.dot_general(qb_ref[r], st.astype(BF16), NT, preferred_element_type=F32)
        upd = lax.dot_general(v_ref[r].astype(BF16), kb_ref[r], TN, preferred_element_type=F32)
        st_ref[...] = st * db_ref[c * Q:c * Q + 1] + jnp.where(bd_t, upd, 0.0)
    y = y_ref[...]
    seg = (lax.broadcasted_iota(jnp.int32, (GLA_V, GLA_V), 0) // GLA_DV ==
           lax.broadcasted_iota(jnp.int32, (GLA_V, GLA_V), 1) // GLA_DV).astype(F32)
    ms = jnp.dot(y * y, seg, precision=HIGHEST, preferred_element_type=F32) * (1.0 / GLA_DV)
    o_ref[...] = (y * lax.rsqrt(ms + EPS) * g_ref[...] * _silu(go_ref[...])).astype(BF16)


def gla_mixer(u_gla, batch, seq_len, wg, bg, norm_g_x, tm=512):
    t = u_gla.shape[0]
    nb = seq_len // tm
    row = lambda b, i: (b * nb + i, 0)
    const = lambda b, i: (0, 0)
    outs = pl.pallas_call(
        functools.partial(_gla_fwd_kernel, tm=tm),
        grid=(batch, nb),
        in_specs=[pl.BlockSpec((tm, GLA_PAD), row),
                  pl.BlockSpec((GLA_PAD - GLA_LR0, 2 * GLA_QK), const),
                  pl.BlockSpec((1, 2 * GLA_QK), const)],
        out_specs=[pl.BlockSpec((tm, GLA_V), row), pl.BlockSpec((tm, GLA_QK), row),
                   pl.BlockSpec((tm, GLA_QK), row), pl.BlockSpec((tm, GLA_QK), row)],
        out_shape=[jax.ShapeDtypeStruct((t, GLA_V), F32),
                   jax.ShapeDtypeStruct((t, GLA_QK), BF16),
                   jax.ShapeDtypeStruct((t, GLA_QK), BF16),
                   jax.ShapeDtypeStruct((t, GLA_QK), F32)],
        scratch_shapes=[pltpu.VMEM((GLA_V, GLA_QK), F32)],
        compiler_params=_cparams(2),
        name="gla_fwd",
    )(u_gla, wg, bg)
    rrow = lambda b, i: (b * nb + nb - 1 - i, 0)
    return pl.pallas_call(
        functools.partial(_gla_bwd_kernel, tm=tm),
        grid=(batch, nb),
        in_specs=[pl.BlockSpec((tm, GLA_V), rrow), pl.BlockSpec((tm, GLA_QK), rrow),
                  pl.BlockSpec((tm, GLA_QK), rrow), pl.BlockSpec((tm, GLA_QK), rrow),
                  pl.BlockSpec((tm, GLA_V), lambda b, i: (b * nb + nb - 1 - i, 1)),
                  pl.BlockSpec((tm, GLA_V), lambda b, i: (b * nb + nb - 1 - i, 2)),
                  pl.BlockSpec((1, GLA_V), const)],
        out_specs=pl.BlockSpec((tm, GLA_V), rrow),
        out_shape=jax.ShapeDtypeStruct((t, GLA_V), BF16),
        scratch_shapes=[pltpu.VMEM((GLA_V, GLA_QK), F32), pltpu.VMEM((tm, GLA_V), F32)],
        compiler_params=_cparams(2),
        name="gla_bwd",
    )(*outs, u_gla, u_gla, norm_g_x)


def _out_ffn_kernel(x_ref, y0_ref, y1_ref, y2_ref, y3_ref, wo_ref, g2_ref, wu_ref, wd_ref, o_ref, *, tf):
    mixed = jnp.concatenate([y0_ref[...], y1_ref[...], y2_ref[...], y3_ref[...]], axis=1)
    x1 = x_ref[...] + jnp.dot(mixed, wo_ref[...], preferred_element_type=F32)
    ms = jnp.mean(x1 * x1, axis=-1, keepdims=True)
    h2 = (x1 * lax.rsqrt(ms + EPS) * g2_ref[...]).astype(BF16)
    acc = x1
    for c in range(D_FF // tf):
        up = jnp.dot(h2, wu_ref[:, c * tf:(c + 1) * tf], preferred_element_type=F32)
        a = jnp.square(jnp.maximum(up, 0.0)).astype(BF16)
        acc = acc + jnp.dot(a, wd_ref[c * tf:(c + 1) * tf, :], preferred_element_type=F32)
    o_ref[...] = acc


def out_ffn(x2d, ys, wo, g2, wu, wd, tm=512, tf=1024):
    t = x2d.shape[0]
    const = lambda i: (0, 0)
    single = pl.Buffered(1)
    return pl.pallas_call(
        functools.partial(_out_ffn_kernel, tf=tf),
        grid=(t // tm,),
        in_specs=[pl.BlockSpec((tm, D_MODEL), lambda i: (i, 0))] +
                 [pl.BlockSpec((tm, GROUP_W), lambda i: (i, 0))] * 4 +
                 [pl.BlockSpec((D_MODEL, D_MODEL), const, pipeline_mode=single),
                  pl.BlockSpec((1, D_MODEL), const),
                  pl.BlockSpec((D_MODEL, D_FF), const, pipeline_mode=single),
                  pl.BlockSpec((D_FF, D_MODEL), const, pipeline_mode=single)],
        out_specs=pl.BlockSpec((tm, D_MODEL), lambda i: (i, 0)),
        out_shape=jax.ShapeDtypeStruct((t, D_MODEL), F32),
        compiler_params=_cparams(1),
        name="out_ffn",
    )(x2d, *ys, wo, g2.reshape(1, D_MODEL), wu, wd)


HY_N1 = 128
HY_ROWS = HY_N1 // 2


def _hy_pre_kernel(u_ref, up_ref, un_ref, cw_ref, cb_ref, o_ref, *, tm):
    i = pl.program_id(1)
    nb = pl.num_programs(1)
    x = u_ref[...]
    prev_row = jnp.where(i > 0, up_ref[7:8, :], 0.0)
    next_row = jnp.where(i < nb - 1, un_ref[0:1, :], 0.0)
    rid = lax.broadcasted_iota(jnp.int32, x.shape, 0)
    xm = jnp.where(rid == 0, prev_row, pltpu.roll(x, 1, axis=0))
    xp = jnp.where(rid == tm - 1, next_row, pltpu.roll(x, tm - 1, axis=0))
    uc = xm * cw_ref[0:1] + x * cw_ref[1:2] + xp * cw_ref[2:3] + cb_ref[...]
    o_ref[0] = uc.T


def hy_pre(u_hy, batch, seq_len, conv_w, conv_b, tm=512):
    t = u_hy.shape[0]
    nb = seq_len // tm
    hb = tm // 8
    last_halo = t // 8 - 1
    const = lambda b, i: (0, 0)
    return pl.pallas_call(
        functools.partial(_hy_pre_kernel, tm=tm),
        grid=(batch, nb),
        in_specs=[pl.BlockSpec((tm, HY_IN), lambda b, i: (b * nb + i, 0)),
                  pl.BlockSpec((8, HY_IN), lambda b, i: (jnp.maximum((b * nb + i) * hb - 1, 0), 0)),
                  pl.BlockSpec((8, HY_IN), lambda b, i: (jnp.minimum((b * nb + i + 1) * hb, last_halo), 0)),
                  pl.BlockSpec((3, HY_IN), const),
                  pl.BlockSpec((1, HY_IN), const)],
        out_specs=pl.BlockSpec((1, HY_IN, tm), lambda b, i: (b, 0, i)),
        out_shape=jax.ShapeDtypeStruct((batch, HY_IN, seq_len), F32),
        compiler_params=_cparams(2),
        name="hy_pre",
    )(u_hy, u_hy, u_hy, conv_w, conv_b.reshape(1, HY_IN))


def _hy_conv_kernel(z_ref, g_ref, sk_ref, h_ref, f1_ref, w2_ref, t_ref, gi_ref, o_ref, *, cb, n2):
    ct = z_ref.shape[1]
    f1 = f1_ref[...]
    w2 = w2_ref[...]
    tre, tim = t_ref[0:HY_N1], t_ref[HY_N1:]
    gi = gi_ref[...]

    def quads(blk):
        return blk[0:HY_N1, 0:n2], blk[0:HY_N1, n2:], blk[HY_N1:, 0:n2], blk[HY_N1:, n2:]

    def body(it, carry):
        c0 = it * cb
        planes = []
        for u in range(cb):
            a = jnp.dot(f1, z_ref[0, c0 + u].astype(BF16), preferred_element_type=F32)
            are, aim = a[0:HY_N1], a[HY_N1:]
            planes += [(are * tre - aim * tim).astype(BF16), (are * tim + aim * tre).astype(BF16)]
        spec = jnp.dot(jnp.concatenate(planes, axis=0), w2, preferred_element_type=F32)
        planes = []
        for u in range(cb):
            tl, tr, bl, br = quads(spec[u * 2 * HY_N1:(u + 1) * 2 * HY_N1])
            bre, bim = tl - br, tr + bl
            hre, him = h_ref[c0 + u, 0:HY_N1], h_ref[c0 + u, HY_N1:]
            planes += [(bre * hre - bim * him).astype(BF16), (bre * him + bim * hre).astype(BF16)]
        back = jnp.dot(jnp.concatenate(planes, axis=0), w2, preferred_element_type=F32)
        for u in range(cb):
            tl, tr, bl, br = quads(back[u * 2 * HY_N1:(u + 1) * 2 * HY_N1])
            are, aim = tl + br, bl - tr
            st = jnp.concatenate([are * tre + aim * tim, aim * tre - are * tim], axis=0).astype(BF16)
            y = jnp.dot(gi, st, preferred_element_type=F32)
            o_ref[0, c0 + u] = g_ref[0, c0 + u] * (y + z_ref[0, c0 + u] * sk_ref[c0 + u])
        return carry

    lax.fori_loop(0, ct // cb, body, 0)


def hy_conv(z4, z_off, gate4, gate_off, skip, h_spec, consts, ct=32, cb=2):
    b, _, _, n2 = z4.shape
    f1, w2, tw, gi = consts
    nct = HY_C // ct
    seq = lambda off: (lambda bi, j: (bi, off // ct + j, 0, 0))
    const = lambda bi, j: (0, 0)
    return pl.pallas_call(
        functools.partial(_hy_conv_kernel, cb=cb, n2=n2),
        grid=(b, nct),
        in_specs=[pl.BlockSpec((1, ct, HY_ROWS, n2), seq(z_off)),
                  pl.BlockSpec((1, ct, HY_ROWS, n2), seq(gate_off)),
                  pl.BlockSpec((ct, 1, 1), lambda bi, j: (j, 0, 0)),
                  pl.BlockSpec((ct, 2 * HY_N1, n2), lambda bi, j: (j, 0, 0)),
                  pl.BlockSpec((2 * HY_N1, HY_ROWS), const),
                  pl.BlockSpec((n2, 2 * n2), const),
                  pl.BlockSpec((2 * HY_N1, n2), const),
                  pl.BlockSpec((HY_ROWS, 2 * HY_N1), const)],
        out_specs=pl.BlockSpec((1, ct, HY_ROWS, n2), lambda bi, j: (bi, j, 0, 0)),
        out_shape=jax.ShapeDtypeStruct((b, HY_C, HY_ROWS, n2), F32),
        compiler_params=_cparams(2),
        name="hy_conv",
    )(z4, gate4, skip.reshape(HY_C, 1, 1), h_spec, f1, w2, tw, gi)


def _hy_post_kernel(z_ref, g_ref, o_ref):
    x = z_ref[0]
    ms = jnp.mean(x * x, axis=0, keepdims=True)
    o_ref[...] = (x * lax.rsqrt(ms + EPS) * g_ref[...]).T.astype(BF16)


def hy_post(z3, g, tm=512):
    b, _, seq_len = z3.shape
    nb = seq_len // tm
    return pl.pallas_call(
        _hy_post_kernel,
        grid=(b, nb),
        in_specs=[pl.BlockSpec((1, HY_C, tm), lambda bi, i: (bi, 0, i)),
                  pl.BlockSpec((HY_C, 1), lambda bi, i: (0, 0))],
        out_specs=pl.BlockSpec((tm, HY_C), lambda bi, i: (bi * nb + i, 0)),
        out_shape=jax.ShapeDtypeStruct((b * seq_len, HY_C), BF16),
        compiler_params=_cparams(2),
        name="hy_post",
    )(z3, g.reshape(HY_C, 1))


def _dft_consts(n2):
    n = HY_N1 * n2
    k1 = np.arange(HY_N1, dtype=np.float64)
    a1 = 2.0 * np.pi * np.outer(k1, np.arange(HY_ROWS)) / HY_N1
    f1 = np.concatenate([np.cos(a1), -np.sin(a1)], axis=0)
    m2 = np.arange(n2, dtype=np.float64)
    a2 = 2.0 * np.pi * np.outer(m2, m2) / n2
    w2 = np.concatenate([np.cos(a2), -np.sin(a2)], axis=1)
    at = 2.0 * np.pi * np.outer(k1, m2) / n
    tw = np.concatenate([np.cos(at), -np.sin(at)], axis=0)
    return (jnp.asarray(f1, BF16), jnp.asarray(w2, BF16), jnp.asarray(tw, F32), jnp.asarray(f1.T, BF16))


def _hy_taps_kernel(z_ref, w1_ref, b1_ref, w2_ref, b2_ref, fr_ref, w3_ref, dl_ref, o_ref, hid_ref):
    @pl.when((pl.program_id(0) == 0) & (pl.program_id(1) == 0))
    def _():
        fr = fr_ref[...]
        hid = jnp.sin(fr * (jnp.dot(w1_ref[...], z_ref[...], precision=HIGHEST,
                                    preferred_element_type=F32) + b1_ref[...]))
        hid_ref[...] = jnp.sin(fr * (jnp.dot(w2_ref[...], hid, precision=HIGHEST,
                                             preferred_element_type=F32) + b2_ref[...]))

    hid = hid_ref[...]
    dec = jnp.exp(-z_ref[0:1, :] * jnp.abs(dl_ref[...]))
    h = [jnp.dot(w3_ref[0, d], hid, precision=HIGHEST, preferred_element_type=F32) * dec for d in range(2)]
    norm = (jnp.sum(jnp.abs(h[0]), axis=1, keepdims=True) + jnp.sum(jnp.abs(h[1]), axis=1, keepdims=True))
    for d in range(2):
        o_ref[0, d] = h[d] / norm


def hy_taps(L, w1, b1, w2, b2, w3, freq, rb=32):
    t = jnp.linspace(0.0, 1.0, L, dtype=F32)[:, None]
    w = 2.0 * math.pi * jnp.arange(L, dtype=F32) / L
    bands = jnp.linspace(1e-4, HY_BANDS - 1, HY_BANDS, dtype=F32)
    ang = w[:, None] * bands
    zt = jnp.concatenate([t, jnp.cos(ang), -jnp.sin(ang)], axis=-1).T
    deltas = jnp.linspace(HY_MIN_DECAY, HY_MAX_DECAY, HY_C, dtype=F32).reshape(HY_C, 1)
    hid_n = w2.shape[0]
    emb = zt.shape[0]
    col = lambda v: v.reshape(hid_n, 1)
    w3t = w3.T.reshape(HY_ORDER, 2, HY_C, hid_n)
    const = lambda o, j: (0, 0)
    return pl.pallas_call(
        _hy_taps_kernel,
        grid=(HY_ORDER, HY_C // rb),
        in_specs=[pl.BlockSpec((emb, L), const),
                  pl.BlockSpec((hid_n, emb), const), pl.BlockSpec((hid_n, 1), const),
                  pl.BlockSpec((hid_n, hid_n), const), pl.BlockSpec((hid_n, 1), const),
                  pl.BlockSpec((hid_n, 1), const),
                  pl.BlockSpec((1, 2, rb, hid_n), lambda o, j: (o, 0, j, 0)),
                  pl.BlockSpec((rb, 1), lambda o, j: (j, 0))],
        out_specs=pl.BlockSpec((1, 2, rb, L), lambda o, j: (o, 0, j, 0)),
        out_shape=jax.ShapeDtypeStruct((HY_ORDER, 2, HY_C, L), F32),
        scratch_shapes=[pltpu.VMEM((hid_n, L), F32)],
        compiler_params=_cparams(2),
        name="hy_taps",
    )(zt, w1.T, col(b1), w2.T, col(b2), col(freq), w3t, deltas)


def _hy_spec_kernel(h_ref, f1_ref, w2_ref, t_ref, o_ref, *, cb, n2):
    ct = h_ref.shape[2]
    f1 = f1_ref[...]
    w2 = w2_ref[...]
    tre, tim = t_ref[0:HY_N1], t_ref[HY_N1:]
    inv_n = 1.0 / (HY_N1 * n2)

    def body(it, carry):
        c0 = it * cb
        planes = []
        for u in range(cb):
            for d in range(2):
                a = jnp.dot(f1, h_ref[0, d, c0 + u].astype(BF16), preferred_element_type=F32)
                are, aim = a[0:HY_N1], a[HY_N1:]
                planes += [(are * tre - aim * tim).astype(BF16), (are * tim + aim * tre).astype(BF16)]
        spec = jnp.dot(jnp.concatenate(planes, axis=0), w2, preferred_element_type=F32)
        for u in range(cb):
            re, im = [], []
            for d in range(2):
                blk = spec[(2 * u + d) * 2 * HY_N1:(2 * u + d + 1) * 2 * HY_N1]
                tl, tr, bl, br = blk[0:HY_N1, 0:n2], blk[0:HY_N1, n2:], blk[HY_N1:, 0:n2], blk[HY_N1:, n2:]
                re.append(tl - br)
                im.append(tr + bl)
            o_ref[0, c0 + u, 0:HY_N1] = (re[0] + re[1]) * inv_n
            o_ref[0, c0 + u, HY_N1:] = (im[0] - im[1]) * inv_n
        return carry

    lax.fori_loop(0, ct // cb, body, 0)


def hy_spec(taps, consts, ct=32, cb=2):
    L = taps.shape[-1]
    n2 = 2 * L // HY_N1
    f1, w2, tw, _ = consts
    const = lambda o, j: (0, 0)
    return pl.pallas_call(
        functools.partial(_hy_spec_kernel, cb=cb, n2=n2),
        grid=(HY_ORDER, HY_C // ct),
        in_specs=[pl.BlockSpec((1, 2, ct, HY_ROWS, n2), lambda o, j: (o, 0, j, 0, 0)),
                  pl.BlockSpec((2 * HY_N1, HY_ROWS), const),
                  pl.BlockSpec((n2, 2 * n2), const),
                  pl.BlockSpec((2 * HY_N1, n2), const)],
        out_specs=pl.BlockSpec((1, ct, 2 * HY_N1, n2), lambda o, j: (o, j, 0, 0)),
        out_shape=jax.ShapeDtypeStruct((HY_ORDER, HY_C, 2 * HY_N1, n2), F32),
        compiler_params=_cparams(2),
        name="hy_spec",
    )(taps.reshape(HY_ORDER, 2, HY_C, HY_ROWS, n2), f1, w2, tw)


def _hyena_filter_freq(L, w1, b1, w2, b2, w3, freq):
    return hy_spec(hy_taps(L, w1, b1, w2, b2, w3, freq), _dft_consts(2 * L // HY_N1))


def hyena_mixer(u_hy, batch, seq_len, p, h_spec):
    n2 = 2 * seq_len // HY_N1
    consts = _dft_consts(n2)
    uct = hy_pre(u_hy, batch, seq_len, p['hy_conv_w'], p['hy_conv_b'])
    uc4 = uct.reshape(batch, HY_IN, HY_ROWS, n2)
    z = hy_conv(uc4, 0, uc4, HY_C, p['hy_skip'][0], h_spec[0], consts)
    z = hy_conv(z, 0, uc4, 2 * HY_C, p['hy_skip'][1], h_spec[1], consts)
    return hy_post(z.reshape(batch, HY_C, seq_len), p['hy_out_g'])


_DEINTERLEAVE = np.concatenate([np.arange(0, ATT_HEAD_DIM, 2), np.arange(1, ATT_HEAD_DIM, 2)])


def _pack_w_in(w):
    c0 = 0
    w_hy = w[:, c0:c0 + HY_IN]; c0 += HY_IN
    w_ssd = w[:, c0:c0 + SSD_IN]; c0 += SSD_IN
    w_gla = w[:, c0:c0 + GLA_IN]; c0 += GLA_IN
    w_att = w[:, c0:c0 + ATT_IN]
    zeros = lambda n: jnp.zeros((D_MODEL, n), w.dtype)
    cols = [w_hy, w_ssd[:, :SSD_INNER + SSD_XBC],
            jnp.repeat(w_ssd[:, SSD_INNER + SSD_XBC:], SSD_HEAD_DIM, axis=1),
            w_gla, zeros(GLA_PAD - GLA_IN)]
    for h in range(ATT_HEADS):
        wh = w_att[:, h * ATT_HEAD_DIM:(h + 1) * ATT_HEAD_DIM][:, _DEINTERLEAVE]
        cols += [wh, zeros(64)] if h < 2 else [zeros(64), wh]
    kq = ATT_HEADS * ATT_HEAD_DIM
    for g in range(ATT_KV_HEADS):
        cols.append(w_att[:, kq + g * ATT_HEAD_DIM:kq + (g + 1) * ATT_HEAD_DIM][:, _DEINTERLEAVE])
    cols.append(w_att[:, kq + ATT_KV_HEADS * ATT_HEAD_DIM:])
    return jnp.concatenate(cols, axis=1).astype(BF16)


def _rope_tables(seq_len):
    rows = seq_len // GRID_W
    row = jnp.repeat(jnp.arange(rows, dtype=F32), GRID_W)
    col = jnp.tile(jnp.arange(GRID_W, dtype=F32), rows)
    inv_freq = 1.0 / (ROPE_THETA ** (jnp.arange(0, ROPE_AXIS_DIM, 2, dtype=F32) / ROPE_AXIS_DIM))
    ang = jnp.concatenate([row[:, None] * inv_freq, col[:, None] * inv_freq], axis=-1)
    cos, sin = jnp.cos(ang), jnp.sin(ang)
    return jnp.tile(cos, (1, 4)), jnp.tile(jnp.concatenate([-sin, sin], axis=-1), (1, 2))


def _layer_params(i, s):
    p = {name: val[i] for name, val in s.items()}
    p['w_in_p'] = _pack_w_in(p['w_in'])
    p['w_out_b'] = p['w_out'].astype(BF16)
    p['w_up_b'] = p['w_up'].astype(BF16)
    p['w_down_b'] = p['w_down'].astype(BF16)
    p['ssd_dt_bias_x'] = jnp.repeat(p['ssd_dt_bias'].reshape(-1), SSD_HEAD_DIM).reshape(1, SSD_DTX)
    p['ssd_a_x'] = jnp.repeat(-jnp.exp(p['ssd_A_log']).reshape(-1), SSD_HEAD_DIM).reshape(1, SSD_DTX)
    p['ssd_d_x'] = jnp.repeat(p['ssd_D'], SSD_HEAD_DIM).reshape(1, SSD_INNER)
    wg = jnp.zeros((GLA_PAD - GLA_LR0, 2 * GLA_QK), F32)
    wg = wg.at[0:GLA_RANK, 0:GLA_QK].set(p['gla_gk_w'][0])
    wg = wg.at[GLA_RANK:2 * GLA_RANK, GLA_QK:].set(p['gla_gk_w'][1])
    p['gla_wg'] = wg
    p['gla_bg'] = p['gla_gk_b'].reshape(1, 2 * GLA_QK)
    p['gla_norm_g_x'] = jnp.tile(p['gla_norm_g'], GLA_HEADS).reshape(1, GLA_V)
    p['att_gq'] = jnp.tile(p['att_q_norm_g'][_DEINTERLEAVE], 2).reshape(1, LANES)
    p['att_gk'] = jnp.tile(p['att_k_norm_g'][_DEINTERLEAVE], 2).reshape(1, LANES)
    return p


def _layer(x, tables, p):
    b, L, _ = x.shape
    x2d = x.reshape(b * L, D_MODEL)
    u_hy, u_ssd, u_gla, u_att = in_proj(x2d, p['ln1_g'], p['w_in_p'])

    hf = _hyena_filter_freq(L, p['hy_ffn_w1'], p['hy_ffn_b1'], p['hy_ffn_w2'], p['hy_ffn_b2'],
                            p['hy_ffn_w3'], p['hy_sin_freq'])
    y_hy = hyena_mixer(u_hy, b, L, p, hf)
    y_ssd = ssd_mixer(u_ssd, b, L, p['ssd_conv_w'], p['ssd_conv_b'], p['ssd_dt_bias_x'], p['ssd_a_x'],
                      p['ssd_d_x'], p['ssd_norm_g'])
    y_gla = gla_mixer(u_gla, b, L, p['gla_wg'], p['gla_bg'], p['gla_norm_g_x'])
    q, k, vt = att_prep(u_att, tables[0], tables[1], p['att_gq'], p['att_gk'], b, L)
    y_att = flash_attention(q.reshape(b, L, ATT_Q_PAD), k.reshape(b, L, LANES), vt, p['att_out_g'])

    y = out_ffn(x2d, (y_hy, y_ssd, y_gla, y_att.reshape(b * L, GROUP_W)), p['w_out_b'], p['ln2_g'],
                p['w_up_b'], p['w_down_b'])
    return y.reshape(b, L, D_MODEL)


def kernel(x_prompt, x_sample, ln1_g, w_in, hy_conv_w, hy_conv_b, hy_ffn_w1, hy_ffn_b1, hy_ffn_w2, hy_ffn_b2, hy_ffn_w3, hy_sin_freq, hy_skip, hy_out_g, ssd_conv_w, ssd_conv_b, ssd_A_log, ssd_dt_bias, ssd_D, ssd_norm_g, gla_gk_w, gla_gk_b, gla_norm_g, att_q_norm_g, att_k_norm_g, att_out_g, w_out, ln2_g, w_up, w_down):
    stacked = dict(ln1_g=ln1_g, w_in=w_in, hy_conv_w=hy_conv_w, hy_conv_b=hy_conv_b, hy_ffn_w1=hy_ffn_w1,
                   hy_ffn_b1=hy_ffn_b1, hy_ffn_w2=hy_ffn_w2, hy_ffn_b2=hy_ffn_b2, hy_ffn_w3=hy_ffn_w3,
                   hy_sin_freq=hy_sin_freq, hy_skip=hy_skip, hy_out_g=hy_out_g, ssd_conv_w=ssd_conv_w,
                   ssd_conv_b=ssd_conv_b, ssd_A_log=ssd_A_log, ssd_dt_bias=ssd_dt_bias, ssd_D=ssd_D,
                   ssd_norm_g=ssd_norm_g, gla_gk_w=gla_gk_w, gla_gk_b=gla_gk_b, gla_norm_g=gla_norm_g,
                   att_q_norm_g=att_q_norm_g, att_k_norm_g=att_k_norm_g, att_out_g=att_out_g,
                   w_out=w_out, ln2_g=ln2_g, w_up=w_up, w_down=w_down)
    tables_p = _rope_tables(x_prompt.shape[1])
    tables_s = _rope_tables(x_sample.shape[1])
    y_prompt, y_sample = x_prompt, x_sample
    for i in range(DEPTH):
        p = _layer_params(i, stacked)
        y_prompt = _layer(y_prompt, tables_p, p)
        y_sample = _layer(y_sample, tables_s, p)
    return (y_prompt, y_sample)
```

```python
import functools
import math

import jax
import jax.numpy as jnp
import numpy as np
from jax import lax
from jax.experimental import pallas as pl
from jax.experimental.pallas import tpu as pltpu

D_MODEL = 1024
DEPTH = 4
GRID_W = 64
EPS = 1e-6
GROUP_W = 256
D_FF = 4096

HY_C = 256
HY_ORDER = 2
HY_BANDS = 8
HY_MIN_DECAY = math.log(1e-2) / 1.5
HY_MAX_DECAY = math.log(1e-2) / 0.3
HY_IN = 768

SSD_HEAD_DIM = 64
SSD_HEADS = 4
SSD_GROUPS = 2
SSD_STATE = 128
SSD_CHUNK = 128
SSD_INNER = 256
SSD_BC = 256
SSD_XBC = 768
SSD_IN = 1032

GLA_HEADS = 4
GLA_DV = 64
GLA_DK = 32
GLA_RANK = 16
GLA_NORMALIZER = 16.0
GLA_CHUNK = 64
GLA_QK = 128
GLA_V = 256
GLA_IN = 800

ATT_HEAD_DIM = 64
ATT_HEADS = 4
ATT_KV_HEADS = 2
ROPE_THETA = 10000.0
ROPE_AXIS_DIM = 32
ATT_IN = 512

LANES = 128
SSD_DTX = 2 * SSD_HEADS * SSD_HEAD_DIM
SSD_PAD = SSD_INNER + SSD_XBC + SSD_DTX
GLA_PAD = 896
GLA_LR0 = 2 * GLA_QK + 2 * GLA_V
ATT_Q_PAD = ATT_HEADS * LANES
ATT_PAD = ATT_Q_PAD + 2 * LANES
ATT_VT_ROWS = ATT_HEAD_DIM + 16
ATT_ROW_CHUNK = 64
IN_PAD = HY_IN + SSD_PAD + GLA_PAD + ATT_PAD

VMEM_LIMIT = 48 * 1024 * 1024
F32 = jnp.float32
BF16 = jnp.bfloat16
HIGHEST = lax.Precision.HIGHEST
NT = (((1,), (1,)), ((), ()))
TN = (((0,), (0,)), ((), ()))


def _cparams(n_grid):
    return pltpu.CompilerParams(dimension_semantics=("arbitrary",) * n_grid,
                                vmem_limit_bytes=VMEM_LIMIT)


def _softplus(x):
    return jnp.maximum(x, 0.0) + jnp.log1p(jnp.exp(-jnp.abs(x)))


def _silu(x):
    return x * jax.nn.sigmoid(x)


def _in_proj_kernel(x_ref, g_ref, w_ref, hy_ref, ssd_ref, gla_ref, att_ref):
    x = x_ref[...]
    ms = jnp.mean(x * x, axis=-1, keepdims=True)
    h = (x * lax.rsqrt(ms + EPS) * g_ref[...]).astype(BF16)
    c0 = 0
    for ref in (hy_ref, ssd_ref, gla_ref, att_ref):
        n = ref.shape[-1]
        ref[...] = jnp.dot(h, w_ref[:, c0:c0 + n], preferred_element_type=F32)
        c0 += n


def in_proj(x2d, g, w_p, tm=256):
    t = x2d.shape[0]
    widths = (HY_IN, SSD_PAD, GLA_PAD, ATT_PAD)
    return pl.pallas_call(
        _in_proj_kernel,
        grid=(t // tm,),
        in_specs=[pl.BlockSpec((tm, D_MODEL), lambda i: (i, 0)),
                  pl.BlockSpec((1, D_MODEL), lambda i: (0, 0)),
                  pl.BlockSpec((D_MODEL, IN_PAD), lambda i: (0, 0), pipeline_mode=pl.Buffered(1))],
        out_specs=[pl.BlockSpec((tm, n), lambda i: (i, 0)) for n in widths],
        out_shape=[jax.ShapeDtypeStruct((t, n), F32) for n in widths],
        compiler_params=_cparams(1),
        name="in_proj",
    )(x2d, g.reshape(1, D_MODEL), w_p)


def _rope(x, cos_t, sin_t):
    lane = lax.broadcasted_iota(jnp.int32, x.shape, 1)
    nxt = pltpu.roll(x, LANES - 32, axis=1)
    prv = pltpu.roll(x, 32, axis=1)
    sw = jnp.where((lane % 64) < 32, nxt, prv)
    return x * cos_t + sw * sin_t


def _att_prep_kernel(u_ref, cos_ref, sin_ref, gq_ref, gk_ref, q_ref, k_ref, vt_ref):
    cos_t = cos_ref[...]
    sin_t = sin_ref[...]
    inv = 1.0 / ATT_HEAD_DIM
    qscale = (ATT_HEAD_DIM ** -0.5) * math.log2(math.e)
    for h in range(ATT_HEADS):
        x = u_ref[:, h * LANES:(h + 1) * LANES]
        ms = jnp.sum(x * x, axis=-1, keepdims=True) * inv
        xn = x * lax.rsqrt(ms + EPS) * gq_ref[...]
        q_ref[:, h * LANES:(h + 1) * LANES] = (_rope(xn, cos_t, sin_t) * qscale).astype(BF16)
    k = u_ref[:, ATT_Q_PAD:ATT_Q_PAD + LANES]
    lane = lax.broadcasted_iota(jnp.int32, k.shape, 1)
    k2 = k * k
    s_all = jnp.sum(k2, axis=-1, keepdims=True)
    s_lo = jnp.sum(jnp.where(lane < 64, k2, 0.0), axis=-1, keepdims=True)
    ms = jnp.where(lane < 64, s_lo, s_all - s_lo) * inv
    kn = k * lax.rsqrt(ms + EPS) * gk_ref[...]
    k_ref[...] = _rope(kn, cos_t, sin_t).astype(BF16)
    vt = u_ref[:, ATT_Q_PAD + LANES:ATT_PAD].T
    rid = lax.broadcasted_iota(jnp.int32, (ATT_VT_ROWS - ATT_HEAD_DIM, vt.shape[1]), 0)
    tail = jnp.where(rid == 0, 1.0, 0.0)
    for g in range(ATT_KV_HEADS):
        vt_ref[0, g] = jnp.concatenate(
            [vt[g * ATT_HEAD_DIM:(g + 1) * ATT_HEAD_DIM], tail], axis=0).astype(BF16)


def att_prep(u_att, cos_t, sin_t, gq, gk, batch, seq_len, tm=512):
    t = u_att.shape[0]
    nl = seq_len // tm
    return pl.pallas_call(
        _att_prep_kernel,
        grid=(t // tm,),
        in_specs=[pl.BlockSpec((tm, ATT_PAD), lambda i: (i, 0)),
                  pl.BlockSpec((tm, LANES), lambda i: (i % nl, 0)),
                  pl.BlockSpec((tm, LANES), lambda i: (i % nl, 0)),
                  pl.BlockSpec((1, LANES), lambda i: (0, 0)),
                  pl.BlockSpec((1, LANES), lambda i: (0, 0))],
        out_specs=[pl.BlockSpec((tm, ATT_Q_PAD), lambda i: (i, 0)),
                   pl.BlockSpec((tm, LANES), lambda i: (i, 0)),
                   pl.BlockSpec((1, ATT_KV_HEADS, ATT_VT_ROWS, tm), lambda i: (i // nl, 0, 0, i % nl))],
        out_shape=[jax.ShapeDtypeStruct((t, ATT_Q_PAD), BF16),
                   jax.ShapeDtypeStruct((t, LANES), BF16),
                   jax.ShapeDtypeStruct((batch, ATT_KV_HEADS, ATT_VT_ROWS, seq_len), BF16)],
        compiler_params=_cparams(1),
        name="att_prep",
    )(u_att, cos_t, sin_t, gq, gk)


def _flash_kernel(q_ref, k_ref, vt_ref, g_ref, o_ref, m_ref, acc_ref, st_ref, p_ref, *, tk, nk):
    m_ref[...] = jnp.full(m_ref.shape, -jnp.inf, F32)
    acc_ref[...] = jnp.zeros(acc_ref.shape, F32)

    def scores(j, slot, h):
        off = pl.multiple_of(j * tk, tk)
        k = k_ref[0, pl.ds(off, tk), :]
        q = q_ref[0, :, h * LANES:(h + 1) * LANES]
        st_ref[slot, h] = lax.dot_general(k, q, NT, preferred_element_type=F32)

    def consume(j, slot, h):
        off = pl.multiple_of(j * tk, tk)
        rows = [slice(r0, r0 + ATT_ROW_CHUNK) for r0 in range(0, tk, ATT_ROW_CHUNK)]
        m_prev = m_ref[h]
        m_new = m_prev
        for r in rows:
            m_new = jnp.maximum(m_new, jnp.max(st_ref[slot, h, r], axis=0, keepdims=True))
        alpha = jnp.exp2(m_prev - m_new)
        for r in rows:
            p_ref[h, r] = jnp.exp2((st_ref[slot, h, r] - m_new).astype(BF16))
        vt = vt_ref[0, h // 2, :, pl.ds(off, tk)]
        acc_ref[h] = alpha * acc_ref[h] + jnp.dot(vt, p_ref[h], preferred_element_type=F32)
        m_ref[h] = m_new

    for h in range(ATT_HEADS):
        scores(0, 0, h)

    def body(jj, carry):
        j = 2 * jj
        j2 = jnp.minimum(j + 2, nk - 1)
        for h in range(ATT_HEADS):
            scores(j + 1, 1, h)
            consume(j, 0, h)
        for h in range(ATT_HEADS):
            scores(j2, 0, h)
            consume(j + 1, 1, h)
        return carry

    lax.fori_loop(0, nk // 2, body, 0)
    ot = jnp.concatenate([acc_ref[h, 0:ATT_HEAD_DIM] / acc_ref[h, ATT_HEAD_DIM:ATT_HEAD_DIM + 1]
                          for h in range(ATT_HEADS)], axis=0)
    o = ot.T
    ms = jnp.mean(o * o, axis=-1, keepdims=True)
    o_ref[0] = (o * lax.rsqrt(ms + EPS) * g_ref[...]).astype(BF16)


def flash_attention(q, k, vt, g_out, tq=512, tk=512):
    b, seq_len, _ = q.shape
    return pl.pallas_call(
        functools.partial(_flash_kernel, tk=tk, nk=seq_len // tk),
        grid=(b, seq_len // tq),
        in_specs=[pl.BlockSpec((1, tq, ATT_Q_PAD), lambda bi, i: (bi, i, 0)),
                  pl.BlockSpec((1, seq_len, LANES), lambda bi, i: (bi, 0, 0)),
                  pl.BlockSpec((1, ATT_KV_HEADS, ATT_VT_ROWS, seq_len), lambda bi, i: (bi, 0, 0, 0)),
                  pl.BlockSpec((1, GROUP_W), lambda bi, i: (0, 0))],
        out_specs=pl.BlockSpec((1, tq, GROUP_W), lambda bi, i: (bi, i, 0)),
        out_shape=jax.ShapeDtypeStruct((b, seq_len, GROUP_W), BF16),
        scratch_shapes=[pltpu.VMEM((ATT_HEADS, 1, tq), F32),
                        pltpu.VMEM((ATT_HEADS, ATT_VT_ROWS, tq), F32),
                        pltpu.VMEM((2, ATT_HEADS, tk, tq), F32),
                        pltpu.VMEM((ATT_HEADS, tk, tq), BF16)],
        compiler_params=_cparams(2),
        name="flash_attention",
    )(q, k, vt, g_out.reshape(1, GROUP_W))


def _tri(n, upper, block=None):
    ii = lax.broadcasted_iota(jnp.int32, (n, n), 0)
    jj = lax.broadcasted_iota(jnp.int32, (n, n), 1)
    m = (jj >= ii) if upper else (jj <= ii)
    if block is not None:
        m = m & ((ii // block) == (jj // block))
    return m


def _tri_dot(t, x):
    hi = x.astype(BF16)
    r1 = x - hi.astype(F32)
    mid = r1.astype(BF16)
    lo = (r1 - mid.astype(F32)).astype(BF16)
    n = x.shape[1]
    y = jnp.dot(t, jnp.concatenate([hi, mid, lo], axis=1), preferred_element_type=F32)
    return (y[:, 0:n] + y[:, n:2 * n]) + y[:, 2 * n:]


def _ssd_fwd_kernel(u_ref, up_ref, un_ref, cw_ref, cb_ref, dtb_ref, ax_ref, dx_ref,
                    yp_ref, cm_ref, bm_ref, wxb_ref, eb_ref, s_ref, *, tm):
    Q = SSD_CHUNK
    i = pl.program_id(1)
    nb = pl.num_programs(1)

    @pl.when(i == 0)
    def _():
        s_ref[...] = jnp.zeros(s_ref.shape, F32)

    x = u_ref[:, SSD_INNER:SSD_INNER + SSD_XBC]
    prev_row = jnp.where(i > 0, up_ref[7:8, SSD_INNER:SSD_INNER + SSD_XBC], 0.0)
    next_row = jnp.where(i < nb - 1, un_ref[0:1, SSD_INNER:SSD_INNER + SSD_XBC], 0.0)
    rid = lax.broadcasted_iota(jnp.int32, x.shape, 0)
    xm = jnp.where(rid == 0, prev_row, pltpu.roll(x, 1, axis=0))
    xp = jnp.where(rid == tm - 1, next_row, pltpu.roll(x, tm - 1, axis=0))
    conv = xm * cw_ref[0:1] + x * cw_ref[1:2] + xp * cw_ref[2:3] + cb_ref[...]
    xbc = _silu(conv)
    xs = xbc[:, 0:SSD_INNER]
    bm = xbc[:, SSD_INNER:SSD_INNER + SSD_BC].astype(BF16)
    cm = xbc[:, SSD_INNER + SSD_BC:].astype(BF16)
    bm_ref[...] = bm
    cm_ref[...] = cm
    dt = _softplus(u_ref[:, SSD_INNER + SSD_XBC:] + dtb_ref[...])
    da = dt * ax_ref[...]

    tl = _tri(Q, False).astype(BF16)
    tu = _tri(Q, True).astype(BF16)
    ii = lax.broadcasted_iota(jnp.int32, (Q, Q), 0)
    jj = lax.broadcasted_iota(jnp.int32, (Q, Q), 1)
    lo_half = lax.broadcasted_iota(jnp.int32, (Q, LANES), 1) < SSD_HEAD_DIM
    ninf = jnp.float32(-jnp.inf)
    state = [s_ref[g] for g in range(SSD_GROUPS)]

    for c in range(tm // Q):
        r = slice(c * Q, (c + 1) * Q)
        cum = _tri_dot(tl, da[r, 0:SSD_INNER])
        rc = _tri_dot(tu, da[r, SSD_INNER:])
        for g in range(SSD_GROUPS):
            ln = slice(g * LANES, (g + 1) * LANES)
            cumg, rcg = cum[:, ln], rc[:, ln]
            dtf, dtbk = dt[r, ln], dt[r, SSD_INNER + g * LANES:SSD_INNER + (g + 1) * LANES]
            xsg = xs[r, ln]
            xsg_b = xsg.astype(BF16)
            bg, cg = bm[r, ln], cm[r, ln]
            cb = lax.dot_general(cg, bg, NT, preferred_element_type=F32)
            cum_t, rc_t, dtf_t, dtb_t = cumg.T, rcg.T, dtf.T, dtbk.T
            cum_sw, rc_sw = pltpu.roll(cumg, SSD_HEAD_DIM, axis=1), pltpu.roll(rcg, SSD_HEAD_DIM, axis=1)
            yd = []
            for hh in range(2):
                row = slice(hh * SSD_HEAD_DIM, hh * SSD_HEAD_DIM + 1)
                col_c = jnp.where(lo_half, cumg, cum_sw) if hh == 0 else jnp.where(lo_half, cum_sw, cumg)
                col_r = jnp.where(lo_half, rcg, rc_sw) if hh == 0 else jnp.where(lo_half, rc_sw, rcg)
                ef = jnp.exp(jnp.where(jj <= ii, col_c - cum_t[row], ninf)) * dtf_t[row]
                eb = jnp.exp(jnp.where(jj >= ii, col_r - rc_t[row], ninf)) * dtb_t[row]
                w = (cb * (ef + eb)).astype(BF16)
                yd.append(jnp.dot(w, xsg_b, preferred_element_type=F32))
            y = jnp.where(lo_half, yd[0], yd[1])
            last = cumg[Q - 1:Q]
            wxf = (xsg * jnp.exp(last - cumg) * dtf).astype(BF16)
            y = y + jnp.dot(cg, state[g].astype(BF16), preferred_element_type=F32) * jnp.exp(cumg)
            state[g] = state[g] * jnp.exp(last) + lax.dot_general(bg, wxf, TN, preferred_element_type=F32)
            yp_ref[r, ln] = y + dx_ref[:, ln] * xsg
            wxb_ref[r, ln] = (xsg * jnp.exp(rcg[0:1] - rcg) * dtbk).astype(BF16)
            eb_ref[r, ln] = jnp.exp(rcg)
    for g in range(SSD_GROUPS):
        s_ref[g] = state[g]


def _ssd_bwd_kernel(yp_ref, cm_ref, bm_ref, wxb_ref, eb_ref, z_ref, g_ref, o_ref, s_ref, y_ref, *, tm):
    Q = SSD_CHUNK

    @pl.when(pl.program_id(1) == 0)
    def _():
        s_ref[...] = jnp.zeros(s_ref.shape, F32)

    state = [s_ref[g] for g in range(SSD_GROUPS)]
    for c in reversed(range(tm // Q)):
        r = slice(c * Q, (c + 1) * Q)
        for g in range(SSD_GROUPS):
            ln = slice(g * LANES, (g + 1) * LANES)
            ebg = eb_ref[r, ln]
            y_ref[r, ln] = yp_ref[r, ln] + jnp.dot(cm_ref[r, ln], state[g].astype(BF16),
                                                   preferred_element_type=F32) * ebg
            state[g] = state[g] * ebg[0:1] + lax.dot_general(bm_ref[r, ln], wxb_ref[r, ln], TN,
                                                             preferred_element_type=F32)
    for g in range(SSD_GROUPS):
        s_ref[g] = state[g]
    y = y_ref[...] * _silu(z_ref[...])
    ms = jnp.mean(y * y, axis=-1, keepdims=True)
    o_ref[...] = (y * lax.rsqrt(ms + EPS) * g_ref[...]).astype(BF16)


def ssd_mixer(u_ssd, batch, seq_len, conv_w, conv_b, dt_bias_x, a_x, d_x, norm_g, tm=512):
    t = u_ssd.shape[0]
    nb = seq_len // tm
    hb = tm // 8
    last_halo = t // 8 - 1
    row = lambda b, i: (b * nb + i, 0)
    const = lambda b, i: (0, 0)
    outs = pl.pallas_call(
        functools.partial(_ssd_fwd_kernel, tm=tm),
        grid=(batch, nb),
        in_specs=[pl.BlockSpec((tm, SSD_PAD), row),
                  pl.BlockSpec((8, SSD_PAD), lambda b, i: (jnp.maximum((b * nb + i) * hb - 1, 0), 0)),
                  pl.BlockSpec((8, SSD_PAD), lambda b, i: (jnp.minimum((b * nb + i + 1) * hb, last_halo), 0)),
                  pl.BlockSpec((3, SSD_XBC), const),
                  pl.BlockSpec((1, SSD_XBC), const),
                  pl.BlockSpec((1, SSD_DTX), const),
                  pl.BlockSpec((1, SSD_DTX), const),
                  pl.BlockSpec((1, SSD_INNER), const)],
        out_specs=[pl.BlockSpec((tm, SSD_INNER), row)] * 5,
        out_shape=[jax.ShapeDtypeStruct((t, SSD_INNER), F32),
                   jax.ShapeDtypeStruct((t, SSD_INNER), BF16),
                   jax.ShapeDtypeStruct((t, SSD_INNER), BF16),
                   jax.ShapeDtypeStruct((t, SSD_INNER), BF16),
                   jax.ShapeDtypeStruct((t, SSD_INNER), F32)],
        scratch_shapes=[pltpu.VMEM((SSD_GROUPS, SSD_STATE, LANES), F32)],
        compiler_params=_cparams(2),
        name="ssd_fwd",
    )(u_ssd, u_ssd, u_ssd, conv_w, conv_b.reshape(1, SSD_XBC), dt_bias_x, a_x, d_x)
    rrow = lambda b, i: (b * nb + nb - 1 - i, 0)
    return pl.pallas_call(
        functools.partial(_ssd_bwd_kernel, tm=tm),
        grid=(batch, nb),
        in_specs=[pl.BlockSpec((tm, SSD_INNER), rrow)] * 6 + [pl.BlockSpec((1, SSD_INNER), const)],
        out_specs=pl.BlockSpec((tm, SSD_INNER), rrow),
        out_shape=jax.ShapeDtypeStruct((t, SSD_INNER), BF16),
        scratch_shapes=[pltpu.VMEM((SSD_GROUPS, SSD_STATE, LANES), F32),
                        pltpu.VMEM((tm, SSD_INNER), F32)],
        compiler_params=_cparams(2),
        name="ssd_bwd",
    )(*outs, u_ssd, norm_g.reshape(1, SSD_INNER))


def _gla_fwd_kernel(u_ref, wg_ref, bg_ref, op_ref, qb_ref, kb_ref, db_ref, st_ref, *, tm):
    Q = GLA_CHUNK
    P = 2 * Q

    @pl.when(pl.program_id(1) == 0)
    def _():
        st_ref[...] = jnp.zeros(st_ref.shape, F32)

    pre = jnp.dot(u_ref[:, GLA_LR0:GLA_PAD], wg_ref[...], precision=HIGHEST,
                  preferred_element_type=F32) + bg_ref[...]
    gk = (jnp.minimum(pre, 0.0) - jnp.log1p(jnp.exp(-jnp.abs(pre)))) * (1.0 / GLA_NORMALIZER)

    mask_f = _tri(P, False, Q)
    mask_b = _tri(P, True, Q)
    tl = mask_f.astype(BF16)
    tu = mask_b.astype(BF16)
    lane_k = lax.broadcasted_iota(jnp.int32, (P, GLA_QK), 1) // GLA_DK
    lane_v = lax.broadcasted_iota(jnp.int32, (P, GLA_V), 1) // GLA_DV
    bd_t = (lax.broadcasted_iota(jnp.int32, (GLA_V, GLA_QK), 0) // GLA_DV ==
            lax.broadcasted_iota(jnp.int32, (GLA_V, GLA_QK), 1) // GLA_DK)

    st = st_ref[...]
    for c in range(tm // P):
        r = slice(c * P, (c + 1) * P)
        gf = _tri_dot(tl, gk[r, 0:GLA_QK])
        gb = _tri_dot(tu, gk[r, GLA_QK:])
        qs = u_ref[r, 0:GLA_QK] * (GLA_DK ** -0.5)
        k = u_ref[r, GLA_QK:2 * GLA_QK]
        v = u_ref[r, 2 * GLA_QK:2 * GLA_QK + GLA_V]
        vb = v.astype(BF16)
        q_f = qs * jnp.exp(gf)
        k_f = (k * jnp.exp(-gf)).astype(BF16)
        q_b = qs * jnp.exp(gb)
        k_b = (k * jnp.exp(-gb)).astype(BF16)
        heads = range(GLA_HEADS)
        qf_h = jnp.concatenate([jnp.where(lane_k == h, q_f, 0.0) for h in heads], axis=0).astype(BF16)
        qb_h = jnp.concatenate([jnp.where(lane_k == h, q_b, 0.0) for h in heads], axis=0).astype(BF16)
        af = lax.dot_general(qf_h, k_f, NT, preferred_element_type=F32)
        ab = lax.dot_general(qb_h, k_b, NT, preferred_element_type=F32)
        att = jnp.concatenate([jnp.where(mask_f, af[h * P:(h + 1) * P], 0.0) +
                               jnp.where(mask_b, ab[h * P:(h + 1) * P], 0.0) for h in heads], axis=1)
        v_h = jnp.concatenate([jnp.where(lane_v == h, v, 0.0) for h in heads], axis=0).astype(BF16)
        o = jnp.dot(att.astype(BF16), v_h, preferred_element_type=F32)
        q_fb = q_f.astype(BF16)
        for cc in range(2):
            rr = slice(cc * Q, (cc + 1) * Q)
            ro = slice(c * P + cc * Q, c * P + (cc + 1) * Q)
            g_last = gf[cc * Q + Q - 1:cc * Q + Q]
            k_end = (k[rr] * jnp.exp(g_last - gf[rr])).astype(BF16)
            op_ref[ro] = o[rr] + lax.dot_general(q_fb[rr], st.astype(BF16), NT, preferred_element_type=F32)
            upd = lax.dot_general(vb[rr], k_end, TN, preferred_element_type=F32)
            st = st * jnp.exp(g_last) + jnp.where(bd_t, upd, 0.0)
            g_first = gb[cc * Q:cc * Q + 1]
            kb_ref[ro] = (k[rr] * jnp.exp(g_first - gb[rr])).astype(BF16)
        qb_ref[r] = q_b.astype(BF16)
        db_ref[r] = jnp.exp(gb)
    st_ref[...] = st


def _gla_bwd_kernel(op_ref, qb_ref, kb_ref, db_ref, v_ref, go_ref, g_ref, o_ref, st_ref, y_ref, *, tm):
    Q = GLA_CHUNK

    @pl.when(pl.program_id(1) == 0)
    def _():
        st_ref[...] = jnp.zeros(st_ref.shape, F32)

    bd_t = (lax.broadcasted_iota(jnp.int32, (GLA_V, GLA_QK), 0) // GLA_DV ==
            lax.broadcasted_iota(jnp.int32, (GLA_V, GLA_QK), 1) // GLA_DK)
    st = st_ref[...]
    for c in reversed(range(tm // Q)):
        r = slice(c * Q, (c + 1) * Q)
        y_ref[r] = op_ref[r] + lax.dot_general(qb_ref[r], st.astype(BF16), NT, preferred_element_type=F32)
        upd = lax.dot_general(v_ref[r].astype(BF16), kb_ref[r], TN, preferred_element_type=F32)
        st = st * db_ref[c * Q:c * Q + 1] + jnp.where(bd_t, upd, 0.0)
    st_ref[...] = st
    y = y_ref[...]
    seg = (lax.broadcasted_iota(jnp.int32, (GLA_V, GLA_V), 0) // GLA_DV ==
           lax.broadcasted_iota(jnp.int32, (GLA_V, GLA_V), 1) // GLA_DV).astype(F32)
    ms = jnp.dot(y * y, seg, precision=HIGHEST, preferred_element_type=F32) * (1.0 / GLA_DV)
    o_ref[...] = (y * lax.rsqrt(ms + EPS) * g_ref[...] * _silu(go_ref[...])).astype(BF16)


def gla_mixer(u_gla, batch, seq_len, wg, bg, norm_g_x, tm=512):
    t = u_gla.shape[0]
    nb = seq_len // tm
    row = lambda b, i: (b * nb + i, 0)
    const = lambda b, i: (0, 0)
    outs = pl.pallas_call(
        functools.partial(_gla_fwd_kernel, tm=tm),
        grid=(batch, nb),
        in_specs=[pl.BlockSpec((tm, GLA_PAD), row),
                  pl.BlockSpec((GLA_PAD - GLA_LR0, 2 * GLA_QK), const),
                  pl.BlockSpec((1, 2 * GLA_QK), const)],
        out_specs=[pl.BlockSpec((tm, GLA_V), row), pl.BlockSpec((tm, GLA_QK), row),
                   pl.BlockSpec((tm, GLA_QK), row), pl.BlockSpec((tm, GLA_QK), row)],
        out_shape=[jax.ShapeDtypeStruct((t, GLA_V), F32),
                   jax.ShapeDtypeStruct((t, GLA_QK), BF16),
                   jax.ShapeDtypeStruct((t, GLA_QK), BF16),
                   jax.ShapeDtypeStruct((t, GLA_QK), F32)],
        scratch_shapes=[pltpu.VMEM((GLA_V, GLA_QK), F32)],
        compiler_params=_cparams(2),
        name="gla_fwd",
    )(u_gla, wg, bg)
    rrow = lambda b, i: (b * nb + nb - 1 - i, 0)
    return pl.pallas_call(
        functools.partial(_gla_bwd_kernel, tm=tm),
        grid=(batch, nb),
        in_specs=[pl.BlockSpec((tm, GLA_V), rrow), pl.BlockSpec((tm, GLA_QK), rrow),
                  pl.BlockSpec((tm, GLA_QK), rrow), pl.BlockSpec((tm, GLA_QK), rrow),
                  pl.BlockSpec((tm, GLA_V), lambda b, i: (b * nb + nb - 1 - i, 1)),
                  pl.BlockSpec((tm, GLA_V), lambda b, i: (b * nb + nb - 1 - i, 2)),
                  pl.BlockSpec((1, GLA_V), const)],
        out_specs=pl.BlockSpec((tm, GLA_V), rrow),
        out_shape=jax.ShapeDtypeStruct((t, GLA_V), BF16),
        scratch_shapes=[pltpu.VMEM((GLA_V, GLA_QK), F32), pltpu.VMEM((tm, GLA_V), F32)],
        compiler_params=_cparams(2),
        name="gla_bwd",
    )(*outs, u_gla, u_gla, norm_g_x)


def _out_ffn_kernel(x_ref, y0_ref, y1_ref, y2_ref, y3_ref, wo_ref, g2_ref, wu_ref, wd_ref, o_ref, *, tf):
    mixed = jnp.concatenate([y0_ref[...], y1_ref[...], y2_ref[...], y3_ref[...]], axis=1)
    x1 = x_ref[...] + jnp.dot(mixed, wo_ref[...], preferred_element_type=F32)
    ms = jnp.mean(x1 * x1, axis=-1, keepdims=True)
    h2 = (x1 * lax.rsqrt(ms + EPS) * g2_ref[...]).astype(BF16)
    acc = x1
    for c in range(D_FF // tf):
        up = jnp.dot(h2, wu_ref[:, c * tf:(c + 1) * tf], preferred_element_type=F32)
        a = jnp.square(jnp.maximum(up, 0.0)).astype(BF16)
        acc = acc + jnp.dot(a, wd_ref[c * tf:(c + 1) * tf, :], preferred_element_type=F32)
    o_ref[...] = acc


def out_ffn(x2d, ys, wo, g2, wu, wd, tm=512, tf=1024):
    t = x2d.shape[0]
    const = lambda i: (0, 0)
    single = pl.Buffered(1)
    return pl.pallas_call(
        functools.partial(_out_ffn_kernel, tf=tf),
        grid=(t // tm,),
        in_specs=[pl.BlockSpec((tm, D_MODEL), lambda i: (i, 0))] +
                 [pl.BlockSpec((tm, GROUP_W), lambda i: (i, 0))] * 4 +
                 [pl.BlockSpec((D_MODEL, D_MODEL), const, pipeline_mode=single),
                  pl.BlockSpec((1, D_MODEL), const),
                  pl.BlockSpec((D_MODEL, D_FF), const, pipeline_mode=single),
                  pl.BlockSpec((D_FF, D_MODEL), const, pipeline_mode=single)],
        out_specs=pl.BlockSpec((tm, D_MODEL), lambda i: (i, 0)),
        out_shape=jax.ShapeDtypeStruct((t, D_MODEL), F32),
        compiler_params=_cparams(1),
        name="out_ffn",
    )(x2d, *ys, wo, g2.reshape(1, D_MODEL), wu, wd)


HY_N1 = 128
HY_ROWS = HY_N1 // 2


def _hy_pre_kernel(u_ref, up_ref, un_ref, cw_ref, cb_ref, o_ref, *, tm):
    i = pl.program_id(1)
    nb = pl.num_programs(1)
    x = u_ref[...]
    prev_row = jnp.where(i > 0, up_ref[7:8, :], 0.0)
    next_row = jnp.where(i < nb - 1, un_ref[0:1, :], 0.0)
    rid = lax.broadcasted_iota(jnp.int32, x.shape, 0)
    xm = jnp.where(rid == 0, prev_row, pltpu.roll(x, 1, axis=0))
    xp = jnp.where(rid == tm - 1, next_row, pltpu.roll(x, tm - 1, axis=0))
    uc = xm * cw_ref[0:1] + x * cw_ref[1:2] + xp * cw_ref[2:3] + cb_ref[...]
    o_ref[0] = uc.T


def hy_pre(u_hy, batch, seq_len, conv_w, conv_b, tm=512):
    t = u_hy.shape[0]
    nb = seq_len // tm
    hb = tm // 8
    last_halo = t // 8 - 1
    const = lambda b, i: (0, 0)
    return pl.pallas_call(
        functools.partial(_hy_pre_kernel, tm=tm),
        grid=(batch, nb),
        in_specs=[pl.BlockSpec((tm, HY_IN), lambda b, i: (b * nb + i, 0)),
                  pl.BlockSpec((8, HY_IN), lambda b, i: (jnp.maximum((b * nb + i) * hb - 1, 0), 0)),
                  pl.BlockSpec((8, HY_IN), lambda b, i: (jnp.minimum((b * nb + i + 1) * hb, last_halo), 0)),
                  pl.BlockSpec((3, HY_IN), const),
                  pl.BlockSpec((1, HY_IN), const)],
        out_specs=pl.BlockSpec((1, HY_IN, tm), lambda b, i: (b, 0, i)),
        out_shape=jax.ShapeDtypeStruct((batch, HY_IN, seq_len), F32),
        compiler_params=_cparams(2),
        name="hy_pre",
    )(u_hy, u_hy, u_hy, conv_w, conv_b.reshape(1, HY_IN))


def _hy_conv_kernel(z_ref, g_ref, sk_ref, h_ref, f1_ref, w2_ref, t_ref, gi_ref, o_ref, *, cb, n2):
    ct = z_ref.shape[1]
    f1 = f1_ref[...]
    w2 = w2_ref[...]
    tre, tim = t_ref[0:HY_N1], t_ref[HY_N1:]
    gi = gi_ref[...]

    def quads(blk):
        return blk[0:HY_N1, 0:n2], blk[0:HY_N1, n2:], blk[HY_N1:, 0:n2], blk[HY_N1:, n2:]

    def body(it, carry):
        c0 = it * cb
        planes = []
        for u in range(cb):
            a = jnp.dot(f1, z_ref[0, c0 + u].astype(BF16), preferred_element_type=F32)
            are, aim = a[0:HY_N1], a[HY_N1:]
            planes += [(are * tre - aim * tim).astype(BF16), (are * tim + aim * tre).astype(BF16)]
        spec = jnp.dot(jnp.concatenate(planes, axis=0), w2, preferred_element_type=F32)
        planes = []
        for u in range(cb):
            tl, tr, bl, br = quads(spec[u * 2 * HY_N1:(u + 1) * 2 * HY_N1])
            bre, bim = tl - br, tr + bl
            hre, him = h_ref[c0 + u, 0:HY_N1], h_ref[c0 + u, HY_N1:]
            planes += [(bre * hre - bim * him).astype(BF16), (bre * him + bim * hre).astype(BF16)]
        back = jnp.dot(jnp.concatenate(planes, axis=0), w2, preferred_element_type=F32)
        for u in range(cb):
            tl, tr, bl, br = quads(back[u * 2 * HY_N1:(u + 1) * 2 * HY_N1])
            are, aim = tl + br, bl - tr
            st = jnp.concatenate([are * tre + aim * tim, aim * tre - are * tim], axis=0).astype(BF16)
            y = jnp.dot(gi, st, preferred_element_type=F32)
            o_ref[0, c0 + u] = g_ref[0, c0 + u] * (y + z_ref[0, c0 + u] * sk_ref[c0 + u])
        return carry

    lax.fori_loop(0, ct // cb, body, 0)


def hy_conv(z4, z_off, gate4, gate_off, skip, h_spec, consts, ct=32, cb=8):
    b, _, _, n2 = z4.shape
    f1, w2, tw, gi = consts
    nct = HY_C // ct
    seq = lambda off: (lambda bi, j: (bi, off // ct + j, 0, 0))
    const = lambda bi, j: (0, 0)
    return pl.pallas_call(
        functools.partial(_hy_conv_kernel, cb=cb, n2=n2),
        grid=(b, nct),
        in_specs=[pl.BlockSpec((1, ct, HY_ROWS, n2), seq(z_off)),
                  pl.BlockSpec((1, ct, HY_ROWS, n2), seq(gate_off)),
                  pl.BlockSpec((ct, 1, 1), lambda bi, j: (j, 0, 0)),
                  pl.BlockSpec((ct, 2 * HY_N1, n2), lambda bi, j: (j, 0, 0)),
                  pl.BlockSpec((2 * HY_N1, HY_ROWS), const),
                  pl.BlockSpec((n2, 2 * n2), const),
                  pl.BlockSpec((2 * HY_N1, n2), const),
                  pl.BlockSpec((HY_ROWS, 2 * HY_N1), const)],
        out_specs=pl.BlockSpec((1, ct, HY_ROWS, n2), lambda bi, j: (bi, j, 0, 0)),
        out_shape=jax.ShapeDtypeStruct((b, HY_C, HY_ROWS, n2), F32),
        compiler_params=_cparams(2),
        name="hy_conv",
    )(z4, gate4, skip.reshape(HY_C, 1, 1), h_spec, f1, w2, tw, gi)


def _hy_post_kernel(z_ref, g_ref, o_ref):
    x = z_ref[0]
    ms = jnp.mean(x * x, axis=0, keepdims=True)
    o_ref[...] = (x * lax.rsqrt(ms + EPS) * g_ref[...]).T.astype(BF16)


def hy_post(z3, g, tm=512):
    b, _, seq_len = z3.shape
    nb = seq_len // tm
    return pl.pallas_call(
        _hy_post_kernel,
        grid=(b, nb),
        in_specs=[pl.BlockSpec((1, HY_C, tm), lambda bi, i: (bi, 0, i)),
                  pl.BlockSpec((HY_C, 1), lambda bi, i: (0, 0))],
        out_specs=pl.BlockSpec((tm, HY_C), lambda bi, i: (bi * nb + i, 0)),
        out_shape=jax.ShapeDtypeStruct((b * seq_len, HY_C), BF16),
        compiler_params=_cparams(2),
        name="hy_post",
    )(z3, g.reshape(HY_C, 1))


def _dft_consts(n2):
    n = HY_N1 * n2
    k1 = np.arange(HY_N1, dtype=np.float64)
    a1 = 2.0 * np.pi * np.outer(k1, np.arange(HY_ROWS)) / HY_N1
    f1 = np.concatenate([np.cos(a1), -np.sin(a1)], axis=0)
    m2 = np.arange(n2, dtype=np.float64)
    a2 = 2.0 * np.pi * np.outer(m2, m2) / n2
    w2 = np.concatenate([np.cos(a2), -np.sin(a2)], axis=1)
    at = 2.0 * np.pi * np.outer(k1, m2) / n
    tw = np.concatenate([np.cos(at), -np.sin(at)], axis=0)
    return (jnp.asarray(f1, BF16), jnp.asarray(w2, BF16), jnp.asarray(tw, F32), jnp.asarray(f1.T, BF16))


def _hy_taps_kernel(z_ref, w1_ref, b1_ref, w2_ref, b2_ref, fr_ref, w3_ref, dl_ref, o_ref, hid_ref):
    @pl.when((pl.program_id(0) == 0) & (pl.program_id(1) == 0))
    def _():
        fr = fr_ref[...]
        hid = jnp.sin(fr * (jnp.dot(w1_ref[...], z_ref[...], precision=HIGHEST,
                                    preferred_element_type=F32) + b1_ref[...]))
        hid_ref[...] = jnp.sin(fr * (jnp.dot(w2_ref[...], hid, precision=HIGHEST,
                                             preferred_element_type=F32) + b2_ref[...]))

    hid = hid_ref[...]
    dec = jnp.exp(-z_ref[0:1, :] * jnp.abs(dl_ref[...]))
    h = [jnp.dot(w3_ref[0, d], hid, precision=HIGHEST, preferred_element_type=F32) * dec for d in range(2)]
    norm = (jnp.sum(jnp.abs(h[0]), axis=1, keepdims=True) + jnp.sum(jnp.abs(h[1]), axis=1, keepdims=True))
    for d in range(2):
        o_ref[0, d] = h[d] / norm


def hy_taps(L, w1, b1, w2, b2, w3, freq, rb=32):
    t = jnp.linspace(0.0, 1.0, L, dtype=F32)[:, None]
    w = 2.0 * math.pi * jnp.arange(L, dtype=F32) / L
    bands = jnp.linspace(1e-4, HY_BANDS - 1, HY_BANDS, dtype=F32)
    ang = w[:, None] * bands
    zt = jnp.concatenate([t, jnp.cos(ang), -jnp.sin(ang)], axis=-1).T
    deltas = jnp.linspace(HY_MIN_DECAY, HY_MAX_DECAY, HY_C, dtype=F32).reshape(HY_C, 1)
    hid_n = w2.shape[0]
    emb = zt.shape[0]
    col = lambda v: v.reshape(hid_n, 1)
    w3t = w3.T.reshape(HY_ORDER, 2, HY_C, hid_n)
    const = lambda o, j: (0, 0)
    return pl.pallas_call(
        _hy_taps_kernel,
        grid=(HY_ORDER, HY_C // rb),
        in_specs=[pl.BlockSpec((emb, L), const),
                  pl.BlockSpec((hid_n, emb), const), pl.BlockSpec((hid_n, 1), const),
                  pl.BlockSpec((hid_n, hid_n), const), pl.BlockSpec((hid_n, 1), const),
                  pl.BlockSpec((hid_n, 1), const),
                  pl.BlockSpec((1, 2, rb, hid_n), lambda o, j: (o, 0, j, 0)),
                  pl.BlockSpec((rb, 1), lambda o, j: (j, 0))],
        out_specs=pl.BlockSpec((1, 2, rb, L), lambda o, j: (o, 0, j, 0)),
        out_shape=jax.ShapeDtypeStruct((HY_ORDER, 2, HY_C, L), F32),
        scratch_shapes=[pltpu.VMEM((hid_n, L), F32)],
        compiler_params=_cparams(2),
        name="hy_taps",
    )(zt, w1.T, col(b1), w2.T, col(b2), col(freq), w3t, deltas)


def _hy_spec_kernel(h_ref, f1_ref, w2_ref, t_ref, o_ref, *, cb, n2):
    ct = h_ref.shape[2]
    f1 = f1_ref[...]
    w2 = w2_ref[...]
    tre, tim = t_ref[0:HY_N1], t_ref[HY_N1:]
    inv_n = 1.0 / (HY_N1 * n2)

    def body(it, carry):
        c0 = it * cb
        planes = []
        for u in range(cb):
            for d in range(2):
                a = jnp.dot(f1, h_ref[0, d, c0 + u].astype(BF16), preferred_element_type=F32)
                are, aim = a[0:HY_N1], a[HY_N1:]
                planes += [(are * tre - aim * tim).astype(BF16), (are * tim + aim * tre).astype(BF16)]
        spec = jnp.dot(jnp.concatenate(planes, axis=0), w2, preferred_element_type=F32)
        for u in range(cb):
            re, im = [], []
            for d in range(2):
                blk = spec[(2 * u + d) * 2 * HY_N1:(2 * u + d + 1) * 2 * HY_N1]
                tl, tr, bl, br = blk[0:HY_N1, 0:n2], blk[0:HY_N1, n2:], blk[HY_N1:, 0:n2], blk[HY_N1:, n2:]
                re.append(tl - br)
                im.append(tr + bl)
            o_ref[0, c0 + u, 0:HY_N1] = (re[0] + re[1]) * inv_n
            o_ref[0, c0 + u, HY_N1:] = (im[0] - im[1]) * inv_n
        return carry

    lax.fori_loop(0, ct // cb, body, 0)


def hy_spec(taps, consts, ct=32, cb=4):
    L = taps.shape[-1]
    n2 = 2 * L // HY_N1
    f1, w2, tw, _ = consts
    const = lambda o, j: (0, 0)
    return pl.pallas_call(
        functools.partial(_hy_spec_kernel, cb=cb, n2=n2),
        grid=(HY_ORDER, HY_C // ct),
        in_specs=[pl.BlockSpec((1, 2, ct, HY_ROWS, n2), lambda o, j: (o, 0, j, 0, 0)),
                  pl.BlockSpec((2 * HY_N1, HY_ROWS), const),
                  pl.BlockSpec((n2, 2 * n2), const),
                  pl.BlockSpec((2 * HY_N1, n2), const)],
        out_specs=pl.BlockSpec((1, ct, 2 * HY_N1, n2), lambda o, j: (o, j, 0, 0)),
        out_shape=jax.ShapeDtypeStruct((HY_ORDER, HY_C, 2 * HY_N1, n2), F32),
        compiler_params=_cparams(2),
        name="hy_spec",
    )(taps.reshape(HY_ORDER, 2, HY_C, HY_ROWS, n2), f1, w2, tw)


def _hyena_filter_freq(L, w1, b1, w2, b2, w3, freq):
    return hy_spec(hy_taps(L, w1, b1, w2, b2, w3, freq), _dft_consts(2 * L // HY_N1))


def hyena_mixer(u_hy, batch, seq_len, p, h_spec):
    n2 = 2 * seq_len // HY_N1
    consts = _dft_consts(n2)
    uct = hy_pre(u_hy, batch, seq_len, p['hy_conv_w'], p['hy_conv_b'])
    uc4 = uct.reshape(batch, HY_IN, HY_ROWS, n2)
    z = hy_conv(uc4, 0, uc4, HY_C, p['hy_skip'][0], h_spec[0], consts)
    z = hy_conv(z, 0, uc4, 2 * HY_C, p['hy_skip'][1], h_spec[1], consts)
    return hy_post(z.reshape(batch, HY_C, seq_len), p['hy_out_g'])


_DEINTERLEAVE = np.concatenate([np.arange(0, ATT_HEAD_DIM, 2), np.arange(1, ATT_HEAD_DIM, 2)])


def _pack_w_in(w):
    c0 = 0
    w_hy = w[:, c0:c0 + HY_IN]; c0 += HY_IN
    w_ssd = w[:, c0:c0 + SSD_IN]; c0 += SSD_IN
    w_gla = w[:, c0:c0 + GLA_IN]; c0 += GLA_IN
    w_att = w[:, c0:c0 + ATT_IN]
    zeros = lambda n: jnp.zeros((D_MODEL, n), w.dtype)
    cols = [w_hy, w_ssd[:, :SSD_INNER + SSD_XBC],
            jnp.repeat(w_ssd[:, SSD_INNER + SSD_XBC:], SSD_HEAD_DIM, axis=1),
            w_gla, zeros(GLA_PAD - GLA_IN)]
    for h in range(ATT_HEADS):
        wh = w_att[:, h * ATT_HEAD_DIM:(h + 1) * ATT_HEAD_DIM][:, _DEINTERLEAVE]
        cols += [wh, zeros(64)] if h < 2 else [zeros(64), wh]
    kq = ATT_HEADS * ATT_HEAD_DIM
    for g in range(ATT_KV_HEADS):
        cols.append(w_att[:, kq + g * ATT_HEAD_DIM:kq + (g + 1) * ATT_HEAD_DIM][:, _DEINTERLEAVE])
    cols.append(w_att[:, kq + ATT_KV_HEADS * ATT_HEAD_DIM:])
    return jnp.concatenate(cols, axis=1).astype(BF16)


def _rope_tables(seq_len):
    rows = seq_len // GRID_W
    row = jnp.repeat(jnp.arange(rows, dtype=F32), GRID_W)
    col = jnp.tile(jnp.arange(GRID_W, dtype=F32), rows)
    inv_freq = 1.0 / (ROPE_THETA ** (jnp.arange(0, ROPE_AXIS_DIM, 2, dtype=F32) / ROPE_AXIS_DIM))
    ang = jnp.concatenate([row[:, None] * inv_freq, col[:, None] * inv_freq], axis=-1)
    cos, sin = jnp.cos(ang), jnp.sin(ang)
    return jnp.tile(cos, (1, 4)), jnp.tile(jnp.concatenate([-sin, sin], axis=-1), (1, 2))


def _layer_params(i, s):
    p = {name: val[i] for name, val in s.items()}
    p['w_in_p'] = _pack_w_in(p['w_in'])
    p['w_out_b'] = p['w_out'].astype(BF16)
    p['w_up_b'] = p['w_up'].astype(BF16)
    p['w_down_b'] = p['w_down'].astype(BF16)
    p['ssd_dt_bias_x'] = jnp.repeat(p['ssd_dt_bias'].reshape(-1), SSD_HEAD_DIM).reshape(1, SSD_DTX)
    p['ssd_a_x'] = jnp.repeat(-jnp.exp(p['ssd_A_log']).reshape(-1), SSD_HEAD_DIM).reshape(1, SSD_DTX)
    p['ssd_d_x'] = jnp.repeat(p['ssd_D'], SSD_HEAD_DIM).reshape(1, SSD_INNER)
    wg = jnp.zeros((GLA_PAD - GLA_LR0, 2 * GLA_QK), F32)
    wg = wg.at[0:GLA_RANK, 0:GLA_QK].set(p['gla_gk_w'][0])
    wg = wg.at[GLA_RANK:2 * GLA_RANK, GLA_QK:].set(p['gla_gk_w'][1])
    p['gla_wg'] = wg
    p['gla_bg'] = p['gla_gk_b'].reshape(1, 2 * GLA_QK)
    p['gla_norm_g_x'] = jnp.tile(p['gla_norm_g'], GLA_HEADS).reshape(1, GLA_V)
    p['att_gq'] = jnp.tile(p['att_q_norm_g'][_DEINTERLEAVE], 2).reshape(1, LANES)
    p['att_gk'] = jnp.tile(p['att_k_norm_g'][_DEINTERLEAVE], 2).reshape(1, LANES)
    return p


def _layer(x, tables, p):
    b, L, _ = x.shape
    x2d = x.reshape(b * L, D_MODEL)
    u_hy, u_ssd, u_gla, u_att = in_proj(x2d, p['ln1_g'], p['w_in_p'])

    hf = _hyena_filter_freq(L, p['hy_ffn_w1'], p['hy_ffn_b1'], p['hy_ffn_w2'], p['hy_ffn_b2'],
                            p['hy_ffn_w3'], p['hy_sin_freq'])
    y_hy = hyena_mixer(u_hy, b, L, p, hf)
    y_ssd = ssd_mixer(u_ssd, b, L, p['ssd_conv_w'], p['ssd_conv_b'], p['ssd_dt_bias_x'], p['ssd_a_x'],
                      p['ssd_d_x'], p['ssd_norm_g'])
    y_gla = gla_mixer(u_gla, b, L, p['gla_wg'], p['gla_bg'], p['gla_norm_g_x'])
    q, k, vt = att_prep(u_att, tables[0], tables[1], p['att_gq'], p['att_gk'], b, L)
    y_att = flash_attention(q.reshape(b, L, ATT_Q_PAD), k.reshape(b, L, LANES), vt, p['att_out_g'])

    y = out_ffn(x2d, (y_hy, y_ssd, y_gla, y_att.reshape(b * L, GROUP_W)), p['w_out_b'], p['ln2_g'],
                p['w_up_b'], p['w_down_b'])
    return y.reshape(b, L, D_MODEL)


def kernel(x_prompt, x_sample, ln1_g, w_in, hy_conv_w, hy_conv_b, hy_ffn_w1, hy_ffn_b1, hy_ffn_w2, hy_ffn_b2, hy_ffn_w3, hy_sin_freq, hy_skip, hy_out_g, ssd_conv_w, ssd_conv_b, ssd_A_log, ssd_dt_bias, ssd_D, ssd_norm_g, gla_gk_w, gla_gk_b, gla_norm_g, att_q_norm_g, att_k_norm_g, att_out_g, w_out, ln2_g, w_up, w_down):
    stacked = dict(ln1_g=ln1_g, w_in=w_in, hy_conv_w=hy_conv_w, hy_conv_b=hy_conv_b, hy_ffn_w1=hy_ffn_w1,
                   hy_ffn_b1=hy_ffn_b1, hy_ffn_w2=hy_ffn_w2, hy_ffn_b2=hy_ffn_b2, hy_ffn_w3=hy_ffn_w3,
                   hy_sin_freq=hy_sin_freq, hy_skip=hy_skip, hy_out_g=hy_out_g, ssd_conv_w=ssd_conv_w,
                   ssd_conv_b=ssd_conv_b, ssd_A_log=ssd_A_log, ssd_dt_bias=ssd_dt_bias, ssd_D=ssd_D,
                   ssd_norm_g=ssd_norm_g, gla_gk_w=gla_gk_w, gla_gk_b=gla_gk_b, gla_norm_g=gla_norm_g,
                   att_q_norm_g=att_q_norm_g, att_k_norm_g=att_k_norm_g, att_out_g=att_out_g,
                   w_out=w_out, ln2_g=ln2_g, w_up=w_up, w_down=w_down)
    tables_p = _rope_tables(x_prompt.shape[1])
    tables_s = _rope_tables(x_sample.shape[1])
    y_prompt, y_sample = x_prompt, x_sample
    for i in range(DEPTH):
        p = _layer_params(i, stacked)
        y_prompt = _layer(y_prompt, tables_p, p)
        y_sample = _layer(y_sample, tables_s, p)
    return (y_prompt, y_sample)
```

```python
import functools
import math

import jax
import jax.numpy as jnp
import numpy as np
from jax import lax
from jax.experimental import pallas as pl
from jax.experimental.pallas import tpu as pltpu

D_MODEL = 1024
DEPTH = 4
GRID_W = 64
EPS = 1e-6
GROUP_W = 256
D_FF = 4096

HY_C = 256
HY_ORDER = 2
HY_BANDS = 8
HY_MIN_DECAY = math.log(1e-2) / 1.5
HY_MAX_DECAY = math.log(1e-2) / 0.3
HY_IN = 768

SSD_HEAD_DIM = 64
SSD_HEADS = 4
SSD_GROUPS = 2
SSD_STATE = 128
SSD_CHUNK = 128
SSD_INNER = 256
SSD_BC = 256
SSD_XBC = 768
SSD_IN = 1032

GLA_HEADS = 4
GLA_DV = 64
GLA_DK = 32
GLA_RANK = 16
GLA_NORMALIZER = 16.0
GLA_CHUNK = 64
GLA_QK = 128
GLA_V = 256
GLA_IN = 800

ATT_HEAD_DIM = 64
ATT_HEADS = 4
ATT_KV_HEADS = 2
ROPE_THETA = 10000.0
ROPE_AXIS_DIM = 32
ATT_IN = 512

LANES = 128
SSD_DTX = 2 * SSD_HEADS * SSD_HEAD_DIM
SSD_PAD = SSD_INNER + SSD_XBC + SSD_DTX
GLA_PAD = 896
GLA_LR0 = 2 * GLA_QK + 2 * GLA_V
ATT_Q_PAD = ATT_HEADS * LANES
ATT_PAD = ATT_Q_PAD + 2 * LANES
ATT_VT_ROWS = ATT_HEAD_DIM + 16
ATT_ROW_CHUNK = 64
IN_PAD = HY_IN + SSD_PAD + GLA_PAD + ATT_PAD

VMEM_LIMIT = 48 * 1024 * 1024
F32 = jnp.float32
BF16 = jnp.bfloat16
HIGHEST = lax.Precision.HIGHEST
NT = (((1,), (1,)), ((), ()))
TN = (((0,), (0,)), ((), ()))


def _cparams(n_grid):
    return pltpu.CompilerParams(dimension_semantics=("arbitrary",) * n_grid,
                                vmem_limit_bytes=VMEM_LIMIT)


def _softplus(x):
    return jnp.maximum(x, 0.0) + jnp.log1p(jnp.exp(-jnp.abs(x)))


def _shift_rows(x, prev_row, next_row):
    n = x.shape[0]
    rid = lax.broadcasted_iota(jnp.int32, (8, x.shape[1]), 0)
    dn = pltpu.roll(x, 1, axis=0)
    up = pltpu.roll(x, n - 1, axis=0)
    xm = jnp.concatenate([jnp.where(rid == 0, prev_row, dn[0:8]), dn[8:]], axis=0)
    xp = jnp.concatenate([up[:n - 8], jnp.where(rid == 7, next_row, up[n - 8:])], axis=0)
    return xm, xp


def _silu(x):
    return x * (0.5 * jnp.tanh(0.5 * x) + 0.5)


def _in_proj_kernel(x_ref, g_ref, w_ref, hy_ref, ssd_ref, gla_ref, att_ref):
    x = x_ref[...]
    ms = jnp.mean(x * x, axis=-1, keepdims=True)
    h = (x * lax.rsqrt(ms + EPS) * g_ref[...]).astype(BF16)
    c0 = 0
    for ref in (hy_ref, ssd_ref, gla_ref, att_ref):
        n = ref.shape[-1]
        ref[...] = jnp.dot(h, w_ref[:, c0:c0 + n], preferred_element_type=F32)
        c0 += n


def in_proj(x2d, g, w_p, tm=256):
    t = x2d.shape[0]
    widths = (HY_IN, SSD_PAD, GLA_PAD, ATT_PAD)
    return pl.pallas_call(
        _in_proj_kernel,
        grid=(t // tm,),
        in_specs=[pl.BlockSpec((tm, D_MODEL), lambda i: (i, 0)),
                  pl.BlockSpec((1, D_MODEL), lambda i: (0, 0)),
                  pl.BlockSpec((D_MODEL, IN_PAD), lambda i: (0, 0), pipeline_mode=pl.Buffered(1))],
        out_specs=[pl.BlockSpec((tm, n), lambda i: (i, 0)) for n in widths],
        out_shape=[jax.ShapeDtypeStruct((t, n), F32) for n in widths],
        compiler_params=_cparams(1),
        name="in_proj",
    )(x2d, g.reshape(1, D_MODEL), w_p)


def _rope(x, cos_t, sin_t):
    lane = lax.broadcasted_iota(jnp.int32, x.shape, 1)
    nxt = pltpu.roll(x, LANES - 32, axis=1)
    prv = pltpu.roll(x, 32, axis=1)
    sw = jnp.where((lane % 64) < 32, nxt, prv)
    return x * cos_t + sw * sin_t


def _att_prep_kernel(u_ref, cos_ref, sin_ref, gq_ref, gk_ref, q_ref, k_ref, vt_ref):
    cos_t = cos_ref[...]
    sin_t = sin_ref[...]
    inv = 1.0 / ATT_HEAD_DIM
    qscale = (ATT_HEAD_DIM ** -0.5) * math.log2(math.e)
    for h in range(ATT_HEADS):
        x = u_ref[:, h * LANES:(h + 1) * LANES]
        ms = jnp.sum(x * x, axis=-1, keepdims=True) * inv
        xn = x * lax.rsqrt(ms + EPS) * gq_ref[...]
        q_ref[:, h * LANES:(h + 1) * LANES] = (_rope(xn, cos_t, sin_t) * qscale).astype(BF16)
    k = u_ref[:, ATT_Q_PAD:ATT_Q_PAD + LANES]
    lane = lax.broadcasted_iota(jnp.int32, k.shape, 1)
    k2 = k * k
    s_all = jnp.sum(k2, axis=-1, keepdims=True)
    s_lo = jnp.sum(jnp.where(lane < 64, k2, 0.0), axis=-1, keepdims=True)
    ms = jnp.where(lane < 64, s_lo, s_all - s_lo) * inv
    kn = k * lax.rsqrt(ms + EPS) * gk_ref[...]
    k_ref[...] = _rope(kn, cos_t, sin_t).astype(BF16)
    vt = u_ref[:, ATT_Q_PAD + LANES:ATT_PAD].T
    rid = lax.broadcasted_iota(jnp.int32, (ATT_VT_ROWS - ATT_HEAD_DIM, vt.shape[1]), 0)
    tail = jnp.where(rid == 0, 1.0, 0.0)
    for g in range(ATT_KV_HEADS):
        vt_ref[0, g] = jnp.concatenate(
            [vt[g * ATT_HEAD_DIM:(g + 1) * ATT_HEAD_DIM], tail], axis=0).astype(BF16)


def att_prep(u_att, cos_t, sin_t, gq, gk, batch, seq_len, tm=512):
    t = u_att.shape[0]
    nl = seq_len // tm
    return pl.pallas_call(
        _att_prep_kernel,
        grid=(t // tm,),
        in_specs=[pl.BlockSpec((tm, ATT_PAD), lambda i: (i, 0)),
                  pl.BlockSpec((tm, LANES), lambda i: (i % nl, 0)),
                  pl.BlockSpec((tm, LANES), lambda i: (i % nl, 0)),
                  pl.BlockSpec((1, LANES), lambda i: (0, 0)),
                  pl.BlockSpec((1, LANES), lambda i: (0, 0))],
        out_specs=[pl.BlockSpec((tm, ATT_Q_PAD), lambda i: (i, 0)),
                   pl.BlockSpec((tm, LANES), lambda i: (i, 0)),
                   pl.BlockSpec((1, ATT_KV_HEADS, ATT_VT_ROWS, tm), lambda i: (i // nl, 0, 0, i % nl))],
        out_shape=[jax.ShapeDtypeStruct((t, ATT_Q_PAD), BF16),
                   jax.ShapeDtypeStruct((t, LANES), BF16),
                   jax.ShapeDtypeStruct((batch, ATT_KV_HEADS, ATT_VT_ROWS, seq_len), BF16)],
        compiler_params=_cparams(1),
        name="att_prep",
    )(u_att, cos_t, sin_t, gq, gk)


def _flash_kernel(q_ref, k_ref, vt_ref, g_ref, o_ref, m_ref, acc_ref, st_ref, p_ref, *, tk, nk):
    m_ref[...] = jnp.full(m_ref.shape, -jnp.inf, F32)
    acc_ref[...] = jnp.zeros(acc_ref.shape, F32)

    def scores(j, slot, h):
        off = pl.multiple_of(j * tk, tk)
        k = k_ref[0, pl.ds(off, tk), :]
        q = q_ref[0, :, h * LANES:(h + 1) * LANES]
        st_ref[slot, h] = lax.dot_general(k, q, NT, preferred_element_type=F32)

    def consume(j, slot, h):
        off = pl.multiple_of(j * tk, tk)
        rows = [slice(r0, r0 + ATT_ROW_CHUNK) for r0 in range(0, tk, ATT_ROW_CHUNK)]
        m_prev = m_ref[h]
        m_new = m_prev
        for r in rows:
            m_new = jnp.maximum(m_new, jnp.max(st_ref[slot, h, r], axis=0, keepdims=True))
        alpha = jnp.exp2(m_prev - m_new)
        for r in rows:
            p_ref[h, r] = jnp.exp2((st_ref[slot, h, r] - m_new).astype(BF16))
        vt = vt_ref[0, h // 2, :, pl.ds(off, tk)]
        acc_ref[h] = alpha * acc_ref[h] + jnp.dot(vt, p_ref[h], preferred_element_type=F32)
        m_ref[h] = m_new

    for h in range(ATT_HEADS):
        scores(0, 0, h)

    def body(jj, carry):
        j = 2 * jj
        j2 = jnp.minimum(j + 2, nk - 1)
        for h in range(ATT_HEADS):
            scores(j + 1, 1, h)
            consume(j, 0, h)
        for h in range(ATT_HEADS):
            scores(j2, 0, h)
            consume(j + 1, 1, h)
        return carry

    lax.fori_loop(0, nk // 2, body, 0)
    ot = jnp.concatenate([acc_ref[h, 0:ATT_HEAD_DIM] / acc_ref[h, ATT_HEAD_DIM:ATT_HEAD_DIM + 1]
                          for h in range(ATT_HEADS)], axis=0)
    o = ot.T
    ms = jnp.mean(o * o, axis=-1, keepdims=True)
    o_ref[0] = (o * lax.rsqrt(ms + EPS) * g_ref[...]).astype(BF16)


def flash_attention(q, k, vt, g_out, tq=512, tk=512):
    b, seq_len, _ = q.shape
    return pl.pallas_call(
        functools.partial(_flash_kernel, tk=tk, nk=seq_len // tk),
        grid=(b, seq_len // tq),
        in_specs=[pl.BlockSpec((1, tq, ATT_Q_PAD), lambda bi, i: (bi, i, 0)),
                  pl.BlockSpec((1, seq_len, LANES), lambda bi, i: (bi, 0, 0)),
                  pl.BlockSpec((1, ATT_KV_HEADS, ATT_VT_ROWS, seq_len), lambda bi, i: (bi, 0, 0, 0)),
                  pl.BlockSpec((1, GROUP_W), lambda bi, i: (0, 0))],
        out_specs=pl.BlockSpec((1, tq, GROUP_W), lambda bi, i: (bi, i, 0)),
        out_shape=jax.ShapeDtypeStruct((b, seq_len, GROUP_W), BF16),
        scratch_shapes=[pltpu.VMEM((ATT_HEADS, 1, tq), F32),
                        pltpu.VMEM((ATT_HEADS, ATT_VT_ROWS, tq), F32),
                        pltpu.VMEM((2, ATT_HEADS, tk, tq), F32),
                        pltpu.VMEM((ATT_HEADS, tk, tq), BF16)],
        compiler_params=_cparams(2),
        name="flash_attention",
    )(q, k, vt, g_out.reshape(1, GROUP_W))


def _tri(n, upper, block=None):
    ii = lax.broadcasted_iota(jnp.int32, (n, n), 0)
    jj = lax.broadcasted_iota(jnp.int32, (n, n), 1)
    m = (jj >= ii) if upper else (jj <= ii)
    if block is not None:
        m = m & ((ii // block) == (jj // block))
    return m


def _tri_dot(t, x):
    hi = x.astype(BF16)
    r1 = x - hi.astype(F32)
    mid = r1.astype(BF16)
    lo = (r1 - mid.astype(F32)).astype(BF16)
    n = x.shape[1]
    y = jnp.dot(t, jnp.concatenate([hi, mid, lo], axis=1), preferred_element_type=F32)
    return (y[:, 0:n] + y[:, n:2 * n]) + y[:, 2 * n:]


def _ssd_fwd_kernel(u_ref, up_ref, un_ref, cw_ref, cb_ref, dtb_ref, ax_ref, dx_ref,
                    yp_ref, cm_ref, bm_ref, wxb_ref, eb_ref, s_ref, *, tm):
    Q = SSD_CHUNK
    i = pl.program_id(1)
    nb = pl.num_programs(1)

    @pl.when(i == 0)
    def _():
        s_ref[...] = jnp.zeros(s_ref.shape, F32)

    x = u_ref[:, SSD_INNER:SSD_INNER + SSD_XBC]
    prev_row = jnp.where(i > 0, up_ref[7:8, SSD_INNER:SSD_INNER + SSD_XBC], 0.0)
    next_row = jnp.where(i < nb - 1, un_ref[0:1, SSD_INNER:SSD_INNER + SSD_XBC], 0.0)
    xm, xp = _shift_rows(x, prev_row, next_row)
    conv = xm * cw_ref[0:1] + x * cw_ref[1:2] + xp * cw_ref[2:3] + cb_ref[...]
    xbc = _silu(conv)
    xs = xbc[:, 0:SSD_INNER]
    bm = xbc[:, SSD_INNER:SSD_INNER + SSD_BC].astype(BF16)
    cm = xbc[:, SSD_INNER + SSD_BC:].astype(BF16)
    bm_ref[...] = bm
    cm_ref[...] = cm
    dt = _softplus(u_ref[:, SSD_INNER + SSD_XBC:] + dtb_ref[...])
    da = dt * ax_ref[...]

    tl = _tri(Q, False).astype(BF16)
    tu = _tri(Q, True).astype(BF16)
    ii = lax.broadcasted_iota(jnp.int32, (Q, Q), 0)
    jj = lax.broadcasted_iota(jnp.int32, (Q, Q), 1)
    lo_half = lax.broadcasted_iota(jnp.int32, (Q, LANES), 1) < SSD_HEAD_DIM
    ninf = jnp.float32(-jnp.inf)
    state = [s_ref[g] for g in range(SSD_GROUPS)]

    for c in range(tm // Q):
        r = slice(c * Q, (c + 1) * Q)
        cum = _tri_dot(tl, da[r, 0:SSD_INNER])
        rc = _tri_dot(tu, da[r, SSD_INNER:])
        for g in range(SSD_GROUPS):
            ln = slice(g * LANES, (g + 1) * LANES)
            cumg, rcg = cum[:, ln], rc[:, ln]
            dtf, dtbk = dt[r, ln], dt[r, SSD_INNER + g * LANES:SSD_INNER + (g + 1) * LANES]
            xsg = xs[r, ln]
            xsg_b = xsg.astype(BF16)
            bg, cg = bm[r, ln], cm[r, ln]
            cb = lax.dot_general(cg, bg, NT, preferred_element_type=F32)
            cum_t, rc_t, dtf_t, dtb_t = cumg.T, rcg.T, dtf.T, dtbk.T
            cum_sw, rc_sw = pltpu.roll(cumg, SSD_HEAD_DIM, axis=1), pltpu.roll(rcg, SSD_HEAD_DIM, axis=1)
            yd = []
            for hh in range(2):
                row = slice(hh * SSD_HEAD_DIM, hh * SSD_HEAD_DIM + 1)
                col_c = jnp.where(lo_half, cumg, cum_sw) if hh == 0 else jnp.where(lo_half, cum_sw, cumg)
                col_r = jnp.where(lo_half, rcg, rc_sw) if hh == 0 else jnp.where(lo_half, rc_sw, rcg)
                ef = jnp.exp(jnp.where(jj <= ii, col_c - cum_t[row], ninf)) * dtf_t[row]
                eb = jnp.exp(jnp.where(jj >= ii, col_r - rc_t[row], ninf)) * dtb_t[row]
                w = (cb * (ef + eb)).astype(BF16)
                yd.append(jnp.dot(w, xsg_b, preferred_element_type=F32))
            y = jnp.where(lo_half, yd[0], yd[1])
            last = cumg[Q - 1:Q]
            wxf = (xsg * jnp.exp(last - cumg) * dtf).astype(BF16)
            y = y + jnp.dot(cg, state[g].astype(BF16), preferred_element_type=F32) * jnp.exp(cumg)
            state[g] = state[g] * jnp.exp(last) + lax.dot_general(bg, wxf, TN, preferred_element_type=F32)
            yp_ref[r, ln] = y + dx_ref[:, ln] * xsg
            wxb_ref[r, ln] = (xsg * jnp.exp(rcg[0:1] - rcg) * dtbk).astype(BF16)
            eb_ref[r, ln] = jnp.exp(rcg)
    for g in range(SSD_GROUPS):
        s_ref[g] = state[g]


def _ssd_bwd_kernel(yp_ref, cm_ref, bm_ref, wxb_ref, eb_ref, z_ref, g_ref, o_ref, s_ref, y_ref, *, tm):
    Q = SSD_CHUNK

    @pl.when(pl.program_id(1) == 0)
    def _():
        s_ref[...] = jnp.zeros(s_ref.shape, F32)

    state = [s_ref[g] for g in range(SSD_GROUPS)]
    for c in reversed(range(tm // Q)):
        r = slice(c * Q, (c + 1) * Q)
        for g in range(SSD_GROUPS):
            ln = slice(g * LANES, (g + 1) * LANES)
            ebg = eb_ref[r, ln]
            y_ref[r, ln] = yp_ref[r, ln] + jnp.dot(cm_ref[r, ln], state[g].astype(BF16),
                                                   preferred_element_type=F32) * ebg
            state[g] = state[g] * ebg[0:1] + lax.dot_general(bm_ref[r, ln], wxb_ref[r, ln], TN,
                                                             preferred_element_type=F32)
    for g in range(SSD_GROUPS):
        s_ref[g] = state[g]
    y = y_ref[...] * _silu(z_ref[...])
    ms = jnp.mean(y * y, axis=-1, keepdims=True)
    o_ref[...] = (y * lax.rsqrt(ms + EPS) * g_ref[...]).astype(BF16)


def ssd_mixer(u_ssd, batch, seq_len, conv_w, conv_b, dt_bias_x, a_x, d_x, norm_g, tm=512, tm_bwd=1024):
    t = u_ssd.shape[0]
    nb = seq_len // tm
    hb = tm // 8
    last_halo = t // 8 - 1
    row = lambda b, i: (b * nb + i, 0)
    const = lambda b, i: (0, 0)
    outs = pl.pallas_call(
        functools.partial(_ssd_fwd_kernel, tm=tm),
        grid=(batch, nb),
        in_specs=[pl.BlockSpec((tm, SSD_PAD), row),
                  pl.BlockSpec((8, SSD_PAD), lambda b, i: (jnp.maximum((b * nb + i) * hb - 1, 0), 0)),
                  pl.BlockSpec((8, SSD_PAD), lambda b, i: (jnp.minimum((b * nb + i + 1) * hb, last_halo), 0)),
                  pl.BlockSpec((3, SSD_XBC), const),
                  pl.BlockSpec((1, SSD_XBC), const),
                  pl.BlockSpec((1, SSD_DTX), const),
                  pl.BlockSpec((1, SSD_DTX), const),
                  pl.BlockSpec((1, SSD_INNER), const)],
        out_specs=[pl.BlockSpec((tm, SSD_INNER), row)] * 5,
        out_shape=[jax.ShapeDtypeStruct((t, SSD_INNER), F32),
                   jax.ShapeDtypeStruct((t, SSD_INNER), BF16),
                   jax.ShapeDtypeStruct((t, SSD_INNER), BF16),
                   jax.ShapeDtypeStruct((t, SSD_INNER), BF16),
                   jax.ShapeDtypeStruct((t, SSD_INNER), F32)],
        scratch_shapes=[pltpu.VMEM((SSD_GROUPS, SSD_STATE, LANES), F32)],
        compiler_params=_cparams(2),
        name="ssd_fwd",
    )(u_ssd, u_ssd, u_ssd, conv_w, conv_b.reshape(1, SSD_XBC), dt_bias_x, a_x, d_x)
    nb = seq_len // tm_bwd
    rrow = lambda b, i: (b * nb + nb - 1 - i, 0)
    return pl.pallas_call(
        functools.partial(_ssd_bwd_kernel, tm=tm_bwd),
        grid=(batch, nb),
        in_specs=[pl.BlockSpec((tm_bwd, SSD_INNER), rrow)] * 6 + [pl.BlockSpec((1, SSD_INNER), const)],
        out_specs=pl.BlockSpec((tm_bwd, SSD_INNER), rrow),
        out_shape=jax.ShapeDtypeStruct((t, SSD_INNER), BF16),
        scratch_shapes=[pltpu.VMEM((SSD_GROUPS, SSD_STATE, LANES), F32),
                        pltpu.VMEM((tm_bwd, SSD_INNER), F32)],
        compiler_params=_cparams(2),
        name="ssd_bwd",
    )(*outs, u_ssd, norm_g.reshape(1, SSD_INNER))


def _gla_fwd_kernel(u_ref, wg_ref, bg_ref, op_ref, qb_ref, kb_ref, db_ref, st_ref, *, tm):
    Q = GLA_CHUNK
    P = 2 * Q

    @pl.when(pl.program_id(1) == 0)
    def _():
        st_ref[...] = jnp.zeros(st_ref.shape, F32)

    pre = jnp.dot(u_ref[:, GLA_LR0:GLA_PAD], wg_ref[...], precision=HIGHEST,
                  preferred_element_type=F32) + bg_ref[...]
    gk = (jnp.minimum(pre, 0.0) - jnp.log1p(jnp.exp(-jnp.abs(pre)))) * (1.0 / GLA_NORMALIZER)

    mask_f = _tri(P, False, Q)
    mask_b = _tri(P, True, Q)
    tl = mask_f.astype(BF16)
    tu = mask_b.astype(BF16)
    lane_k = lax.broadcasted_iota(jnp.int32, (P, GLA_QK), 1) // GLA_DK
    lane_v = lax.broadcasted_iota(jnp.int32, (P, GLA_V), 1) // GLA_DV
    bd_t = (lax.broadcasted_iota(jnp.int32, (GLA_V, GLA_QK), 0) // GLA_DV ==
            lax.broadcasted_iota(jnp.int32, (GLA_V, GLA_QK), 1) // GLA_DK)

    st = st_ref[...]
    for c in range(tm // P):
        r = slice(c * P, (c + 1) * P)
        gf = _tri_dot(tl, gk[r, 0:GLA_QK])
        gb = _tri_dot(tu, gk[r, GLA_QK:])
        qs = u_ref[r, 0:GLA_QK] * (GLA_DK ** -0.5)
        k = u_ref[r, GLA_QK:2 * GLA_QK]
        v = u_ref[r, 2 * GLA_QK:2 * GLA_QK + GLA_V]
        vb = v.astype(BF16)
        q_f = qs * jnp.exp(gf)
        k_f = (k * jnp.exp(-gf)).astype(BF16)
        q_b = qs * jnp.exp(gb)
        k_b = (k * jnp.exp(-gb)).astype(BF16)
        heads = range(GLA_HEADS)
        qf_h = jnp.concatenate([jnp.where(lane_k == h, q_f, 0.0) for h in heads], axis=0).astype(BF16)
        qb_h = jnp.concatenate([jnp.where(lane_k == h, q_b, 0.0) for h in heads], axis=0).astype(BF16)
        af = lax.dot_general(qf_h, k_f, NT, preferred_element_type=F32)
        ab = lax.dot_general(qb_h, k_b, NT, preferred_element_type=F32)
        att = jnp.concatenate([jnp.where(mask_f, af[h * P:(h + 1) * P], 0.0) +
                               jnp.where(mask_b, ab[h * P:(h + 1) * P], 0.0) for h in heads], axis=1)
        v_h = jnp.concatenate([jnp.where(lane_v == h, v, 0.0) for h in heads], axis=0).astype(BF16)
        o = jnp.dot(att.astype(BF16), v_h, preferred_element_type=F32)
        q_fb = q_f.astype(BF16)
        for cc in range(2):
            rr = slice(cc * Q, (cc + 1) * Q)
            ro = slice(c * P + cc * Q, c * P + (cc + 1) * Q)
            g_last = gf[cc * Q + Q - 1:cc * Q + Q]
            k_end = (k[rr] * jnp.exp(g_last - gf[rr])).astype(BF16)
            op_ref[ro] = o[rr] + lax.dot_general(q_fb[rr], st.astype(BF16), NT, preferred_element_type=F32)
            upd = lax.dot_general(vb[rr], k_end, TN, preferred_element_type=F32)
            st = st * jnp.exp(g_last) + jnp.where(bd_t, upd, 0.0)
            g_first = gb[cc * Q:cc * Q + 1]
            kb_ref[ro] = (k[rr] * jnp.exp(g_first - gb[rr])).astype(BF16)
        qb_ref[r] = q_b.astype(BF16)
        db_ref[r] = jnp.exp(gb)
    st_ref[...] = st


def _gla_bwd_kernel(op_ref, qb_ref, kb_ref, db_ref, v_ref, go_ref, g_ref, o_ref, st_ref, y_ref, *, tm):
    Q = GLA_CHUNK

    @pl.when(pl.program_id(1) == 0)
    def _():
        st_ref[...] = jnp.zeros(st_ref.shape, F32)

    bd_t = (lax.broadcasted_iota(jnp.int32, (GLA_V, GLA_QK), 0) // GLA_DV ==
            lax.broadcasted_iota(jnp.int32, (GLA_V, GLA_QK), 1) // GLA_DK)
    st = st_ref[...]
    for c in reversed(range(tm // Q)):
        r = slice(c * Q, (c + 1) * Q)
        y_ref[r] = op_ref[r] + lax.dot_general(qb_ref[r], st.astype(BF16), NT, preferred_element_type=F32)
        upd = lax.dot_general(v_ref[r].astype(BF16), kb_ref[r], TN, preferred_element_type=F32)
        st = st * db_ref[c * Q:c * Q + 1] + jnp.where(bd_t, upd, 0.0)
    st_ref[...] = st
    y = y_ref[...]
    seg = (lax.broadcasted_iota(jnp.int32, (GLA_V, GLA_V), 0) // GLA_DV ==
           lax.broadcasted_iota(jnp.int32, (GLA_V, GLA_V), 1) // GLA_DV).astype(F32)
    ms = jnp.dot(y * y, seg, precision=HIGHEST, preferred_element_type=F32) * (1.0 / GLA_DV)
    o_ref[...] = (y * lax.rsqrt(ms + EPS) * g_ref[...] * _silu(go_ref[...])).astype(BF16)


def gla_mixer(u_gla, batch, seq_len, wg, bg, norm_g_x, tm=512, tm_bwd=1024):
    t = u_gla.shape[0]
    nb = seq_len // tm
    row = lambda b, i: (b * nb + i, 0)
    const = lambda b, i: (0, 0)
    outs = pl.pallas_call(
        functools.partial(_gla_fwd_kernel, tm=tm),
        grid=(batch, nb),
        in_specs=[pl.BlockSpec((tm, GLA_PAD), row),
                  pl.BlockSpec((GLA_PAD - GLA_LR0, 2 * GLA_QK), const),
                  pl.BlockSpec((1, 2 * GLA_QK), const)],
        out_specs=[pl.BlockSpec((tm, GLA_V), row), pl.BlockSpec((tm, GLA_QK), row),
                   pl.BlockSpec((tm, GLA_QK), row), pl.BlockSpec((tm, GLA_QK), row)],
        out_shape=[jax.ShapeDtypeStruct((t, GLA_V), F32),
                   jax.ShapeDtypeStruct((t, GLA_QK), BF16),
                   jax.ShapeDtypeStruct((t, GLA_QK), BF16),
                   jax.ShapeDtypeStruct((t, GLA_QK), F32)],
        scratch_shapes=[pltpu.VMEM((GLA_V, GLA_QK), F32)],
        compiler_params=_cparams(2),
        name="gla_fwd",
    )(u_gla, wg, bg)
    nb = seq_len // tm_bwd
    rrow = lambda b, i: (b * nb + nb - 1 - i, 0)
    return pl.pallas_call(
        functools.partial(_gla_bwd_kernel, tm=tm_bwd),
        grid=(batch, nb),
        in_specs=[pl.BlockSpec((tm_bwd, GLA_V), rrow), pl.BlockSpec((tm_bwd, GLA_QK), rrow),
                  pl.BlockSpec((tm_bwd, GLA_QK), rrow), pl.BlockSpec((tm_bwd, GLA_QK), rrow),
                  pl.BlockSpec((tm_bwd, GLA_V), lambda b, i: (b * nb + nb - 1 - i, 1)),
                  pl.BlockSpec((tm_bwd, GLA_V), lambda b, i: (b * nb + nb - 1 - i, 2)),
                  pl.BlockSpec((1, GLA_V), const)],
        out_specs=pl.BlockSpec((tm_bwd, GLA_V), rrow),
        out_shape=jax.ShapeDtypeStruct((t, GLA_V), BF16),
        scratch_shapes=[pltpu.VMEM((GLA_V, GLA_QK), F32), pltpu.VMEM((tm_bwd, GLA_V), F32)],
        compiler_params=_cparams(2),
        name="gla_bwd",
    )(*outs, u_gla, u_gla, norm_g_x)


def _out_ffn_kernel(x_ref, y0_ref, y1_ref, y2_ref, y3_ref, wo_ref, g2_ref, wu_ref, wd_ref, o_ref, *, tf):
    mixed = jnp.concatenate([y0_ref[...], y1_ref[...], y2_ref[...], y3_ref[...]], axis=1)
    x1 = x_ref[...] + jnp.dot(mixed, wo_ref[...], preferred_element_type=F32)
    ms = jnp.mean(x1 * x1, axis=-1, keepdims=True)
    h2 = (x1 * lax.rsqrt(ms + EPS) * g2_ref[...]).astype(BF16)
    acc = x1
    for c in range(D_FF // tf):
        up = jnp.dot(h2, wu_ref[:, c * tf:(c + 1) * tf], preferred_element_type=F32)
        a = jnp.square(jnp.maximum(up, 0.0)).astype(BF16)
        acc = acc + jnp.dot(a, wd_ref[c * tf:(c + 1) * tf, :], preferred_element_type=F32)
    o_ref[...] = acc


def out_ffn(x2d, ys, wo, g2, wu, wd, tm=512, tf=1024):
    t = x2d.shape[0]
    const = lambda i: (0, 0)
    single = pl.Buffered(1)
    return pl.pallas_call(
        functools.partial(_out_ffn_kernel, tf=tf),
        grid=(t // tm,),
        in_specs=[pl.BlockSpec((tm, D_MODEL), lambda i: (i, 0))] +
                 [pl.BlockSpec((tm, GROUP_W), lambda i: (i, 0))] * 4 +
                 [pl.BlockSpec((D_MODEL, D_MODEL), const, pipeline_mode=single),
                  pl.BlockSpec((1, D_MODEL), const),
                  pl.BlockSpec((D_MODEL, D_FF), const, pipeline_mode=single),
                  pl.BlockSpec((D_FF, D_MODEL), const, pipeline_mode=single)],
        out_specs=pl.BlockSpec((tm, D_MODEL), lambda i: (i, 0)),
        out_shape=jax.ShapeDtypeStruct((t, D_MODEL), F32),
        compiler_params=_cparams(1),
        name="out_ffn",
    )(x2d, *ys, wo, g2.reshape(1, D_MODEL), wu, wd)


HY_N1 = 128
HY_ROWS = HY_N1 // 2


def _hy_pre_kernel(u_ref, up_ref, un_ref, cw_ref, cb_ref, o_ref, *, tm):
    i = pl.program_id(1)
    nb = pl.num_programs(1)
    x = u_ref[...]
    prev_row = jnp.where(i > 0, up_ref[7:8, :], 0.0)
    next_row = jnp.where(i < nb - 1, un_ref[0:1, :], 0.0)
    xm, xp = _shift_rows(x, prev_row, next_row)
    uc = xm * cw_ref[0:1] + x * cw_ref[1:2] + xp * cw_ref[2:3] + cb_ref[...]
    o_ref[0] = uc.T


def hy_pre(u_hy, batch, seq_len, conv_w, conv_b, tm=512):
    t = u_hy.shape[0]
    nb = seq_len // tm
    hb = tm // 8
    last_halo = t // 8 - 1
    const = lambda b, i: (0, 0)
    return pl.pallas_call(
        functools.partial(_hy_pre_kernel, tm=tm),
        grid=(batch, nb),
        in_specs=[pl.BlockSpec((tm, HY_IN), lambda b, i: (b * nb + i, 0)),
                  pl.BlockSpec((8, HY_IN), lambda b, i: (jnp.maximum((b * nb + i) * hb - 1, 0), 0)),
                  pl.BlockSpec((8, HY_IN), lambda b, i: (jnp.minimum((b * nb + i + 1) * hb, last_halo), 0)),
                  pl.BlockSpec((3, HY_IN), const),
                  pl.BlockSpec((1, HY_IN), const)],
        out_specs=pl.BlockSpec((1, HY_IN, tm), lambda b, i: (b, 0, i)),
        out_shape=jax.ShapeDtypeStruct((batch, HY_IN, seq_len), F32),
        compiler_params=_cparams(2),
        name="hy_pre",
    )(u_hy, u_hy, u_hy, conv_w, conv_b.reshape(1, HY_IN))


def _hy_conv_kernel(z_ref, g_ref, sk_ref, h_ref, f1_ref, w2_ref, t_ref, gi_ref, o_ref, *, cb, n2):
    ct = z_ref.shape[1]
    f1 = f1_ref[...]
    w2 = w2_ref[...]
    tre, tim = t_ref[0:HY_N1], t_ref[HY_N1:]
    gi = gi_ref[...]

    def quads(blk):
        return blk[0:HY_N1, 0:n2], blk[0:HY_N1, n2:], blk[HY_N1:, 0:n2], blk[HY_N1:, n2:]

    def body(it, carry):
        c0 = it * cb
        planes = []
        for u in range(cb):
            a = jnp.dot(f1, z_ref[0, c0 + u].astype(BF16), preferred_element_type=F32)
            are, aim = a[0:HY_N1], a[HY_N1:]
            planes += [(are * tre - aim * tim).astype(BF16), (are * tim + aim * tre).astype(BF16)]
        spec = jnp.dot(jnp.concatenate(planes, axis=0), w2, preferred_element_type=F32)
        planes = []
        for u in range(cb):
            tl, tr, bl, br = quads(spec[u * 2 * HY_N1:(u + 1) * 2 * HY_N1])
            bre, bim = tl - br, tr + bl
            hre, him = h_ref[0, c0 + u, 0:HY_N1], h_ref[0, c0 + u, HY_N1:]
            planes += [(bre * hre - bim * him).astype(BF16), (bre * him + bim * hre).astype(BF16)]
        back = jnp.dot(jnp.concatenate(planes, axis=0), w2, preferred_element_type=F32)
        for u in range(cb):
            tl, tr, bl, br = quads(back[u * 2 * HY_N1:(u + 1) * 2 * HY_N1])
            are, aim = tl + br, bl - tr
            st = jnp.concatenate([are * tre + aim * tim, aim * tre - are * tim], axis=0).astype(BF16)
            y = jnp.dot(gi, st, preferred_element_type=F32)
            o_ref[0, c0 + u] = g_ref[0, c0 + u] * (y + z_ref[0, c0 + u] * sk_ref[c0 + u])
        return carry

    lax.fori_loop(0, ct // cb, body, 0)


def hy_conv(z4, z_off, gate4, gate_off, skip, h_spec, order, consts, ct=32, cb=None):
    b, _, _, n2 = z4.shape
    if cb is None:
        cb = min(ct, 2048 // n2)
    f1, w2, tw, gi = consts
    nct = HY_C // ct
    seq = lambda off: (lambda bi, j: (bi, off // ct + j, 0, 0))
    const = lambda bi, j: (0, 0)
    return pl.pallas_call(
        functools.partial(_hy_conv_kernel, cb=cb, n2=n2),
        grid=(b, nct),
        in_specs=[pl.BlockSpec((1, ct, HY_ROWS, n2), seq(z_off)),
                  pl.BlockSpec((1, ct, HY_ROWS, n2), seq(gate_off)),
                  pl.BlockSpec((ct, 1, 1), lambda bi, j: (j, 0, 0)),
                  pl.BlockSpec((1, ct, 2 * HY_N1, n2), lambda bi, j: (order, j, 0, 0)),
                  pl.BlockSpec((2 * HY_N1, HY_ROWS), const),
                  pl.BlockSpec((n2, 2 * n2), const),
                  pl.BlockSpec((2 * HY_N1, n2), const),
                  pl.BlockSpec((HY_ROWS, 2 * HY_N1), const)],
        out_specs=pl.BlockSpec((1, ct, HY_ROWS, n2), lambda bi, j: (bi, j, 0, 0)),
        out_shape=jax.ShapeDtypeStruct((b, HY_C, HY_ROWS, n2), F32),
        compiler_params=_cparams(2),
        name="hy_conv",
    )(z4, gate4, skip.reshape(HY_C, 1, 1), h_spec, f1, w2, tw, gi)


def _hy_post_kernel(z_ref, g_ref, o_ref):
    x = z_ref[0]
    ms = jnp.mean(x * x, axis=0, keepdims=True)
    o_ref[...] = (x * lax.rsqrt(ms + EPS) * g_ref[...]).T.astype(BF16)


def hy_post(z3, g, tm=512):
    b, _, seq_len = z3.shape
    nb = seq_len // tm
    return pl.pallas_call(
        _hy_post_kernel,
        grid=(b, nb),
        in_specs=[pl.BlockSpec((1, HY_C, tm), lambda bi, i: (bi, 0, i)),
                  pl.BlockSpec((HY_C, 1), lambda bi, i: (0, 0))],
        out_specs=pl.BlockSpec((tm, HY_C), lambda bi, i: (bi * nb + i, 0)),
        out_shape=jax.ShapeDtypeStruct((b * seq_len, HY_C), BF16),
        compiler_params=_cparams(2),
        name="hy_post",
    )(z3, g.reshape(HY_C, 1))


def _dft_consts(n2):
    n = HY_N1 * n2
    k1 = np.arange(HY_N1, dtype=np.float64)
    a1 = 2.0 * np.pi * np.outer(k1, np.arange(HY_ROWS)) / HY_N1
    f1 = np.concatenate([np.cos(a1), -np.sin(a1)], axis=0)
    m2 = np.arange(n2, dtype=np.float64)
    a2 = 2.0 * np.pi * np.outer(m2, m2) / n2
    w2 = np.concatenate([np.cos(a2), -np.sin(a2)], axis=1)
    at = 2.0 * np.pi * np.outer(k1, m2) / n
    tw = np.concatenate([np.cos(at), -np.sin(at)], axis=0)
    return (jnp.asarray(f1, BF16), jnp.asarray(w2, BF16), jnp.asarray(tw, F32), jnp.asarray(f1.T, BF16))


def _hy_taps_kernel(z_ref, w1_ref, b1_ref, w2_ref, b2_ref, fr_ref, w3_ref, dl_ref, o_ref, hid_ref):
    @pl.when((pl.program_id(0) == 0) & (pl.program_id(1) == 0))
    def _():
        fr = fr_ref[...]
        hid = jnp.sin(fr * (jnp.dot(w1_ref[...], z_ref[...], precision=HIGHEST,
                                    preferred_element_type=F32) + b1_ref[...]))
        hid_ref[...] = jnp.sin(fr * (jnp.dot(w2_ref[...], hid, precision=HIGHEST,
                                             preferred_element_type=F32) + b2_ref[...]))

    hid = hid_ref[...]
    dec = jnp.exp(-z_ref[0:1, :] * jnp.abs(dl_ref[...]))
    h = [jnp.dot(w3_ref[0, d], hid, precision=HIGHEST, preferred_element_type=F32) * dec for d in range(2)]
    norm = (jnp.sum(jnp.abs(h[0]), axis=1, keepdims=True) + jnp.sum(jnp.abs(h[1]), axis=1, keepdims=True))
    for d in range(2):
        o_ref[0, d] = h[d] / norm


def hy_taps(L, w1, b1, w2, b2, w3, freq, rb=32):
    t = jnp.linspace(0.0, 1.0, L, dtype=F32)[:, None]
    w = 2.0 * math.pi * jnp.arange(L, dtype=F32) / L
    bands = jnp.linspace(1e-4, HY_BANDS - 1, HY_BANDS, dtype=F32)
    ang = w[:, None] * bands
    zt = jnp.concatenate([t, jnp.cos(ang), -jnp.sin(ang)], axis=-1).T
    deltas = jnp.linspace(HY_MIN_DECAY, HY_MAX_DECAY, HY_C, dtype=F32).reshape(HY_C, 1)
    hid_n = w2.shape[0]
    emb = zt.shape[0]
    col = lambda v: v.reshape(hid_n, 1)
    w3t = w3.T.reshape(HY_ORDER, 2, HY_C, hid_n)
    const = lambda o, j: (0, 0)
    return pl.pallas_call(
        _hy_taps_kernel,
        grid=(HY_ORDER, HY_C // rb),
        in_specs=[pl.BlockSpec((emb, L), const),
                  pl.BlockSpec((hid_n, emb), const), pl.BlockSpec((hid_n, 1), const),
                  pl.BlockSpec((hid_n, hid_n), const), pl.BlockSpec((hid_n, 1), const),
                  pl.BlockSpec((hid_n, 1), const),
                  pl.BlockSpec((1, 2, rb, hid_n), lambda o, j: (o, 0, j, 0)),
                  pl.BlockSpec((rb, 1), lambda o, j: (j, 0))],
        out_specs=pl.BlockSpec((1, 2, rb, L), lambda o, j: (o, 0, j, 0)),
        out_shape=jax.ShapeDtypeStruct((HY_ORDER, 2, HY_C, L), F32),
        scratch_shapes=[pltpu.VMEM((hid_n, L), F32)],
        compiler_params=_cparams(2),
        name="hy_taps",
    )(zt, w1.T, col(b1), w2.T, col(b2), col(freq), w3t, deltas)


def _hy_spec_kernel(h_ref, f1_ref, w2_ref, t_ref, o_ref, *, cb, n2):
    ct = h_ref.shape[2]
    f1 = f1_ref[...]
    w2 = w2_ref[...]
    tre, tim = t_ref[0:HY_N1], t_ref[HY_N1:]
    inv_n = 1.0 / (HY_N1 * n2)

    def body(it, carry):
        c0 = it * cb
        planes = []
        for u in range(cb):
            for d in range(2):
                a = jnp.dot(f1, h_ref[0, d, c0 + u].astype(BF16), preferred_element_type=F32)
                are, aim = a[0:HY_N1], a[HY_N1:]
                planes += [(are * tre - aim * tim).astype(BF16), (are * tim + aim * tre).astype(BF16)]
        spec = jnp.dot(jnp.concatenate(planes, axis=0), w2, preferred_element_type=F32)
        for u in range(cb):
            re, im = [], []
            for d in range(2):
                blk = spec[(2 * u + d) * 2 * HY_N1:(2 * u + d + 1) * 2 * HY_N1]
                tl, tr, bl, br = blk[0:HY_N1, 0:n2], blk[0:HY_N1, n2:], blk[HY_N1:, 0:n2], blk[HY_N1:, n2:]
                re.append(tl - br)
                im.append(tr + bl)
            o_ref[0, c0 + u, 0:HY_N1] = (re[0] + re[1]) * inv_n
            o_ref[0, c0 + u, HY_N1:] = (im[0] - im[1]) * inv_n
        return carry

    lax.fori_loop(0, ct // cb, body, 0)


def hy_spec(taps, consts, ct=32, cb=8):
    L = taps.shape[-1]
    n2 = 2 * L // HY_N1
    f1, w2, tw, _ = consts
    const = lambda o, j: (0, 0)
    return pl.pallas_call(
        functools.partial(_hy_spec_kernel, cb=cb, n2=n2),
        grid=(HY_ORDER, HY_C // ct),
        in_specs=[pl.BlockSpec((1, 2, ct, HY_ROWS, n2), lambda o, j: (o, 0, j, 0, 0)),
                  pl.BlockSpec((2 * HY_N1, HY_ROWS), const),
                  pl.BlockSpec((n2, 2 * n2), const),
                  pl.BlockSpec((2 * HY_N1, n2), const)],
        out_specs=pl.BlockSpec((1, ct, 2 * HY_N1, n2), lambda o, j: (o, j, 0, 0)),
        out_shape=jax.ShapeDtypeStruct((HY_ORDER, HY_C, 2 * HY_N1, n2), F32),
        compiler_params=_cparams(2),
        name="hy_spec",
    )(taps.reshape(HY_ORDER, 2, HY_C, HY_ROWS, n2), f1, w2, tw)


def _hyena_filter_freq(L, w1, b1, w2, b2, w3, freq):
    return hy_spec(hy_taps(L, w1, b1, w2, b2, w3, freq), _dft_consts(2 * L // HY_N1))


def hyena_mixer(u_hy, batch, seq_len, p, h_spec):
    n2 = 2 * seq_len // HY_N1
    consts = _dft_consts(n2)
    uct = hy_pre(u_hy, batch, seq_len, p['hy_conv_w'], p['hy_conv_b'])
    uc4 = uct.reshape(batch, HY_IN, HY_ROWS, n2)
    z = hy_conv(uc4, 0, uc4, HY_C, p['hy_skip'][0], h_spec, 0, consts)
    z = hy_conv(z, 0, uc4, 2 * HY_C, p['hy_skip'][1], h_spec, 1, consts)
    return hy_post(z.reshape(batch, HY_C, seq_len), p['hy_out_g'])


_DEINTERLEAVE = np.concatenate([np.arange(0, ATT_HEAD_DIM, 2), np.arange(1, ATT_HEAD_DIM, 2)])


def _pack_w_in(w):
    c0 = 0
    w_hy = w[:, c0:c0 + HY_IN]; c0 += HY_IN
    w_ssd = w[:, c0:c0 + SSD_IN]; c0 += SSD_IN
    w_gla = w[:, c0:c0 + GLA_IN]; c0 += GLA_IN
    w_att = w[:, c0:c0 + ATT_IN]
    zeros = lambda n: jnp.zeros((D_MODEL, n), w.dtype)
    cols = [w_hy, w_ssd[:, :SSD_INNER + SSD_XBC],
            jnp.repeat(w_ssd[:, SSD_INNER + SSD_XBC:], SSD_HEAD_DIM, axis=1),
            w_gla, zeros(GLA_PAD - GLA_IN)]
    for h in range(ATT_HEADS):
        wh = w_att[:, h * ATT_HEAD_DIM:(h + 1) * ATT_HEAD_DIM][:, _DEINTERLEAVE]
        cols += [wh, zeros(64)] if h < 2 else [zeros(64), wh]
    kq = ATT_HEADS * ATT_HEAD_DIM
    for g in range(ATT_KV_HEADS):
        cols.append(w_att[:, kq + g * ATT_HEAD_DIM:kq + (g + 1) * ATT_HEAD_DIM][:, _DEINTERLEAVE])
    cols.append(w_att[:, kq + ATT_KV_HEADS * ATT_HEAD_DIM:])
    return jnp.concatenate(cols, axis=1).astype(BF16)


def _rope_tables(seq_len):
    rows = seq_len // GRID_W
    row = jnp.repeat(jnp.arange(rows, dtype=F32), GRID_W)
    col = jnp.tile(jnp.arange(GRID_W, dtype=F32), rows)
    inv_freq = 1.0 / (ROPE_THETA ** (jnp.arange(0, ROPE_AXIS_DIM, 2, dtype=F32) / ROPE_AXIS_DIM))
    ang = jnp.concatenate([row[:, None] * inv_freq, col[:, None] * inv_freq], axis=-1)
    cos, sin = jnp.cos(ang), jnp.sin(ang)
    return jnp.tile(cos, (1, 4)), jnp.tile(jnp.concatenate([-sin, sin], axis=-1), (1, 2))


def _layer_params(i, s):
    p = {name: val[i] for name, val in s.items()}
    p['w_in_p'] = _pack_w_in(p['w_in'])
    p['w_out_b'] = p['w_out'].astype(BF16)
    p['w_up_b'] = p['w_up'].astype(BF16)
    p['w_down_b'] = p['w_down'].astype(BF16)
    p['ssd_dt_bias_x'] = jnp.repeat(p['ssd_dt_bias'].reshape(-1), SSD_HEAD_DIM).reshape(1, SSD_DTX)
    p['ssd_a_x'] = jnp.repeat(-jnp.exp(p['ssd_A_log']).reshape(-1), SSD_HEAD_DIM).reshape(1, SSD_DTX)
    p['ssd_d_x'] = jnp.repeat(p['ssd_D'], SSD_HEAD_DIM).reshape(1, SSD_INNER)
    wg = jnp.zeros((GLA_PAD - GLA_LR0, 2 * GLA_QK), F32)
    wg = wg.at[0:GLA_RANK, 0:GLA_QK].set(p['gla_gk_w'][0])
    wg = wg.at[GLA_RANK:2 * GLA_RANK, GLA_QK:].set(p['gla_gk_w'][1])
    p['gla_wg'] = wg
    p['gla_bg'] = p['gla_gk_b'].reshape(1, 2 * GLA_QK)
    p['gla_norm_g_x'] = jnp.tile(p['gla_norm_g'], GLA_HEADS).reshape(1, GLA_V)
    p['att_gq'] = jnp.tile(p['att_q_norm_g'][_DEINTERLEAVE], 2).reshape(1, LANES)
    p['att_gk'] = jnp.tile(p['att_k_norm_g'][_DEINTERLEAVE], 2).reshape(1, LANES)
    return p


def _layer(x, tables, p):
    b, L, _ = x.shape
    x2d = x.reshape(b * L, D_MODEL)
    u_hy, u_ssd, u_gla, u_att = in_proj(x2d, p['ln1_g'], p['w_in_p'])

    hf = _hyena_filter_freq(L, p['hy_ffn_w1'], p['hy_ffn_b1'], p['hy_ffn_w2'], p['hy_ffn_b2'],
                            p['hy_ffn_w3'], p['hy_sin_freq'])
    y_hy = hyena_mixer(u_hy, b, L, p, hf)
    y_ssd = ssd_mixer(u_ssd, b, L, p['ssd_conv_w'], p['ssd_conv_b'], p['ssd_dt_bias_x'], p['ssd_a_x'],
                      p['ssd_d_x'], p['ssd_norm_g'])
    y_gla = gla_mixer(u_gla, b, L, p['gla_wg'], p['gla_bg'], p['gla_norm_g_x'])
    q, k, vt = att_prep(u_att, tables[0], tables[1], p['att_gq'], p['att_gk'], b, L)
    y_att = flash_attention(q.reshape(b, L, ATT_Q_PAD), k.reshape(b, L, LANES), vt, p['att_out_g'])

    y = out_ffn(x2d, (y_hy, y_ssd, y_gla, y_att.reshape(b * L, GROUP_W)), p['w_out_b'], p['ln2_g'],
                p['w_up_b'], p['w_down_b'])
    return y.reshape(b, L, D_MODEL)


def kernel(x_prompt, x_sample, ln1_g, w_in, hy_conv_w, hy_conv_b, hy_ffn_w1, hy_ffn_b1, hy_ffn_w2, hy_ffn_b2, hy_ffn_w3, hy_sin_freq, hy_skip, hy_out_g, ssd_conv_w, ssd_conv_b, ssd_A_log, ssd_dt_bias, ssd_D, ssd_norm_g, gla_gk_w, gla_gk_b, gla_norm_g, att_q_norm_g, att_k_norm_g, att_out_g, w_out, ln2_g, w_up, w_down):
    stacked = dict(ln1_g=ln1_g, w_in=w_in, hy_conv_w=hy_conv_w, hy_conv_b=hy_conv_b, hy_ffn_w1=hy_ffn_w1,
                   hy_ffn_b1=hy_ffn_b1, hy_ffn_w2=hy_ffn_w2, hy_ffn_b2=hy_ffn_b2, hy_ffn_w3=hy_ffn_w3,
                   hy_sin_freq=hy_sin_freq, hy_skip=hy_skip, hy_out_g=hy_out_g, ssd_conv_w=ssd_conv_w,
                   ssd_conv_b=ssd_conv_b, ssd_A_log=ssd_A_log, ssd_dt_bias=ssd_dt_bias, ssd_D=ssd_D,
                   ssd_norm_g=ssd_norm_g, gla_gk_w=gla_gk_w, gla_gk_b=gla_gk_b, gla_norm_g=gla_norm_g,
                   att_q_norm_g=att_q_norm_g, att_k_norm_g=att_k_norm_g, att_out_g=att_out_g,
                   w_out=w_out, ln2_g=ln2_g, w_up=w_up, w_down=w_down)
    tables_p = _rope_tables(x_prompt.shape[1])
    tables_s = _rope_tables(x_sample.shape[1])
    y_prompt, y_sample = x_prompt, x_sample
    for i in range(DEPTH):
        p = _layer_params(i, stacked)
        y_prompt = _layer(y_prompt, tables_p, p)
        y_sample = _layer(y_sample, tables_s, p)
    return (y_prompt, y_sample)
```

```python
import functools
import math

import jax
import jax.numpy as jnp
import numpy as np
from jax import lax
from jax.experimental import pallas as pl
from jax.experimental.pallas import tpu as pltpu

D_MODEL = 1024
DEPTH = 4
GRID_W = 64
EPS = 1e-6
GROUP_W = 256
D_FF = 4096

HY_C = 256
HY_ORDER = 2
HY_BANDS = 8
HY_MIN_DECAY = math.log(1e-2) / 1.5
HY_MAX_DECAY = math.log(1e-2) / 0.3
HY_IN = 768

SSD_HEAD_DIM = 64
SSD_HEADS = 4
SSD_GROUPS = 2
SSD_STATE = 128
SSD_CHUNK = 128
SSD_INNER = 256
SSD_BC = 256
SSD_XBC = 768
SSD_IN = 1032

GLA_HEADS = 4
GLA_DV = 64
GLA_DK = 32
GLA_RANK = 16
GLA_NORMALIZER = 16.0
GLA_CHUNK = 64
GLA_QK = 128
GLA_V = 256
GLA_IN = 800

ATT_HEAD_DIM = 64
ATT_HEADS = 4
ATT_KV_HEADS = 2
ROPE_THETA = 10000.0
ROPE_AXIS_DIM = 32
ATT_IN = 512

LANES = 128
SSD_DTX = 2 * SSD_HEADS * SSD_HEAD_DIM
SSD_PAD = SSD_INNER + SSD_XBC + SSD_DTX
GLA_PAD = 896
GLA_LR0 = 2 * GLA_QK + 2 * GLA_V
ATT_Q_PAD = ATT_HEADS * LANES
ATT_PAD = ATT_Q_PAD + 2 * LANES
ATT_VT_ROWS = ATT_HEAD_DIM + 16
ATT_ROW_CHUNK = 64
IN_PAD = HY_IN + SSD_PAD + GLA_PAD + ATT_PAD

VMEM_LIMIT = 48 * 1024 * 1024
F32 = jnp.float32
BF16 = jnp.bfloat16
HIGHEST = lax.Precision.HIGHEST
NT = (((1,), (1,)), ((), ()))
TN = (((0,), (0,)), ((), ()))


def _cparams(n_grid):
    return pltpu.CompilerParams(dimension_semantics=("arbitrary",) * n_grid,
                                vmem_limit_bytes=VMEM_LIMIT)


def _softplus(x):
    return jnp.maximum(x, 0.0) + jnp.log1p(jnp.exp(-jnp.abs(x)))


def _shift_rows(x, prev_row, next_row):
    n = x.shape[0]
    rid = lax.broadcasted_iota(jnp.int32, (8, x.shape[1]), 0)
    dn = pltpu.roll(x, 1, axis=0)
    up = pltpu.roll(x, n - 1, axis=0)
    xm = jnp.concatenate([jnp.where(rid == 0, prev_row, dn[0:8]), dn[8:]], axis=0)
    xp = jnp.concatenate([up[:n - 8], jnp.where(rid == 7, next_row, up[n - 8:])], axis=0)
    return xm, xp


def _silu(x):
    return x * (0.5 * jnp.tanh(0.5 * x) + 0.5)


def _in_proj_kernel(x_ref, g_ref, w_ref, hy_ref, ssd_ref, gla_ref, att_ref):
    x = x_ref[...]
    ms = jnp.mean(x * x, axis=-1, keepdims=True)
    h = (x * lax.rsqrt(ms + EPS) * g_ref[...]).astype(BF16)
    c0 = 0
    for ref in (hy_ref, ssd_ref, gla_ref, att_ref):
        n = ref.shape[-1]
        ref[...] = jnp.dot(h, w_ref[:, c0:c0 + n], preferred_element_type=F32)
        c0 += n


def in_proj(x2d, g, w_p, tm=256):
    t = x2d.shape[0]
    widths = (HY_IN, SSD_PAD, GLA_PAD, ATT_PAD)
    return pl.pallas_call(
        _in_proj_kernel,
        grid=(t // tm,),
        in_specs=[pl.BlockSpec((tm, D_MODEL), lambda i: (i, 0)),
                  pl.BlockSpec((1, D_MODEL), lambda i: (0, 0)),
                  pl.BlockSpec((D_MODEL, IN_PAD), lambda i: (0, 0), pipeline_mode=pl.Buffered(1))],
        out_specs=[pl.BlockSpec((tm, n), lambda i: (i, 0)) for n in widths],
        out_shape=[jax.ShapeDtypeStruct((t, n), F32) for n in widths],
        compiler_params=_cparams(1),
        name="in_proj",
    )(x2d, g.reshape(1, D_MODEL), w_p)


def _rope(x, cos_t, sin_t):
    lane = lax.broadcasted_iota(jnp.int32, x.shape, 1)
    nxt = pltpu.roll(x, LANES - 32, axis=1)
    prv = pltpu.roll(x, 32, axis=1)
    sw = jnp.where((lane % 64) < 32, nxt, prv)
    return x * cos_t + sw * sin_t


def _att_prep_kernel(u_ref, cos_ref, sin_ref, gq_ref, gk_ref, q_ref, k_ref, vt_ref):
    cos_t = cos_ref[...]
    sin_t = sin_ref[...]
    inv = 1.0 / ATT_HEAD_DIM
    qscale = (ATT_HEAD_DIM ** -0.5) * math.log2(math.e)
    for h in range(ATT_HEADS):
        x = u_ref[:, h * LANES:(h + 1) * LANES]
        ms = jnp.sum(x * x, axis=-1, keepdims=True) * inv
        xn = x * lax.rsqrt(ms + EPS) * gq_ref[...]
        q_ref[:, h * LANES:(h + 1) * LANES] = (_rope(xn, cos_t, sin_t) * qscale).astype(BF16)
    k = u_ref[:, ATT_Q_PAD:ATT_Q_PAD + LANES]
    lane = lax.broadcasted_iota(jnp.int32, k.shape, 1)
    k2 = k * k
    s_all = jnp.sum(k2, axis=-1, keepdims=True)
    s_lo = jnp.sum(jnp.where(lane < 64, k2, 0.0), axis=-1, keepdims=True)
    ms = jnp.where(lane < 64, s_lo, s_all - s_lo) * inv
    kn = k * lax.rsqrt(ms + EPS) * gk_ref[...]
    k_ref[...] = _rope(kn, cos_t, sin_t).astype(BF16)
    vt = u_ref[:, ATT_Q_PAD + LANES:ATT_PAD].T
    rid = lax.broadcasted_iota(jnp.int32, (ATT_VT_ROWS - ATT_HEAD_DIM, vt.shape[1]), 0)
    tail = jnp.where(rid == 0, 1.0, 0.0)
    for g in range(ATT_KV_HEADS):
        vt_ref[0, g] = jnp.concatenate(
            [vt[g * ATT_HEAD_DIM:(g + 1) * ATT_HEAD_DIM], tail], axis=0).astype(BF16)


def att_prep(u_att, cos_t, sin_t, gq, gk, batch, seq_len, tm=512):
    t = u_att.shape[0]
    nl = seq_len // tm
    return pl.pallas_call(
        _att_prep_kernel,
        grid=(t // tm,),
        in_specs=[pl.BlockSpec((tm, ATT_PAD), lambda i: (i, 0)),
                  pl.BlockSpec((tm, LANES), lambda i: (i % nl, 0)),
                  pl.BlockSpec((tm, LANES), lambda i: (i % nl, 0)),
                  pl.BlockSpec((1, LANES), lambda i: (0, 0)),
                  pl.BlockSpec((1, LANES), lambda i: (0, 0))],
        out_specs=[pl.BlockSpec((tm, ATT_Q_PAD), lambda i: (i, 0)),
                   pl.BlockSpec((tm, LANES), lambda i: (i, 0)),
                   pl.BlockSpec((1, ATT_KV_HEADS, ATT_VT_ROWS, tm), lambda i: (i // nl, 0, 0, i % nl))],
        out_shape=[jax.ShapeDtypeStruct((t, ATT_Q_PAD), BF16),
                   jax.ShapeDtypeStruct((t, LANES), BF16),
                   jax.ShapeDtypeStruct((batch, ATT_KV_HEADS, ATT_VT_ROWS, seq_len), BF16)],
        compiler_params=_cparams(1),
        name="att_prep",
    )(u_att, cos_t, sin_t, gq, gk)


def _flash_kernel(q_ref, k_ref, vt_ref, g_ref, o_ref, m_ref, acc_ref, st_ref, p_ref, *, tk, nk):
    m_ref[...] = jnp.full(m_ref.shape, -jnp.inf, F32)
    acc_ref[...] = jnp.zeros(acc_ref.shape, F32)

    def scores(j, slot, h):
        off = pl.multiple_of(j * tk, tk)
        k = k_ref[0, pl.ds(off, tk), :]
        q = q_ref[0, :, h * LANES:(h + 1) * LANES]
        st_ref[slot, h] = lax.dot_general(k, q, NT, preferred_element_type=F32)

    def consume(j, slot, h):
        off = pl.multiple_of(j * tk, tk)
        rows = [slice(r0, r0 + ATT_ROW_CHUNK) for r0 in range(0, tk, ATT_ROW_CHUNK)]
        m_prev = m_ref[h]
        m_new = m_prev
        for r in rows:
            m_new = jnp.maximum(m_new, jnp.max(st_ref[slot, h, r], axis=0, keepdims=True))
        alpha = jnp.exp2(m_prev - m_new)
        for r in rows:
            p_ref[h, r] = jnp.exp2((st_ref[slot, h, r] - m_new).astype(BF16))
        vt = vt_ref[0, h // 2, :, pl.ds(off, tk)]
        acc_ref[h] = alpha * acc_ref[h] + jnp.dot(vt, p_ref[h], preferred_element_type=F32)
        m_ref[h] = m_new

    for h in range(ATT_HEADS):
        scores(0, 0, h)

    def body(jj, carry):
        j = 2 * jj
        j2 = jnp.minimum(j + 2, nk - 1)
        for h in range(ATT_HEADS):
            scores(j + 1, 1, h)
            consume(j, 0, h)
        for h in range(ATT_HEADS):
            scores(j2, 0, h)
            consume(j + 1, 1, h)
        return carry

    lax.fori_loop(0, nk // 2, body, 0)
    ot = jnp.concatenate([acc_ref[h, 0:ATT_HEAD_DIM] / acc_ref[h, ATT_HEAD_DIM:ATT_HEAD_DIM + 1]
                          for h in range(ATT_HEADS)], axis=0)
    o = ot.T
    ms = jnp.mean(o * o, axis=-1, keepdims=True)
    o_ref[0] = (o * lax.rsqrt(ms + EPS) * g_ref[...]).astype(BF16)


def flash_attention(q, k, vt, g_out, tq=512, tk=512):
    b, seq_len, _ = q.shape
    return pl.pallas_call(
        functools.partial(_flash_kernel, tk=tk, nk=seq_len // tk),
        grid=(b, seq_len // tq),
        in_specs=[pl.BlockSpec((1, tq, ATT_Q_PAD), lambda bi, i: (bi, i, 0)),
                  pl.BlockSpec((1, seq_len, LANES), lambda bi, i: (bi, 0, 0)),
                  pl.BlockSpec((1, ATT_KV_HEADS, ATT_VT_ROWS, seq_len), lambda bi, i: (bi, 0, 0, 0)),
                  pl.BlockSpec((1, GROUP_W), lambda bi, i: (0, 0))],
        out_specs=pl.BlockSpec((1, tq, GROUP_W), lambda bi, i: (bi, i, 0)),
        out_shape=jax.ShapeDtypeStruct((b, seq_len, GROUP_W), BF16),
        scratch_shapes=[pltpu.VMEM((ATT_HEADS, 1, tq), F32),
                        pltpu.VMEM((ATT_HEADS, ATT_VT_ROWS, tq), F32),
                        pltpu.VMEM((2, ATT_HEADS, tk, tq), F32),
                        pltpu.VMEM((ATT_HEADS, tk, tq), BF16)],
        compiler_params=_cparams(2),
        name="flash_attention",
    )(q, k, vt, g_out.reshape(1, GROUP_W))


def _tri(n, upper, block=None):
    ii = lax.broadcasted_iota(jnp.int32, (n, n), 0)
    jj = lax.broadcasted_iota(jnp.int32, (n, n), 1)
    m = (jj >= ii) if upper else (jj <= ii)
    if block is not None:
        m = m & ((ii // block) == (jj // block))
    return m


def _tri_dot(t, x):
    hi = x.astype(BF16)
    r1 = x - hi.astype(F32)
    mid = r1.astype(BF16)
    lo = (r1 - mid.astype(F32)).astype(BF16)
    n = x.shape[1]
    y = jnp.dot(t, jnp.concatenate([hi, mid, lo], axis=1), preferred_element_type=F32)
    return (y[:, 0:n] + y[:, n:2 * n]) + y[:, 2 * n:]


def _ssd_fwd_kernel(u_ref, up_ref, un_ref, cw_ref, cb_ref, dtb_ref, ax_ref, dx_ref,
                    yp_ref, cm_ref, bm_ref, wxb_ref, eb_ref, s_ref, *, tm):
    Q = SSD_CHUNK
    i = pl.program_id(1)
    nb = pl.num_programs(1)

    @pl.when(i == 0)
    def _():
        s_ref[...] = jnp.zeros(s_ref.shape, F32)

    x = u_ref[:, SSD_INNER:SSD_INNER + SSD_XBC]
    prev_row = jnp.where(i > 0, up_ref[7:8, SSD_INNER:SSD_INNER + SSD_XBC], 0.0)
    next_row = jnp.where(i < nb - 1, un_ref[0:1, SSD_INNER:SSD_INNER + SSD_XBC], 0.0)
    xm, xp = _shift_rows(x, prev_row, next_row)
    conv = xm * cw_ref[0:1] + x * cw_ref[1:2] + xp * cw_ref[2:3] + cb_ref[...]
    xbc = _silu(conv)
    xs = xbc[:, 0:SSD_INNER]
    bm = xbc[:, SSD_INNER:SSD_INNER + SSD_BC].astype(BF16)
    cm = xbc[:, SSD_INNER + SSD_BC:].astype(BF16)
    bm_ref[...] = bm
    cm_ref[...] = cm
    dt = _softplus(u_ref[:, SSD_INNER + SSD_XBC:] + dtb_ref[...])
    da = dt * ax_ref[...]

    tl = _tri(Q, False).astype(BF16)
    tu = _tri(Q, True).astype(BF16)
    ii = lax.broadcasted_iota(jnp.int32, (Q, Q), 0)
    jj = lax.broadcasted_iota(jnp.int32, (Q, Q), 1)
    lo_half = lax.broadcasted_iota(jnp.int32, (Q, LANES), 1) < SSD_HEAD_DIM
    ninf = jnp.float32(-jnp.inf)
    chunks = [slice(c * Q, (c + 1) * Q) for c in range(tm // Q)]
    groups = [slice(g * LANES, (g + 1) * LANES) for g in range(SSD_GROUPS)]
    dtf, dtb = dt[:, 0:SSD_INNER], dt[:, SSD_INNER:]
    cum = jnp.concatenate([_tri_dot(tl, da[r, 0:SSD_INNER]) for r in chunks], axis=0)
    rc = jnp.concatenate([_tri_dot(tu, da[r, SSD_INNER:]) for r in chunks], axis=0)
    e_cum = jnp.exp(cum)
    eb_ref[...] = jnp.exp(rc)
    xs_b = xs.astype(BF16)
    wxf = []
    for r in chunks:
        wxf.append((xs[r] * jnp.exp(cum[r][Q - 1:Q] - cum[r]) * dtf[r]).astype(BF16))
        wxb_ref[r] = (xs[r] * jnp.exp(rc[r][0:1] - rc[r]) * dtb[r]).astype(BF16)
    yd = []
    for r in chunks:
        for ln in groups:
            cumg, rcg = cum[r, ln], rc[r, ln]
            cb = lax.dot_general(cm[r, ln], bm[r, ln], NT, preferred_element_type=F32)
            cum_t, rc_t, dtf_t, dtb_t = cumg.T, rcg.T, dtf[r, ln].T, dtb[r, ln].T
            cum_sw, rc_sw = pltpu.roll(cumg, SSD_HEAD_DIM, axis=1), pltpu.roll(rcg, SSD_HEAD_DIM, axis=1)
            halves = []
            for hh in range(2):
                row = slice(hh * SSD_HEAD_DIM, hh * SSD_HEAD_DIM + 1)
                col_c = jnp.where(lo_half, cumg, cum_sw) if hh == 0 else jnp.where(lo_half, cum_sw, cumg)
                col_r = jnp.where(lo_half, rcg, rc_sw) if hh == 0 else jnp.where(lo_half, rc_sw, rcg)
                ef = jnp.exp(jnp.where(jj <= ii, col_c - cum_t[row], ninf)) * dtf_t[row]
                eb = jnp.exp(jnp.where(jj >= ii, col_r - rc_t[row], ninf)) * dtb_t[row]
                w = (cb * (ef + eb)).astype(BF16)
                halves.append(jnp.dot(w, xs_b[r, ln], preferred_element_type=F32))
            yd.append(jnp.where(lo_half, halves[0], halves[1]) + dx_ref[:, ln] * xs[r, ln])
    state = [s_ref[g] for g in range(SSD_GROUPS)]
    for c, r in enumerate(chunks):
        for g, ln in enumerate(groups):
            yp_ref[r, ln] = yd[c * SSD_GROUPS + g] + jnp.dot(
                cm[r, ln], state[g].astype(BF16), preferred_element_type=F32) * e_cum[r, ln]
            state[g] = state[g] * e_cum[r, ln][Q - 1:Q] + lax.dot_general(
                bm[r, ln], wxf[c][:, ln], TN, preferred_element_type=F32)
    for g in range(SSD_GROUPS):
        s_ref[g] = state[g]


def _ssd_bwd_kernel(yp_ref, cm_ref, bm_ref, wxb_ref, eb_ref, z_ref, g_ref, o_ref, s_ref, y_ref, *, tm):
    Q = SSD_CHUNK

    @pl.when(pl.program_id(1) == 0)
    def _():
        s_ref[...] = jnp.zeros(s_ref.shape, F32)

    state = [s_ref[g] for g in range(SSD_GROUPS)]
    for c in reversed(range(tm // Q)):
        r = slice(c * Q, (c + 1) * Q)
        for g in range(SSD_GROUPS):
            ln = slice(g * LANES, (g + 1) * LANES)
            ebg = eb_ref[r, ln]
            y_ref[r, ln] = yp_ref[r, ln] + jnp.dot(cm_ref[r, ln], state[g].astype(BF16),
                                                   preferred_element_type=F32) * ebg
            state[g] = state[g] * ebg[0:1] + lax.dot_general(bm_ref[r, ln], wxb_ref[r, ln], TN,
                                                             preferred_element_type=F32)
    for g in range(SSD_GROUPS):
        s_ref[g] = state[g]
    y = y_ref[...] * _silu(z_ref[...])
    ms = jnp.mean(y * y, axis=-1, keepdims=True)
    o_ref[...] = (y * lax.rsqrt(ms + EPS) * g_ref[...]).astype(BF16)


def ssd_mixer(u_ssd, batch, seq_len, conv_w, conv_b, dt_bias_x, a_x, d_x, norm_g, tm=512, tm_bwd=1024):
    t = u_ssd.shape[0]
    nb = seq_len // tm
    hb = tm // 8
    last_halo = t // 8 - 1
    row = lambda b, i: (b * nb + i, 0)
    const = lambda b, i: (0, 0)
    outs = pl.pallas_call(
        functools.partial(_ssd_fwd_kernel, tm=tm),
        grid=(batch, nb),
        in_specs=[pl.BlockSpec((tm, SSD_PAD), row),
                  pl.BlockSpec((8, SSD_PAD), lambda b, i: (jnp.maximum((b * nb + i) * hb - 1, 0), 0)),
                  pl.BlockSpec((8, SSD_PAD), lambda b, i: (jnp.minimum((b * nb + i + 1) * hb, last_halo), 0)),
                  pl.BlockSpec((3, SSD_XBC), const),
                  pl.BlockSpec((1, SSD_XBC), const),
                  pl.BlockSpec((1, SSD_DTX), const),
                  pl.BlockSpec((1, SSD_DTX), const),
                  pl.BlockSpec((1, SSD_INNER), const)],
        out_specs=[pl.BlockSpec((tm, SSD_INNER), row)] * 5,
        out_shape=[jax.ShapeDtypeStruct((t, SSD_INNER), F32),
                   jax.ShapeDtypeStruct((t, SSD_INNER), BF16),
                   jax.ShapeDtypeStruct((t, SSD_INNER), BF16),
                   jax.ShapeDtypeStruct((t, SSD_INNER), BF16),
                   jax.ShapeDtypeStruct((t, SSD_INNER), F32)],
        scratch_shapes=[pltpu.VMEM((SSD_GROUPS, SSD_STATE, LANES), F32)],
        compiler_params=_cparams(2),
        name="ssd_fwd",
    )(u_ssd, u_ssd, u_ssd, conv_w, conv_b.reshape(1, SSD_XBC), dt_bias_x, a_x, d_x)
    nb = seq_len // tm_bwd
    rrow = lambda b, i: (b * nb + nb - 1 - i, 0)
    return pl.pallas_call(
        functools.partial(_ssd_bwd_kernel, tm=tm_bwd),
        grid=(batch, nb),
        in_specs=[pl.BlockSpec((tm_bwd, SSD_INNER), rrow)] * 6 + [pl.BlockSpec((1, SSD_INNER), const)],
        out_specs=pl.BlockSpec((tm_bwd, SSD_INNER), rrow),
        out_shape=jax.ShapeDtypeStruct((t, SSD_INNER), BF16),
        scratch_shapes=[pltpu.VMEM((SSD_GROUPS, SSD_STATE, LANES), F32),
                        pltpu.VMEM((tm_bwd, SSD_INNER), F32)],
        compiler_params=_cparams(2),
        name="ssd_bwd",
    )(*outs, u_ssd, norm_g.reshape(1, SSD_INNER))


def _gla_fwd_kernel(u_ref, wg_ref, bg_ref, op_ref, qb_ref, kb_ref, db_ref, st_ref, *, tm):
    Q = GLA_CHUNK
    P = 2 * Q

    @pl.when(pl.program_id(1) == 0)
    def _():
        st_ref[...] = jnp.zeros(st_ref.shape, F32)

    pre = jnp.dot(u_ref[:, GLA_LR0:GLA_PAD], wg_ref[...], precision=HIGHEST,
                  preferred_element_type=F32) + bg_ref[...]
    gk = (jnp.minimum(pre, 0.0) - jnp.log1p(jnp.exp(-jnp.abs(pre)))) * (1.0 / GLA_NORMALIZER)

    mask_f = _tri(P, False, Q)
    mask_b = _tri(P, True, Q)
    tl = mask_f.astype(BF16)
    tu = mask_b.astype(BF16)
    lane_k = lax.broadcasted_iota(jnp.int32, (P, GLA_QK), 1) // GLA_DK
    lane_v = lax.broadcasted_iota(jnp.int32, (P, GLA_V), 1) // GLA_DV
    bd_t = (lax.broadcasted_iota(jnp.int32, (GLA_V, GLA_QK), 0) // GLA_DV ==
            lax.broadcasted_iota(jnp.int32, (GLA_V, GLA_QK), 1) // GLA_DK)

    pairs = [slice(c * P, (c + 1) * P) for c in range(tm // P)]
    heads = range(GLA_HEADS)
    gf = jnp.concatenate([_tri_dot(tl, gk[r, 0:GLA_QK]) for r in pairs], axis=0)
    gb = jnp.concatenate([_tri_dot(tu, gk[r, GLA_QK:]) for r in pairs], axis=0)
    qs = u_ref[:, 0:GLA_QK] * (GLA_DK ** -0.5)
    k = u_ref[:, GLA_QK:2 * GLA_QK]
    v = u_ref[:, 2 * GLA_QK:2 * GLA_QK + GLA_V]
    vb = v.astype(BF16)
    q_f = qs * jnp.exp(gf)
    k_f = (k * jnp.exp(-gf)).astype(BF16)
    q_b = qs * jnp.exp(gb)
    k_b = (k * jnp.exp(-gb)).astype(BF16)
    q_fb = q_f.astype(BF16)
    qb_ref[...] = q_b.astype(BF16)
    db_ref[...] = jnp.exp(gb)
    af, ab = [], []
    for r in pairs:
        qf_h = jnp.concatenate([jnp.where(lane_k == h, q_f[r], 0.0) for h in heads], axis=0).astype(BF16)
        qb_h = jnp.concatenate([jnp.where(lane_k == h, q_b[r], 0.0) for h in heads], axis=0).astype(BF16)
        af.append(lax.dot_general(qf_h, k_f[r], NT, preferred_element_type=F32))
        ab.append(lax.dot_general(qb_h, k_b[r], NT, preferred_element_type=F32))
    o = []
    for c, r in enumerate(pairs):
        att = jnp.concatenate([jnp.where(mask_f, af[c][h * P:(h + 1) * P], 0.0) +
                               jnp.where(mask_b, ab[c][h * P:(h + 1) * P], 0.0) for h in heads], axis=1)
        v_h = jnp.concatenate([jnp.where(lane_v == h, v[r], 0.0) for h in heads], axis=0).astype(BF16)
        o.append(jnp.dot(att.astype(BF16), v_h, preferred_element_type=F32))
    st = st_ref[...]
    for c in range(tm // Q):
        ro = slice(c * Q, (c + 1) * Q)
        rr = slice((c % 2) * Q, (c % 2 + 1) * Q)
        g_last = gf[c * Q + Q - 1:c * Q + Q]
        k_end = (k[ro] * jnp.exp(g_last - gf[ro])).astype(BF16)
        op_ref[ro] = o[c // 2][rr] + lax.dot_general(q_fb[ro], st.astype(BF16), NT, preferred_element_type=F32)
        upd = lax.dot_general(vb[ro], k_end, TN, preferred_element_type=F32)
        st = st * jnp.exp(g_last) + jnp.where(bd_t, upd, 0.0)
        kb_ref[ro] = (k[ro] * jnp.exp(gb[c * Q:c * Q + 1] - gb[ro])).astype(BF16)
    st_ref[...] = st


def _gla_bwd_kernel(op_ref, qb_ref, kb_ref, db_ref, v_ref, go_ref, g_ref, o_ref, st_ref, y_ref, *, tm):
    Q = GLA_CHUNK

    @pl.when(pl.program_id(1) == 0)
    def _():
        st_ref[...] = jnp.zeros(st_ref.shape, F32)

    bd_t = (lax.broadcasted_iota(jnp.int32, (GLA_V, GLA_QK), 0) // GLA_DV ==
            lax.broadcasted_iota(jnp.int32, (GLA_V, GLA_QK), 1) // GLA_DK)
    st = st_ref[...]
    for c in reversed(range(tm // Q)):
        r = slice(c * Q, (c + 1) * Q)
        y_ref[r] = op_ref[r] + lax.dot_general(qb_ref[r], st.astype(BF16), NT, preferred_element_type=F32)
        upd = lax.dot_general(v_ref[r].astype(BF16), kb_ref[r], TN, preferred_element_type=F32)
        st = st * db_ref[c * Q:c * Q + 1] + jnp.where(bd_t, upd, 0.0)
    st_ref[...] = st
    y = y_ref[...]
    seg = (lax.broadcasted_iota(jnp.int32, (GLA_V, GLA_V), 0) // GLA_DV ==
           lax.broadcasted_iota(jnp.int32, (GLA_V, GLA_V), 1) // GLA_DV).astype(F32)
    ms = jnp.dot(y * y, seg, precision=HIGHEST, preferred_element_type=F32) * (1.0 / GLA_DV)
    o_ref[...] = (y * lax.rsqrt(ms + EPS) * g_ref[...] * _silu(go_ref[...])).astype(BF16)


def gla_mixer(u_gla, batch, seq_len, wg, bg, norm_g_x, tm=512, tm_bwd=1024):
    t = u_gla.shape[0]
    nb = seq_len // tm
    row = lambda b, i: (b * nb + i, 0)
    const = lambda b, i: (0, 0)
    outs = pl.pallas_call(
        functools.partial(_gla_fwd_kernel, tm=tm),
        grid=(batch, nb),
        in_specs=[pl.BlockSpec((tm, GLA_PAD), row),
                  pl.BlockSpec((GLA_PAD - GLA_LR0, 2 * GLA_QK), const),
                  pl.BlockSpec((1, 2 * GLA_QK), const)],
        out_specs=[pl.BlockSpec((tm, GLA_V), row), pl.BlockSpec((tm, GLA_QK), row),
                   pl.BlockSpec((tm, GLA_QK), row), pl.BlockSpec((tm, GLA_QK), row)],
        out_shape=[jax.ShapeDtypeStruct((t, GLA_V), F32),
                   jax.ShapeDtypeStruct((t, GLA_QK), BF16),
                   jax.ShapeDtypeStruct((t, GLA_QK), BF16),
                   jax.ShapeDtypeStruct((t, GLA_QK), F32)],
        scratch_shapes=[pltpu.VMEM((GLA_V, GLA_QK), F32)],
        compiler_params=_cparams(2),
        name="gla_fwd",
    )(u_gla, wg, bg)
    nb = seq_len // tm_bwd
    rrow = lambda b, i: (b * nb + nb - 1 - i, 0)
    return pl.pallas_call(
        functools.partial(_gla_bwd_kernel, tm=tm_bwd),
        grid=(batch, nb),
        in_specs=[pl.BlockSpec((tm_bwd, GLA_V), rrow), pl.BlockSpec((tm_bwd, GLA_QK), rrow),
                  pl.BlockSpec((tm_bwd, GLA_QK), rrow), pl.BlockSpec((tm_bwd, GLA_QK), rrow),
                  pl.BlockSpec((tm_bwd, GLA_V), lambda b, i: (b * nb + nb - 1 - i, 1)),
                  pl.BlockSpec((tm_bwd, GLA_V), lambda b, i: (b * nb + nb - 1 - i, 2)),
                  pl.BlockSpec((1, GLA_V), const)],
        out_specs=pl.BlockSpec((tm_bwd, GLA_V), rrow),
        out_shape=jax.ShapeDtypeStruct((t, GLA_V), BF16),
        scratch_shapes=[pltpu.VMEM((GLA_V, GLA_QK), F32), pltpu.VMEM((tm_bwd, GLA_V), F32)],
        compiler_params=_cparams(2),
        name="gla_bwd",
    )(*outs, u_gla, u_gla, norm_g_x)


def _out_ffn_kernel(x_ref, y0_ref, y1_ref, y2_ref, y3_ref, wo_ref, g2_ref, wu_ref, wd_ref, o_ref, *, tf):
    mixed = jnp.concatenate([y0_ref[...], y1_ref[...], y2_ref[...], y3_ref[...]], axis=1)
    x1 = x_ref[...] + jnp.dot(mixed, wo_ref[...], preferred_element_type=F32)
    ms = jnp.mean(x1 * x1, axis=-1, keepdims=True)
    h2 = (x1 * lax.rsqrt(ms + EPS) * g2_ref[...]).astype(BF16)
    acc = x1
    for c in range(D_FF // tf):
        up = jnp.dot(h2, wu_ref[:, c * tf:(c + 1) * tf], preferred_element_type=F32)
        a = jnp.square(jnp.maximum(up, 0.0)).astype(BF16)
        acc = acc + jnp.dot(a, wd_ref[c * tf:(c + 1) * tf, :], preferred_element_type=F32)
    o_ref[...] = acc


def out_ffn(x2d, ys, wo, g2, wu, wd, tm=512, tf=1024):
    t = x2d.shape[0]
    const = lambda i: (0, 0)
    single = pl.Buffered(1)
    return pl.pallas_call(
        functools.partial(_out_ffn_kernel, tf=tf),
        grid=(t // tm,),
        in_specs=[pl.BlockSpec((tm, D_MODEL), lambda i: (i, 0))] +
                 [pl.BlockSpec((tm, GROUP_W), lambda i: (i, 0))] * 4 +
                 [pl.BlockSpec((D_MODEL, D_MODEL), const, pipeline_mode=single),
                  pl.BlockSpec((1, D_MODEL), const),
                  pl.BlockSpec((D_MODEL, D_FF), const, pipeline_mode=single),
                  pl.BlockSpec((D_FF, D_MODEL), const, pipeline_mode=single)],
        out_specs=pl.BlockSpec((tm, D_MODEL), lambda i: (i, 0)),
        out_shape=jax.ShapeDtypeStruct((t, D_MODEL), F32),
        compiler_params=_cparams(1),
        name="out_ffn",
    )(x2d, *ys, wo, g2.reshape(1, D_MODEL), wu, wd)


HY_N1 = 128
HY_ROWS = HY_N1 // 2


def _hy_pre_kernel(u_ref, up_ref, un_ref, cw_ref, cb_ref, o_ref, *, tm):
    i = pl.program_id(1)
    nb = pl.num_programs(1)
    x = u_ref[...]
    prev_row = jnp.where(i > 0, up_ref[7:8, :], 0.0)
    next_row = jnp.where(i < nb - 1, un_ref[0:1, :], 0.0)
    xm, xp = _shift_rows(x, prev_row, next_row)
    uc = xm * cw_ref[0:1] + x * cw_ref[1:2] + xp * cw_ref[2:3] + cb_ref[...]
    o_ref[0] = uc.T


def hy_pre(u_hy, batch, seq_len, conv_w, conv_b, tm=512):
    t = u_hy.shape[0]
    nb = seq_len // tm
    hb = tm // 8
    last_halo = t // 8 - 1
    const = lambda b, i: (0, 0)
    return pl.pallas_call(
        functools.partial(_hy_pre_kernel, tm=tm),
        grid=(batch, nb),
        in_specs=[pl.BlockSpec((tm, HY_IN), lambda b, i: (b * nb + i, 0)),
                  pl.BlockSpec((8, HY_IN), lambda b, i: (jnp.maximum((b * nb + i) * hb - 1, 0), 0)),
                  pl.BlockSpec((8, HY_IN), lambda b, i: (jnp.minimum((b * nb + i + 1) * hb, last_halo), 0)),
                  pl.BlockSpec((3, HY_IN), const),
                  pl.BlockSpec((1, HY_IN), const)],
        out_specs=pl.BlockSpec((1, HY_IN, tm), lambda b, i: (b, 0, i)),
        out_shape=jax.ShapeDtypeStruct((batch, HY_IN, seq_len), F32),
        compiler_params=_cparams(2),
        name="hy_pre",
    )(u_hy, u_hy, u_hy, conv_w, conv_b.reshape(1, HY_IN))


def _hy_conv_kernel(z_ref, g_ref, sk_ref, h_ref, f1_ref, w2_ref, t_ref, gi_ref, o_ref, *, cb, n2):
    ct = z_ref.shape[1]
    f1 = f1_ref[...]
    w2 = w2_ref[...]
    tre, tim = t_ref[0:HY_N1], t_ref[HY_N1:]
    gi = gi_ref[...]

    def quads(blk):
        return blk[0:HY_N1, 0:n2], blk[0:HY_N1, n2:], blk[HY_N1:, 0:n2], blk[HY_N1:, n2:]

    def body(it, carry):
        c0 = it * cb
        planes = []
        for u in range(cb):
            a = jnp.dot(f1, z_ref[0, c0 + u].astype(BF16), preferred_element_type=F32)
            are, aim = a[0:HY_N1], a[HY_N1:]
            planes += [(are * tre - aim * tim).astype(BF16), (are * tim + aim * tre).astype(BF16)]
        spec = jnp.dot(jnp.concatenate(planes, axis=0), w2, preferred_element_type=F32)
        planes = []
        for u in range(cb):
            tl, tr, bl, br = quads(spec[u * 2 * HY_N1:(u + 1) * 2 * HY_N1])
            bre, bim = tl - br, tr + bl
            hre, him = h_ref[0, c0 + u, 0:HY_N1], h_ref[0, c0 + u, HY_N1:]
            planes += [(bre * hre - bim * him).astype(BF16), (bre * him + bim * hre).astype(BF16)]
        back = jnp.dot(jnp.concatenate(planes, axis=0), w2, preferred_element_type=F32)
        for u in range(cb):
            tl, tr, bl, br = quads(back[u * 2 * HY_N1:(u + 1) * 2 * HY_N1])
            are, aim = tl + br, bl - tr
            st = jnp.concatenate([are * tre + aim * tim, aim * tre - are * tim], axis=0).astype(BF16)
            y = jnp.dot(gi, st, preferred_element_type=F32)
            o_ref[0, c0 + u] = g_ref[0, c0 + u] * (y + z_ref[0, c0 + u] * sk_ref[c0 + u])
        return carry

    lax.fori_loop(0, ct // cb, body, 0)


def hy_conv(z4, z_off, gate4, gate_off, skip, h_spec, order, consts, ct=32, cb=None):
    b, _, _, n2 = z4.shape
    if cb is None:
        cb = min(ct, 2048 // n2)
    f1, w2, tw, gi = consts
    nct = HY_C // ct
    seq = lambda off: (lambda bi, j: (bi, off // ct + j, 0, 0))
    const = lambda bi, j: (0, 0)
    return pl.pallas_call(
        functools.partial(_hy_conv_kernel, cb=cb, n2=n2),
        grid=(b, nct),
        in_specs=[pl.BlockSpec((1, ct, HY_ROWS, n2), seq(z_off)),
                  pl.BlockSpec((1, ct, HY_ROWS, n2), seq(gate_off)),
                  pl.BlockSpec((ct, 1, 1), lambda bi, j: (j, 0, 0)),
                  pl.BlockSpec((1, ct, 2 * HY_N1, n2), lambda bi, j: (order, j, 0, 0)),
                  pl.BlockSpec((2 * HY_N1, HY_ROWS), const),
                  pl.BlockSpec((n2, 2 * n2), const),
                  pl.BlockSpec((2 * HY_N1, n2), const),
                  pl.BlockSpec((HY_ROWS, 2 * HY_N1), const)],
        out_specs=pl.BlockSpec((1, ct, HY_ROWS, n2), lambda bi, j: (bi, j, 0, 0)),
        out_shape=jax.ShapeDtypeStruct((b, HY_C, HY_ROWS, n2), F32),
        compiler_params=_cparams(2),
        name="hy_conv",
    )(z4, gate4, skip.reshape(HY_C, 1, 1), h_spec, f1, w2, tw, gi)


def _hy_post_kernel(z_ref, g_ref, o_ref):
    x = z_ref[0]
    ms = jnp.mean(x * x, axis=0, keepdims=True)
    o_ref[...] = (x * lax.rsqrt(ms + EPS) * g_ref[...]).T.astype(BF16)


def hy_post(z3, g, tm=512):
    b, _, seq_len = z3.shape
    nb = seq_len // tm
    return pl.pallas_call(
        _hy_post_kernel,
        grid=(b, nb),
        in_specs=[pl.BlockSpec((1, HY_C, tm), lambda bi, i: (bi, 0, i)),
                  pl.BlockSpec((HY_C, 1), lambda bi, i: (0, 0))],
        out_specs=pl.BlockSpec((tm, HY_C), lambda bi, i: (bi * nb + i, 0)),
        out_shape=jax.ShapeDtypeStruct((b * seq_len, HY_C), BF16),
        compiler_params=_cparams(2),
        name="hy_post",
    )(z3, g.reshape(HY_C, 1))


def _dft_consts(n2):
    n = HY_N1 * n2
    k1 = np.arange(HY_N1, dtype=np.float64)
    a1 = 2.0 * np.pi * np.outer(k1, np.arange(HY_ROWS)) / HY_N1
    f1 = np.concatenate([np.cos(a1), -np.sin(a1)], axis=0)
    m2 = np.arange(n2, dtype=np.float64)
    a2 = 2.0 * np.pi * np.outer(m2, m2) / n2
    w2 = np.concatenate([np.cos(a2), -np.sin(a2)], axis=1)
    at = 2.0 * np.pi * np.outer(k1, m2) / n
    tw = np.concatenate([np.cos(at), -np.sin(at)], axis=0)
    return (jnp.asarray(f1, BF16), jnp.asarray(w2, BF16), jnp.asarray(tw, F32), jnp.asarray(f1.T, BF16))


def _hy_taps_kernel(z_ref, w1_ref, b1_ref, w2_ref, b2_ref, fr_ref, w3_ref, dl_ref, o_ref, hid_ref):
    @pl.when((pl.program_id(0) == 0) & (pl.program_id(1) == 0))
    def _():
        fr = fr_ref[...]
        hid = jnp.sin(fr * (jnp.dot(w1_ref[...], z_ref[...], precision=HIGHEST,
                                    preferred_element_type=F32) + b1_ref[...]))
        hid_ref[...] = jnp.sin(fr * (jnp.dot(w2_ref[...], hid, precision=HIGHEST,
                                             preferred_element_type=F32) + b2_ref[...]))

    hid = hid_ref[...]
    dec = jnp.exp(-z_ref[0:1, :] * jnp.abs(dl_ref[...]))
    h = [jnp.dot(w3_ref[0, d], hid, precision=HIGHEST, preferred_element_type=F32) * dec for d in range(2)]
    norm = (jnp.sum(jnp.abs(h[0]), axis=1, keepdims=True) + jnp.sum(jnp.abs(h[1]), axis=1, keepdims=True))
    for d in range(2):
        o_ref[0, d] = h[d] / norm


def hy_taps(L, w1, b1, w2, b2, w3, freq, rb=32):
    t = jnp.linspace(0.0, 1.0, L, dtype=F32)[:, None]
    w = 2.0 * math.pi * jnp.arange(L, dtype=F32) / L
    bands = jnp.linspace(1e-4, HY_BANDS - 1, HY_BANDS, dtype=F32)
    ang = w[:, None] * bands
    zt = jnp.concatenate([t, jnp.cos(ang), -jnp.sin(ang)], axis=-1).T
    deltas = jnp.linspace(HY_MIN_DECAY, HY_MAX_DECAY, HY_C, dtype=F32).reshape(HY_C, 1)
    hid_n = w2.shape[0]
    emb = zt.shape[0]
    col = lambda v: v.reshape(hid_n, 1)
    w3t = w3.T.reshape(HY_ORDER, 2, HY_C, hid_n)
    const = lambda o, j: (0, 0)
    return pl.pallas_call(
        _hy_taps_kernel,
        grid=(HY_ORDER, HY_C // rb),
        in_specs=[pl.BlockSpec((emb, L), const),
                  pl.BlockSpec((hid_n, emb), const), pl.BlockSpec((hid_n, 1), const),
                  pl.BlockSpec((hid_n, hid_n), const), pl.BlockSpec((hid_n, 1), const),
                  pl.BlockSpec((hid_n, 1), const),
                  pl.BlockSpec((1, 2, rb, hid_n), lambda o, j: (o, 0, j, 0)),
                  pl.BlockSpec((rb, 1), lambda o, j: (j, 0))],
        out_specs=pl.BlockSpec((1, 2, rb, L), lambda o, j: (o, 0, j, 0)),
        out_shape=jax.ShapeDtypeStruct((HY_ORDER, 2, HY_C, L), F32),
        scratch_shapes=[pltpu.VMEM((hid_n, L), F32)],
        compiler_params=_cparams(2),
        name="hy_taps",
    )(zt, w1.T, col(b1), w2.T, col(b2), col(freq), w3t, deltas)


def _hy_spec_kernel(h_ref, f1_ref, w2_ref, t_ref, o_ref, *, cb, n2):
    ct = h_ref.shape[2]
    f1 = f1_ref[...]
    w2 = w2_ref[...]
    tre, tim = t_ref[0:HY_N1], t_ref[HY_N1:]
    inv_n = 1.0 / (HY_N1 * n2)

    def body(it, carry):
        c0 = it * cb
        planes = []
        for u in range(cb):
            for d in range(2):
                a = jnp.dot(f1, h_ref[0, d, c0 + u].astype(BF16), preferred_element_type=F32)
                are, aim = a[0:HY_N1], a[HY_N1:]
                planes += [(are * tre - aim * tim).astype(BF16), (are * tim + aim * tre).astype(BF16)]
        spec = jnp.dot(jnp.concatenate(planes, axis=0), w2, preferred_element_type=F32)
        for u in range(cb):
            re, im = [], []
            for d in range(2):
                blk = spec[(2 * u + d) * 2 * HY_N1:(2 * u + d + 1) * 2 * HY_N1]
                tl, tr, bl, br = blk[0:HY_N1, 0:n2], blk[0:HY_N1, n2:], blk[HY_N1:, 0:n2], blk[HY_N1:, n2:]
                re.append(tl - br)
                im.append(tr + bl)
            o_ref[0, c0 + u, 0:HY_N1] = (re[0] + re[1]) * inv_n
            o_ref[0, c0 + u, HY_N1:] = (im[0] - im[1]) * inv_n
        return carry

    lax.fori_loop(0, ct // cb, body, 0)


def hy_spec(taps, consts, ct=32, cb=8):
    L = taps.shape[-1]
    n2 = 2 * L // HY_N1
    f1, w2, tw, _ = consts
    const = lambda o, j: (0, 0)
    return pl.pallas_call(
        functools.partial(_hy_spec_kernel, cb=cb, n2=n2),
        grid=(HY_ORDER, HY_C // ct),
        in_specs=[pl.BlockSpec((1, 2, ct, HY_ROWS, n2), lambda o, j: (o, 0, j, 0, 0)),
                  pl.BlockSpec((2 * HY_N1, HY_ROWS), const),
                  pl.BlockSpec((n2, 2 * n2), const),
                  pl.BlockSpec((2 * HY_N1, n2), const)],
        out_specs=pl.BlockSpec((1, ct, 2 * HY_N1, n2), lambda o, j: (o, j, 0, 0)),
        out_shape=jax.ShapeDtypeStruct((HY_ORDER, HY_C, 2 * HY_N1, n2), F32),
        compiler_params=_cparams(2),
        name="hy_spec",
    )(taps.reshape(HY_ORDER, 2, HY_C, HY_ROWS, n2), f1, w2, tw)


def _hyena_filter_freq(L, w1, b1, w2, b2, w3, freq):
    return hy_spec(hy_taps(L, w1, b1, w2, b2, w3, freq), _dft_consts(2 * L // HY_N1))


def hyena_mixer(u_hy, batch, seq_len, p, h_spec):
    n2 = 2 * seq_len // HY_N1
    consts = _dft_consts(n2)
    uct = hy_pre(u_hy, batch, seq_len, p['hy_conv_w'], p['hy_conv_b'])
    uc4 = uct.reshape(batch, HY_IN, HY_ROWS, n2)
    z = hy_conv(uc4, 0, uc4, HY_C, p['hy_skip'][0], h_spec, 0, consts)
    z = hy_conv(z, 0, uc4, 2 * HY_C, p['hy_skip'][1], h_spec, 1, consts)
    return hy_post(z.reshape(batch, HY_C, seq_len), p['hy_out_g'])


_DEINTERLEAVE = np.concatenate([np.arange(0, ATT_HEAD_DIM, 2), np.arange(1, ATT_HEAD_DIM, 2)])


def _pack_w_in(w):
    c0 = 0
    w_hy = w[:, c0:c0 + HY_IN]; c0 += HY_IN
    w_ssd = w[:, c0:c0 + SSD_IN]; c0 += SSD_IN
    w_gla = w[:, c0:c0 + GLA_IN]; c0 += GLA_IN
    w_att = w[:, c0:c0 + ATT_IN]
    zeros = lambda n: jnp.zeros((D_MODEL, n), w.dtype)
    cols = [w_hy, w_ssd[:, :SSD_INNER + SSD_XBC],
            jnp.repeat(w_ssd[:, SSD_INNER + SSD_XBC:], SSD_HEAD_DIM, axis=1),
            w_gla, zeros(GLA_PAD - GLA_IN)]
    for h in range(ATT_HEADS):
        wh = w_att[:, h * ATT_HEAD_DIM:(h + 1) * ATT_HEAD_DIM][:, _DEINTERLEAVE]
        cols += [wh, zeros(64)] if h < 2 else [zeros(64), wh]
    kq = ATT_HEADS * ATT_HEAD_DIM
    for g in range(ATT_KV_HEADS):
        cols.append(w_att[:, kq + g * ATT_HEAD_DIM:kq + (g + 1) * ATT_HEAD_DIM][:, _DEINTERLEAVE])
    cols.append(w_att[:, kq + ATT_KV_HEADS * ATT_HEAD_DIM:])
    return jnp.concatenate(cols, axis=1).astype(BF16)


def _rope_tables(seq_len):
    rows = seq_len // GRID_W
    row = jnp.repeat(jnp.arange(rows, dtype=F32), GRID_W)
    col = jnp.tile(jnp.arange(GRID_W, dtype=F32), rows)
    inv_freq = 1.0 / (ROPE_THETA ** (jnp.arange(0, ROPE_AXIS_DIM, 2, dtype=F32) / ROPE_AXIS_DIM))
    ang = jnp.concatenate([row[:, None] * inv_freq, col[:, None] * inv_freq], axis=-1)
    cos, sin = jnp.cos(ang), jnp.sin(ang)
    return jnp.tile(cos, (1, 4)), jnp.tile(jnp.concatenate([-sin, sin], axis=-1), (1, 2))


def _layer_params(i, s):
    p = {name: val[i] for name, val in s.items()}
    p['w_in_p'] = _pack_w_in(p['w_in'])
    p['w_out_b'] = p['w_out'].astype(BF16)
    p['w_up_b'] = p['w_up'].astype(BF16)
    p['w_down_b'] = p['w_down'].astype(BF16)
    p['ssd_dt_bias_x'] = jnp.repeat(p['ssd_dt_bias'].reshape(-1), SSD_HEAD_DIM).reshape(1, SSD_DTX)
    p['ssd_a_x'] = jnp.repeat(-jnp.exp(p['ssd_A_log']).reshape(-1), SSD_HEAD_DIM).reshape(1, SSD_DTX)
    p['ssd_d_x'] = jnp.repeat(p['ssd_D'], SSD_HEAD_DIM).reshape(1, SSD_INNER)
    wg = jnp.zeros((GLA_PAD - GLA_LR0, 2 * GLA_QK), F32)
    wg = wg.at[0:GLA_RANK, 0:GLA_QK].set(p['gla_gk_w'][0])
    wg = wg.at[GLA_RANK:2 * GLA_RANK, GLA_QK:].set(p['gla_gk_w'][1])
    p['gla_wg'] = wg
    p['gla_bg'] = p['gla_gk_b'].reshape(1, 2 * GLA_QK)
    p['gla_norm_g_x'] = jnp.tile(p['gla_norm_g'], GLA_HEADS).reshape(1, GLA_V)
    p['att_gq'] = jnp.tile(p['att_q_norm_g'][_DEINTERLEAVE], 2).reshape(1, LANES)
    p['att_gk'] = jnp.tile(p['att_k_norm_g'][_DEINTERLEAVE], 2).reshape(1, LANES)
    return p


def _layer(x, tables, p):
    b, L, _ = x.shape
    x2d = x.reshape(b * L, D_MODEL)
    u_hy, u_ssd, u_gla, u_att = in_proj(x2d, p['ln1_g'], p['w_in_p'])

    hf = _hyena_filter_freq(L, p['hy_ffn_w1'], p['hy_ffn_b1'], p['hy_ffn_w2'], p['hy_ffn_b2'],
                            p['hy_ffn_w3'], p['hy_sin_freq'])
    y_hy = hyena_mixer(u_hy, b, L, p, hf)
    y_ssd = ssd_mixer(u_ssd, b, L, p['ssd_conv_w'], p['ssd_conv_b'], p['ssd_dt_bias_x'], p['ssd_a_x'],
                      p['ssd_d_x'], p['ssd_norm_g'])
    y_gla = gla_mixer(u_gla, b, L, p['gla_wg'], p['gla_bg'], p['gla_norm_g_x'])
    q, k, vt = att_prep(u_att, tables[0], tables[1], p['att_gq'], p['att_gk'], b, L)
    y_att = flash_attention(q.reshape(b, L, ATT_Q_PAD), k.reshape(b, L, LANES), vt, p['att_out_g'])

    y = out_ffn(x2d, (y_hy, y_ssd, y_gla, y_att.reshape(b * L, GROUP_W)), p['w_out_b'], p['ln2_g'],
                p['w_up_b'], p['w_down_b'])
    return y.reshape(b, L, D_MODEL)


def kernel(x_prompt, x_sample, ln1_g, w_in, hy_conv_w, hy_conv_b, hy_ffn_w1, hy_ffn_b1, hy_ffn_w2, hy_ffn_b2, hy_ffn_w3, hy_sin_freq, hy_skip, hy_out_g, ssd_conv_w, ssd_conv_b, ssd_A_log, ssd_dt_bias, ssd_D, ssd_norm_g, gla_gk_w, gla_gk_b, gla_norm_g, att_q_norm_g, att_k_norm_g, att_out_g, w_out, ln2_g, w_up, w_down):
    stacked = dict(ln1_g=ln1_g, w_in=w_in, hy_conv_w=hy_conv_w, hy_conv_b=hy_conv_b, hy_ffn_w1=hy_ffn_w1,
                   hy_ffn_b1=hy_ffn_b1, hy_ffn_w2=hy_ffn_w2, hy_ffn_b2=hy_ffn_b2, hy_ffn_w3=hy_ffn_w3,
                   hy_sin_freq=hy_sin_freq, hy_skip=hy_skip, hy_out_g=hy_out_g, ssd_conv_w=ssd_conv_w,
                   ssd_conv_b=ssd_conv_b, ssd_A_log=ssd_A_log, ssd_dt_bias=ssd_dt_bias, ssd_D=ssd_D,
                   ssd_norm_g=ssd_norm_g, gla_gk_w=gla_gk_w, gla_gk_b=gla_gk_b, gla_norm_g=gla_norm_g,
                   att_q_norm_g=att_q_norm_g, att_k_norm_g=att_k_norm_g, att_out_g=att_out_g,
                   w_out=w_out, ln2_g=ln2_g, w_up=w_up, w_down=w_down)
    tables_p = _rope_tables(x_prompt.shape[1])
    tables_s = _rope_tables(x_sample.shape[1])
    y_prompt, y_sample = x_prompt, x_sample
    for i in range(DEPTH):
        p = _layer_params(i, stacked)
        y_prompt = _layer(y_prompt, tables_p, p)
        y_sample = _layer(y_sample, tables_s, p)
    return (y_prompt, y_sample)
```

```python
import functools
import math

import jax
import jax.numpy as jnp
import numpy as np
from jax import lax
from jax.experimental import pallas as pl
from jax.experimental.pallas import tpu as pltpu

D_MODEL = 1024
DEPTH = 4
GRID_W = 64
EPS = 1e-6
GROUP_W = 256
D_FF = 4096

HY_C = 256
HY_ORDER = 2
HY_BANDS = 8
HY_MIN_DECAY = math.log(1e-2) / 1.5
HY_MAX_DECAY = math.log(1e-2) / 0.3
HY_IN = 768

SSD_HEAD_DIM = 64
SSD_HEADS = 4
SSD_GROUPS = 2
SSD_STATE = 128
SSD_CHUNK = 128
SSD_INNER = 256
SSD_BC = 256
SSD_XBC = 768
SSD_IN = 1032

GLA_HEADS = 4
GLA_DV = 64
GLA_DK = 32
GLA_RANK = 16
GLA_NORMALIZER = 16.0
GLA_CHUNK = 64
GLA_QK = 128
GLA_V = 256
GLA_IN = 800

ATT_HEAD_DIM = 64
ATT_HEADS = 4
ATT_KV_HEADS = 2
ROPE_THETA = 10000.0
ROPE_AXIS_DIM = 32
ATT_IN = 512

LANES = 128
SSD_DTX = 2 * SSD_HEADS * SSD_HEAD_DIM
SSD_PAD = SSD_INNER + SSD_XBC + SSD_DTX
GLA_PAD = 896
GLA_LR0 = 2 * GLA_QK + 2 * GLA_V
ATT_Q_PAD = ATT_HEADS * LANES
ATT_PAD = ATT_Q_PAD + 2 * LANES
ATT_VT_ROWS = ATT_HEAD_DIM + 16
ATT_ROW_CHUNK = 64
IN_PAD = HY_IN + SSD_PAD + GLA_PAD + ATT_PAD

VMEM_LIMIT = 48 * 1024 * 1024
F32 = jnp.float32
BF16 = jnp.bfloat16
HIGHEST = lax.Precision.HIGHEST
NT = (((1,), (1,)), ((), ()))
TN = (((0,), (0,)), ((), ()))


def _cparams(n_grid):
    return pltpu.CompilerParams(dimension_semantics=("arbitrary",) * n_grid,
                                vmem_limit_bytes=VMEM_LIMIT)


def _softplus(x):
    return jnp.maximum(x, 0.0) + jnp.log1p(jnp.exp(-jnp.abs(x)))


def _shift_rows(x, prev_row, next_row):
    n = x.shape[0]
    rid = lax.broadcasted_iota(jnp.int32, (8, x.shape[1]), 0)
    dn = pltpu.roll(x, 1, axis=0)
    up = pltpu.roll(x, n - 1, axis=0)
    xm = jnp.concatenate([jnp.where(rid == 0, prev_row, dn[0:8]), dn[8:]], axis=0)
    xp = jnp.concatenate([up[:n - 8], jnp.where(rid == 7, next_row, up[n - 8:])], axis=0)
    return xm, xp


def _silu(x):
    return x * (0.5 * jnp.tanh(0.5 * x) + 0.5)


def _in_proj_kernel(x_ref, g_ref, w_ref, hy_ref, ssd_ref, gla_ref, att_ref):
    x = x_ref[...]
    ms = jnp.mean(x * x, axis=-1, keepdims=True)
    h = (x * lax.rsqrt(ms + EPS) * g_ref[...]).astype(BF16)
    c0 = 0
    for ref in (hy_ref, ssd_ref, gla_ref, att_ref):
        n = ref.shape[-1]
        ref[...] = jnp.dot(h, w_ref[:, c0:c0 + n], preferred_element_type=F32)
        c0 += n


def in_proj(x2d, g, w_p, tm=512):
    t = x2d.shape[0]
    widths = (HY_IN, SSD_PAD, GLA_PAD, ATT_PAD)
    return pl.pallas_call(
        _in_proj_kernel,
        grid=(t // tm,),
        in_specs=[pl.BlockSpec((tm, D_MODEL), lambda i: (i, 0)),
                  pl.BlockSpec((1, D_MODEL), lambda i: (0, 0)),
                  pl.BlockSpec((D_MODEL, IN_PAD), lambda i: (0, 0), pipeline_mode=pl.Buffered(1))],
        out_specs=[pl.BlockSpec((tm, n), lambda i: (i, 0)) for n in widths],
        out_shape=[jax.ShapeDtypeStruct((t, n), F32) for n in widths],
        compiler_params=_cparams(1),
        name="in_proj",
    )(x2d, g.reshape(1, D_MODEL), w_p)


def _rope(x, cos_t, sin_t):
    lane = lax.broadcasted_iota(jnp.int32, x.shape, 1)
    nxt = pltpu.roll(x, LANES - 32, axis=1)
    prv = pltpu.roll(x, 32, axis=1)
    sw = jnp.where((lane % 64) < 32, nxt, prv)
    return x * cos_t + sw * sin_t


def _att_prep_kernel(u_ref, cos_ref, sin_ref, gq_ref, gk_ref, q_ref, k_ref, vt_ref):
    cos_t = cos_ref[...]
    sin_t = sin_ref[...]
    inv = 1.0 / ATT_HEAD_DIM
    qscale = (ATT_HEAD_DIM ** -0.5) * math.log2(math.e)
    for h in range(ATT_HEADS):
        x = u_ref[:, h * LANES:(h + 1) * LANES]
        ms = jnp.sum(x * x, axis=-1, keepdims=True) * inv
        xn = x * lax.rsqrt(ms + EPS) * gq_ref[...]
        q_ref[:, h * LANES:(h + 1) * LANES] = (_rope(xn, cos_t, sin_t) * qscale).astype(BF16)
    k = u_ref[:, ATT_Q_PAD:ATT_Q_PAD + LANES]
    lane = lax.broadcasted_iota(jnp.int32, k.shape, 1)
    k2 = k * k
    s_all = jnp.sum(k2, axis=-1, keepdims=True)
    s_lo = jnp.sum(jnp.where(lane < 64, k2, 0.0), axis=-1, keepdims=True)
    ms = jnp.where(lane < 64, s_lo, s_all - s_lo) * inv
    kn = k * lax.rsqrt(ms + EPS) * gk_ref[...]
    k_ref[...] = _rope(kn, cos_t, sin_t).astype(BF16)
    vt = u_ref[:, ATT_Q_PAD + LANES:ATT_PAD].T
    rid = lax.broadcasted_iota(jnp.int32, (ATT_VT_ROWS - ATT_HEAD_DIM, vt.shape[1]), 0)
    tail = jnp.where(rid == 0, 1.0, 0.0)
    for g in range(ATT_KV_HEADS):
        vt_ref[0, g] = jnp.concatenate(
            [vt[g * ATT_HEAD_DIM:(g + 1) * ATT_HEAD_DIM], tail], axis=0).astype(BF16)


def att_prep(u_att, cos_t, sin_t, gq, gk, batch, seq_len, tm=512):
    t = u_att.shape[0]
    nl = seq_len // tm
    return pl.pallas_call(
        _att_prep_kernel,
        grid=(t // tm,),
        in_specs=[pl.BlockSpec((tm, ATT_PAD), lambda i: (i, 0)),
                  pl.BlockSpec((tm, LANES), lambda i: (i % nl, 0)),
                  pl.BlockSpec((tm, LANES), lambda i: (i % nl, 0)),
                  pl.BlockSpec((1, LANES), lambda i: (0, 0)),
                  pl.BlockSpec((1, LANES), lambda i: (0, 0))],
        out_specs=[pl.BlockSpec((tm, ATT_Q_PAD), lambda i: (i, 0)),
                   pl.BlockSpec((tm, LANES), lambda i: (i, 0)),
                   pl.BlockSpec((1, ATT_KV_HEADS, ATT_VT_ROWS, tm), lambda i: (i // nl, 0, 0, i % nl))],
        out_shape=[jax.ShapeDtypeStruct((t, ATT_Q_PAD), BF16),
                   jax.ShapeDtypeStruct((t, LANES), BF16),
                   jax.ShapeDtypeStruct((batch, ATT_KV_HEADS, ATT_VT_ROWS, seq_len), BF16)],
        compiler_params=_cparams(1),
        name="att_prep",
    )(u_att, cos_t, sin_t, gq, gk)


def _flash_kernel(q_ref, k_ref, vt_ref, g_ref, o_ref, m_ref, acc_ref, st_ref, p_ref, *, tk, nk):
    m_ref[...] = jnp.full(m_ref.shape, -jnp.inf, F32)
    acc_ref[...] = jnp.zeros(acc_ref.shape, F32)

    def scores(j, slot, h):
        off = pl.multiple_of(j * tk, tk)
        k = k_ref[0, pl.ds(off, tk), :]
        q = q_ref[0, :, h * LANES:(h + 1) * LANES]
        st_ref[slot, h] = lax.dot_general(k, q, NT, preferred_element_type=F32)

    def consume(j, slot, h):
        off = pl.multiple_of(j * tk, tk)
        rows = [slice(r0, r0 + ATT_ROW_CHUNK) for r0 in range(0, tk, ATT_ROW_CHUNK)]
        m_prev = m_ref[h]
        m_new = m_prev
        for r in rows:
            m_new = jnp.maximum(m_new, jnp.max(st_ref[slot, h, r], axis=0, keepdims=True))
        alpha = jnp.exp2(m_prev - m_new)
        for r in rows:
            p_ref[h, r] = jnp.exp2((st_ref[slot, h, r] - m_new).astype(BF16))
        vt = vt_ref[0, h // 2, :, pl.ds(off, tk)]
        acc_ref[h] = alpha * acc_ref[h] + jnp.dot(vt, p_ref[h], preferred_element_type=F32)
        m_ref[h] = m_new

    for h in range(ATT_HEADS):
        scores(0, 0, h)

    def body(jj, carry):
        j = 2 * jj
        j2 = jnp.minimum(j + 2, nk - 1)
        for h in range(ATT_HEADS):
            scores(j + 1, 1, h)
            consume(j, 0, h)
        for h in range(ATT_HEADS):
            scores(j2, 0, h)
            consume(j + 1, 1, h)
        return carry

    lax.fori_loop(0, nk // 2, body, 0)
    ot = jnp.concatenate([acc_ref[h, 0:ATT_HEAD_DIM] / acc_ref[h, ATT_HEAD_DIM:ATT_HEAD_DIM + 1]
                          for h in range(ATT_HEADS)], axis=0)
    o = ot.T
    ms = jnp.mean(o * o, axis=-1, keepdims=True)
    o_ref[0] = (o * lax.rsqrt(ms + EPS) * g_ref[...]).astype(BF16)


def flash_attention(q, k, vt, g_out, tq=512, tk=512):
    b, seq_len, _ = q.shape
    return pl.pallas_call(
        functools.partial(_flash_kernel, tk=tk, nk=seq_len // tk),
        grid=(b, seq_len // tq),
        in_specs=[pl.BlockSpec((1, tq, ATT_Q_PAD), lambda bi, i: (bi, i, 0)),
                  pl.BlockSpec((1, seq_len, LANES), lambda bi, i: (bi, 0, 0)),
                  pl.BlockSpec((1, ATT_KV_HEADS, ATT_VT_ROWS, seq_len), lambda bi, i: (bi, 0, 0, 0)),
                  pl.BlockSpec((1, GROUP_W), lambda bi, i: (0, 0))],
        out_specs=pl.BlockSpec((1, tq, GROUP_W), lambda bi, i: (bi, i, 0)),
        out_shape=jax.ShapeDtypeStruct((b, seq_len, GROUP_W), BF16),
        scratch_shapes=[pltpu.VMEM((ATT_HEADS, 1, tq), F32),
                        pltpu.VMEM((ATT_HEADS, ATT_VT_ROWS, tq), F32),
                        pltpu.VMEM((2, ATT_HEADS, tk, tq), F32),
                        pltpu.VMEM((ATT_HEADS, tk, tq), BF16)],
        compiler_params=_cparams(2),
        name="flash_attention",
    )(q, k, vt, g_out.reshape(1, GROUP_W))


def _tri(n, upper, block=None):
    ii = lax.broadcasted_iota(jnp.int32, (n, n), 0)
    jj = lax.broadcasted_iota(jnp.int32, (n, n), 1)
    m = (jj >= ii) if upper else (jj <= ii)
    if block is not None:
        m = m & ((ii // block) == (jj // block))
    return m


def _tri_dot(t, x):
    hi = x.astype(BF16)
    r1 = x - hi.astype(F32)
    mid = r1.astype(BF16)
    lo = (r1 - mid.astype(F32)).astype(BF16)
    n = x.shape[1]
    y = jnp.dot(t, jnp.concatenate([hi, mid, lo], axis=1), preferred_element_type=F32)
    return (y[:, 0:n] + y[:, n:2 * n]) + y[:, 2 * n:]


def _ssd_fwd_kernel(u_ref, up_ref, un_ref, cw_ref, cb_ref, dtb_ref, ax_ref, dx_ref,
                    yp_ref, cm_ref, bm_ref, wxb_ref, eb_ref, s_ref, *, tm):
    Q = SSD_CHUNK
    i = pl.program_id(1)
    nb = pl.num_programs(1)

    @pl.when(i == 0)
    def _():
        s_ref[...] = jnp.zeros(s_ref.shape, F32)

    x = u_ref[:, SSD_INNER:SSD_INNER + SSD_XBC]
    prev_row = jnp.where(i > 0, up_ref[7:8, SSD_INNER:SSD_INNER + SSD_XBC], 0.0)
    next_row = jnp.where(i < nb - 1, un_ref[0:1, SSD_INNER:SSD_INNER + SSD_XBC], 0.0)
    xm, xp = _shift_rows(x, prev_row, next_row)
    conv = xm * cw_ref[0:1] + x * cw_ref[1:2] + xp * cw_ref[2:3] + cb_ref[...]
    xbc = _silu(conv)
    xs = xbc[:, 0:SSD_INNER]
    bm = xbc[:, SSD_INNER:SSD_INNER + SSD_BC].astype(BF16)
    cm = xbc[:, SSD_INNER + SSD_BC:].astype(BF16)
    bm_ref[...] = bm
    cm_ref[...] = cm
    dt = _softplus(u_ref[:, SSD_INNER + SSD_XBC:] + dtb_ref[...])
    da = dt * ax_ref[...]

    tl = _tri(Q, False).astype(BF16)
    tu = _tri(Q, True).astype(BF16)
    ii = lax.broadcasted_iota(jnp.int32, (Q, Q), 0)
    jj = lax.broadcasted_iota(jnp.int32, (Q, Q), 1)
    lo_half = lax.broadcasted_iota(jnp.int32, (Q, LANES), 1) < SSD_HEAD_DIM
    ninf = jnp.float32(-jnp.inf)
    chunks = [slice(c * Q, (c + 1) * Q) for c in range(tm // Q)]
    groups = [slice(g * LANES, (g + 1) * LANES) for g in range(SSD_GROUPS)]
    dtf, dtb = dt[:, 0:SSD_INNER], dt[:, SSD_INNER:]
    cum = jnp.concatenate([_tri_dot(tl, da[r, 0:SSD_INNER]) for r in chunks], axis=0)
    rc = jnp.concatenate([_tri_dot(tu, da[r, SSD_INNER:]) for r in chunks], axis=0)
    e_cum = jnp.exp(cum)
    eb_ref[...] = jnp.exp(rc)
    xs_b = xs.astype(BF16)
    wxf = []
    for r in chunks:
        wxf.append((xs[r] * jnp.exp(cum[r][Q - 1:Q] - cum[r]) * dtf[r]).astype(BF16))
        wxb_ref[r] = (xs[r] * jnp.exp(rc[r][0:1] - rc[r]) * dtb[r]).astype(BF16)
    yd = []
    for r in chunks:
        for ln in groups:
            cumg, rcg = cum[r, ln], rc[r, ln]
            cb = lax.dot_general(cm[r, ln], bm[r, ln], NT, preferred_element_type=F32)
            cum_t, rc_t, dtf_t, dtb_t = cumg.T, rcg.T, dtf[r, ln].T, dtb[r, ln].T
            cum_sw, rc_sw = pltpu.roll(cumg, SSD_HEAD_DIM, axis=1), pltpu.roll(rcg, SSD_HEAD_DIM, axis=1)
            halves = []
            for hh in range(2):
                row = slice(hh * SSD_HEAD_DIM, hh * SSD_HEAD_DIM + 1)
                col_c = jnp.where(lo_half, cumg, cum_sw) if hh == 0 else jnp.where(lo_half, cum_sw, cumg)
                col_r = jnp.where(lo_half, rcg, rc_sw) if hh == 0 else jnp.where(lo_half, rc_sw, rcg)
                ef = jnp.exp(jnp.where(jj <= ii, col_c - cum_t[row], ninf)) * dtf_t[row]
                eb = jnp.exp(jnp.where(jj >= ii, col_r - rc_t[row], ninf)) * dtb_t[row]
                w = (cb * (ef + eb)).astype(BF16)
                halves.append(jnp.dot(w, xs_b[r, ln], preferred_element_type=F32))
            yd.append(jnp.where(lo_half, halves[0], halves[1]) + dx_ref[:, ln] * xs[r, ln])
    state = [s_ref[g] for g in range(SSD_GROUPS)]
    for c, r in enumerate(chunks):
        for g, ln in enumerate(groups):
            yp_ref[r, ln] = yd[c * SSD_GROUPS + g] + jnp.dot(
                cm[r, ln], state[g].astype(BF16), preferred_element_type=F32) * e_cum[r, ln]
            state[g] = state[g] * e_cum[r, ln][Q - 1:Q] + lax.dot_general(
                bm[r, ln], wxf[c][:, ln], TN, preferred_element_type=F32)
    for g in range(SSD_GROUPS):
        s_ref[g] = state[g]


def _ssd_bwd_kernel(yp_ref, cm_ref, bm_ref, wxb_ref, eb_ref, z_ref, g_ref, o_ref, s_ref, y_ref, *, tm):
    Q = SSD_CHUNK

    @pl.when(pl.program_id(1) == 0)
    def _():
        s_ref[...] = jnp.zeros(s_ref.shape, F32)

    state = [s_ref[g] for g in range(SSD_GROUPS)]
    for c in reversed(range(tm // Q)):
        r = slice(c * Q, (c + 1) * Q)
        for g in range(SSD_GROUPS):
            ln = slice(g * LANES, (g + 1) * LANES)
            ebg = eb_ref[r, ln]
            y_ref[r, ln] = yp_ref[r, ln] + jnp.dot(cm_ref[r, ln], state[g].astype(BF16),
                                                   preferred_element_type=F32) * ebg
            state[g] = state[g] * ebg[0:1] + lax.dot_general(bm_ref[r, ln], wxb_ref[r, ln], TN,
                                                             preferred_element_type=F32)
    for g in range(SSD_GROUPS):
        s_ref[g] = state[g]
    y = y_ref[...] * _silu(z_ref[...])
    ms = jnp.mean(y * y, axis=-1, keepdims=True)
    o_ref[...] = (y * lax.rsqrt(ms + EPS) * g_ref[...]).astype(BF16)


def ssd_mixer(u_ssd, batch, seq_len, conv_w, conv_b, dt_bias_x, a_x, d_x, norm_g, tm=512, tm_bwd=1024):
    t = u_ssd.shape[0]
    nb = seq_len // tm
    hb = tm // 8
    last_halo = t // 8 - 1
    row = lambda b, i: (b * nb + i, 0)
    const = lambda b, i: (0, 0)
    outs = pl.pallas_call(
        functools.partial(_ssd_fwd_kernel, tm=tm),
        grid=(batch, nb),
        in_specs=[pl.BlockSpec((tm, SSD_PAD), row),
                  pl.BlockSpec((8, SSD_PAD), lambda b, i: (jnp.maximum((b * nb + i) * hb - 1, 0), 0)),
                  pl.BlockSpec((8, SSD_PAD), lambda b, i: (jnp.minimum((b * nb + i + 1) * hb, last_halo), 0)),
                  pl.BlockSpec((3, SSD_XBC), const),
                  pl.BlockSpec((1, SSD_XBC), const),
                  pl.BlockSpec((1, SSD_DTX), const),
                  pl.BlockSpec((1, SSD_DTX), const),
                  pl.BlockSpec((1, SSD_INNER), const)],
        out_specs=[pl.BlockSpec((tm, SSD_INNER), row)] * 5,
        out_shape=[jax.ShapeDtypeStruct((t, SSD_INNER), F32),
                   jax.ShapeDtypeStruct((t, SSD_INNER), BF16),
                   jax.ShapeDtypeStruct((t, SSD_INNER), BF16),
                   jax.ShapeDtypeStruct((t, SSD_INNER), BF16),
                   jax.ShapeDtypeStruct((t, SSD_INNER), F32)],
        scratch_shapes=[pltpu.VMEM((SSD_GROUPS, SSD_STATE, LANES), F32)],
        compiler_params=_cparams(2),
        name="ssd_fwd",
    )(u_ssd, u_ssd, u_ssd, conv_w, conv_b.reshape(1, SSD_XBC), dt_bias_x, a_x, d_x)
    nb = seq_len // tm_bwd
    rrow = lambda b, i: (b * nb + nb - 1 - i, 0)
    return pl.pallas_call(
        functools.partial(_ssd_bwd_kernel, tm=tm_bwd),
        grid=(batch, nb),
        in_specs=[pl.BlockSpec((tm_bwd, SSD_INNER), rrow)] * 6 + [pl.BlockSpec((1, SSD_INNER), const)],
        out_specs=pl.BlockSpec((tm_bwd, SSD_INNER), rrow),
        out_shape=jax.ShapeDtypeStruct((t, SSD_INNER), BF16),
        scratch_shapes=[pltpu.VMEM((SSD_GROUPS, SSD_STATE, LANES), F32),
                        pltpu.VMEM((tm_bwd, SSD_INNER), F32)],
        compiler_params=_cparams(2),
        name="ssd_bwd",
    )(*outs, u_ssd, norm_g.reshape(1, SSD_INNER))


def _gla_fwd_kernel(u_ref, wg_ref, bg_ref, op_ref, qb_ref, kb_ref, db_ref, st_ref, *, tm):
    Q = GLA_CHUNK
    P = 2 * Q

    @pl.when(pl.program_id(1) == 0)
    def _():
        st_ref[...] = jnp.zeros(st_ref.shape, F32)

    pre = jnp.dot(u_ref[:, GLA_LR0:GLA_PAD], wg_ref[...], precision=HIGHEST,
                  preferred_element_type=F32) + bg_ref[...]
    gk = (jnp.minimum(pre, 0.0) - jnp.log1p(jnp.exp(-jnp.abs(pre)))) * (1.0 / GLA_NORMALIZER)

    mask_f = _tri(P, False, Q)
    mask_b = _tri(P, True, Q)
    tl = mask_f.astype(BF16)
    tu = mask_b.astype(BF16)
    lane_k = lax.broadcasted_iota(jnp.int32, (P, GLA_QK), 1) // GLA_DK
    lane_v = lax.broadcasted_iota(jnp.int32, (P, GLA_V), 1) // GLA_DV
    bd_t = (lax.broadcasted_iota(jnp.int32, (GLA_V, GLA_QK), 0) // GLA_DV ==
            lax.broadcasted_iota(jnp.int32, (GLA_V, GLA_QK), 1) // GLA_DK)

    pairs = [slice(c * P, (c + 1) * P) for c in range(tm // P)]
    heads = range(GLA_HEADS)
    gf = jnp.concatenate([_tri_dot(tl, gk[r, 0:GLA_QK]) for r in pairs], axis=0)
    gb = jnp.concatenate([_tri_dot(tu, gk[r, GLA_QK:]) for r in pairs], axis=0)
    qs = u_ref[:, 0:GLA_QK] * (GLA_DK ** -0.5)
    k = u_ref[:, GLA_QK:2 * GLA_QK]
    v = u_ref[:, 2 * GLA_QK:2 * GLA_QK + GLA_V]
    vb = v.astype(BF16)
    q_f = qs * jnp.exp(gf)
    k_f = (k * jnp.exp(-gf)).astype(BF16)
    q_b = qs * jnp.exp(gb)
    k_b = (k * jnp.exp(-gb)).astype(BF16)
    q_fb = q_f.astype(BF16)
    qb_ref[...] = q_b.astype(BF16)
    db_ref[...] = jnp.exp(gb)
    af, ab = [], []
    for r in pairs:
        qf_h = jnp.concatenate([jnp.where(lane_k == h, q_f[r], 0.0) for h in heads], axis=0).astype(BF16)
        qb_h = jnp.concatenate([jnp.where(lane_k == h, q_b[r], 0.0) for h in heads], axis=0).astype(BF16)
        af.append(lax.dot_general(qf_h, k_f[r], NT, preferred_element_type=F32))
        ab.append(lax.dot_general(qb_h, k_b[r], NT, preferred_element_type=F32))
    o = []
    for c, r in enumerate(pairs):
        att = jnp.concatenate([jnp.where(mask_f, af[c][h * P:(h + 1) * P], 0.0) +
                               jnp.where(mask_b, ab[c][h * P:(h + 1) * P], 0.0) for h in heads], axis=1)
        v_h = jnp.concatenate([jnp.where(lane_v == h, v[r], 0.0) for h in heads], axis=0).astype(BF16)
        o.append(jnp.dot(att.astype(BF16), v_h, preferred_element_type=F32))
    st = st_ref[...]
    for c in range(tm // Q):
        ro = slice(c * Q, (c + 1) * Q)
        rr = slice((c % 2) * Q, (c % 2 + 1) * Q)
        g_last = gf[c * Q + Q - 1:c * Q + Q]
        k_end = (k[ro] * jnp.exp(g_last - gf[ro])).astype(BF16)
        op_ref[ro] = o[c // 2][rr] + lax.dot_general(q_fb[ro], st.astype(BF16), NT, preferred_element_type=F32)
        upd = lax.dot_general(vb[ro], k_end, TN, preferred_element_type=F32)
        st = st * jnp.exp(g_last) + jnp.where(bd_t, upd, 0.0)
        kb_ref[ro] = (k[ro] * jnp.exp(gb[c * Q:c * Q + 1] - gb[ro])).astype(BF16)
    st_ref[...] = st


def _gla_bwd_kernel(op_ref, qb_ref, kb_ref, db_ref, v_ref, go_ref, g_ref, o_ref, st_ref, y_ref, *, tm):
    Q = GLA_CHUNK

    @pl.when(pl.program_id(1) == 0)
    def _():
        st_ref[...] = jnp.zeros(st_ref.shape, F32)

    bd_t = (lax.broadcasted_iota(jnp.int32, (GLA_V, GLA_QK), 0) // GLA_DV ==
            lax.broadcasted_iota(jnp.int32, (GLA_V, GLA_QK), 1) // GLA_DK)
    st = st_ref[...]
    for c in reversed(range(tm // Q)):
        r = slice(c * Q, (c + 1) * Q)
        y_ref[r] = op_ref[r] + lax.dot_general(qb_ref[r], st.astype(BF16), NT, preferred_element_type=F32)
        upd = lax.dot_general(v_ref[r].astype(BF16), kb_ref[r], TN, preferred_element_type=F32)
        st = st * db_ref[c * Q:c * Q + 1] + jnp.where(bd_t, upd, 0.0)
    st_ref[...] = st
    y = y_ref[...]
    seg = (lax.broadcasted_iota(jnp.int32, (GLA_V, GLA_V), 0) // GLA_DV ==
           lax.broadcasted_iota(jnp.int32, (GLA_V, GLA_V), 1) // GLA_DV).astype(F32)
    ms = jnp.dot(y * y, seg, precision=HIGHEST, preferred_element_type=F32) * (1.0 / GLA_DV)
    o_ref[...] = (y * lax.rsqrt(ms + EPS) * g_ref[...] * _silu(go_ref[...])).astype(BF16)


def gla_mixer(u_gla, batch, seq_len, wg, bg, norm_g_x, tm=512, tm_bwd=1024):
    t = u_gla.shape[0]
    nb = seq_len // tm
    row = lambda b, i: (b * nb + i, 0)
    const = lambda b, i: (0, 0)
    outs = pl.pallas_call(
        functools.partial(_gla_fwd_kernel, tm=tm),
        grid=(batch, nb),
        in_specs=[pl.BlockSpec((tm, GLA_PAD), row),
                  pl.BlockSpec((GLA_PAD - GLA_LR0, 2 * GLA_QK), const),
                  pl.BlockSpec((1, 2 * GLA_QK), const)],
        out_specs=[pl.BlockSpec((tm, GLA_V), row), pl.BlockSpec((tm, GLA_QK), row),
                   pl.BlockSpec((tm, GLA_QK), row), pl.BlockSpec((tm, GLA_QK), row)],
        out_shape=[jax.ShapeDtypeStruct((t, GLA_V), F32),
                   jax.ShapeDtypeStruct((t, GLA_QK), BF16),
                   jax.ShapeDtypeStruct((t, GLA_QK), BF16),
                   jax.ShapeDtypeStruct((t, GLA_QK), F32)],
        scratch_shapes=[pltpu.VMEM((GLA_V, GLA_QK), F32)],
        compiler_params=_cparams(2),
        name="gla_fwd",
    )(u_gla, wg, bg)
    nb = seq_len // tm_bwd
    rrow = lambda b, i: (b * nb + nb - 1 - i, 0)
    return pl.pallas_call(
        functools.partial(_gla_bwd_kernel, tm=tm_bwd),
        grid=(batch, nb),
        in_specs=[pl.BlockSpec((tm_bwd, GLA_V), rrow), pl.BlockSpec((tm_bwd, GLA_QK), rrow),
                  pl.BlockSpec((tm_bwd, GLA_QK), rrow), pl.BlockSpec((tm_bwd, GLA_QK), rrow),
                  pl.BlockSpec((tm_bwd, GLA_V), lambda b, i: (b * nb + nb - 1 - i, 1)),
                  pl.BlockSpec((tm_bwd, GLA_V), lambda b, i: (b * nb + nb - 1 - i, 2)),
                  pl.BlockSpec((1, GLA_V), const)],
        out_specs=pl.BlockSpec((tm_bwd, GLA_V), rrow),
        out_shape=jax.ShapeDtypeStruct((t, GLA_V), BF16),
        scratch_shapes=[pltpu.VMEM((GLA_V, GLA_QK), F32), pltpu.VMEM((tm_bwd, GLA_V), F32)],
        compiler_params=_cparams(2),
        name="gla_bwd",
    )(*outs, u_gla, u_gla, norm_g_x)


def _out_ffn_kernel(x_ref, y0_ref, y1_ref, y2_ref, y3_ref, wo_ref, g2_ref, wu_ref, wd_ref, o_ref, *, tf):
    mixed = jnp.concatenate([y0_ref[...], y1_ref[...], y2_ref[...], y3_ref[...]], axis=1)
    x1 = x_ref[...] + jnp.dot(mixed, wo_ref[...], preferred_element_type=F32)
    ms = jnp.mean(x1 * x1, axis=-1, keepdims=True)
    h2 = (x1 * lax.rsqrt(ms + EPS) * g2_ref[...]).astype(BF16)
    acc = x1
    for c in range(D_FF // tf):
        up = jnp.dot(h2, wu_ref[:, c * tf:(c + 1) * tf], preferred_element_type=F32)
        a = jnp.square(jnp.maximum(up, 0.0)).astype(BF16)
        acc = acc + jnp.dot(a, wd_ref[c * tf:(c + 1) * tf, :], preferred_element_type=F32)
    o_ref[...] = acc


def out_ffn(x2d, ys, wo, g2, wu, wd, tm=512, tf=1024):
    t = x2d.shape[0]
    const = lambda i: (0, 0)
    single = pl.Buffered(1)
    return pl.pallas_call(
        functools.partial(_out_ffn_kernel, tf=tf),
        grid=(t // tm,),
        in_specs=[pl.BlockSpec((tm, D_MODEL), lambda i: (i, 0))] +
                 [pl.BlockSpec((tm, GROUP_W), lambda i: (i, 0))] * 4 +
                 [pl.BlockSpec((D_MODEL, D_MODEL), const, pipeline_mode=single),
                  pl.BlockSpec((1, D_MODEL), const),
                  pl.BlockSpec((D_MODEL, D_FF), const, pipeline_mode=single),
                  pl.BlockSpec((D_FF, D_MODEL), const, pipeline_mode=single)],
        out_specs=pl.BlockSpec((tm, D_MODEL), lambda i: (i, 0)),
        out_shape=jax.ShapeDtypeStruct((t, D_MODEL), F32),
        compiler_params=_cparams(1),
        name="out_ffn",
    )(x2d, *ys, wo, g2.reshape(1, D_MODEL), wu, wd)


HY_N1 = 128
HY_ROWS = HY_N1 // 2


def _hy_pre_kernel(u_ref, up_ref, un_ref, cw_ref, cb_ref, o_ref, *, tm):
    i = pl.program_id(1)
    nb = pl.num_programs(1)
    x = u_ref[...]
    prev_row = jnp.where(i > 0, up_ref[7:8, :], 0.0)
    next_row = jnp.where(i < nb - 1, un_ref[0:1, :], 0.0)
    xm, xp = _shift_rows(x, prev_row, next_row)
    uc = xm * cw_ref[0:1] + x * cw_ref[1:2] + xp * cw_ref[2:3] + cb_ref[...]
    o_ref[0] = uc.T


def hy_pre(u_hy, batch, seq_len, conv_w, conv_b, tm=512):
    t = u_hy.shape[0]
    nb = seq_len // tm
    hb = tm // 8
    last_halo = t // 8 - 1
    const = lambda b, i: (0, 0)
    return pl.pallas_call(
        functools.partial(_hy_pre_kernel, tm=tm),
        grid=(batch, nb),
        in_specs=[pl.BlockSpec((tm, HY_IN), lambda b, i: (b * nb + i, 0)),
                  pl.BlockSpec((8, HY_IN), lambda b, i: (jnp.maximum((b * nb + i) * hb - 1, 0), 0)),
                  pl.BlockSpec((8, HY_IN), lambda b, i: (jnp.minimum((b * nb + i + 1) * hb, last_halo), 0)),
                  pl.BlockSpec((3, HY_IN), const),
                  pl.BlockSpec((1, HY_IN), const)],
        out_specs=pl.BlockSpec((1, HY_IN, tm), lambda b, i: (b, 0, i)),
        out_shape=jax.ShapeDtypeStruct((batch, HY_IN, seq_len), F32),
        compiler_params=_cparams(2),
        name="hy_pre",
    )(u_hy, u_hy, u_hy, conv_w, conv_b.reshape(1, HY_IN))


def _hy_conv_kernel(z_ref, g_ref, sk_ref, h_ref, f1_ref, w2_ref, t_ref, gi_ref, o_ref, *, cb, n2):
    ct = z_ref.shape[1]
    f1 = f1_ref[...]
    w2 = w2_ref[...]
    tre, tim = t_ref[0:HY_N1], t_ref[HY_N1:]
    gi = gi_ref[...]

    def quads(blk):
        return blk[0:HY_N1, 0:n2], blk[0:HY_N1, n2:], blk[HY_N1:, 0:n2], blk[HY_N1:, n2:]

    def body(it, carry):
        c0 = it * cb
        planes = []
        for u in range(cb):
            a = jnp.dot(f1, z_ref[0, c0 + u].astype(BF16), preferred_element_type=F32)
            are, aim = a[0:HY_N1], a[HY_N1:]
            planes += [(are * tre - aim * tim).astype(BF16), (are * tim + aim * tre).astype(BF16)]
        spec = jnp.dot(jnp.concatenate(planes, axis=0), w2, preferred_element_type=F32)
        planes = []
        for u in range(cb):
            tl, tr, bl, br = quads(spec[u * 2 * HY_N1:(u + 1) * 2 * HY_N1])
            bre, bim = tl - br, tr + bl
            hre, him = h_ref[0, c0 + u, 0:HY_N1], h_ref[0, c0 + u, HY_N1:]
            planes += [(bre * hre - bim * him).astype(BF16), (bre * him + bim * hre).astype(BF16)]
        back = jnp.dot(jnp.concatenate(planes, axis=0), w2, preferred_element_type=F32)
        for u in range(cb):
            tl, tr, bl, br = quads(back[u * 2 * HY_N1:(u + 1) * 2 * HY_N1])
            are, aim = tl + br, bl - tr
            st = jnp.concatenate([are * tre + aim * tim, aim * tre - are * tim], axis=0).astype(BF16)
            y = jnp.dot(gi, st, preferred_element_type=F32)
            o_ref[0, c0 + u] = g_ref[0, c0 + u] * (y + z_ref[0, c0 + u] * sk_ref[c0 + u])
        return carry

    lax.fori_loop(0, ct // cb, body, 0)


def hy_conv(z4, z_off, gate4, gate_off, skip, h_spec, order, consts, ct=32, cb=16):
    b, _, _, n2 = z4.shape
    f1, w2, tw, gi = consts
    nct = HY_C // ct
    seq = lambda off: (lambda bi, j: (bi, off // ct + j, 0, 0))
    const = lambda bi, j: (0, 0)
    return pl.pallas_call(
        functools.partial(_hy_conv_kernel, cb=cb, n2=n2),
        grid=(b, nct),
        in_specs=[pl.BlockSpec((1, ct, HY_ROWS, n2), seq(z_off)),
                  pl.BlockSpec((1, ct, HY_ROWS, n2), seq(gate_off)),
                  pl.BlockSpec((ct, 1, 1), lambda bi, j: (j, 0, 0)),
                  pl.BlockSpec((1, ct, 2 * HY_N1, n2), lambda bi, j: (order, j, 0, 0)),
                  pl.BlockSpec((2 * HY_N1, HY_ROWS), const),
                  pl.BlockSpec((n2, 2 * n2), const),
                  pl.BlockSpec((2 * HY_N1, n2), const),
                  pl.BlockSpec((HY_ROWS, 2 * HY_N1), const)],
        out_specs=pl.BlockSpec((1, ct, HY_ROWS, n2), lambda bi, j: (bi, j, 0, 0)),
        out_shape=jax.ShapeDtypeStruct((b, HY_C, HY_ROWS, n2), F32),
        compiler_params=_cparams(2),
        name="hy_conv",
    )(z4, gate4, skip.reshape(HY_C, 1, 1), h_spec, f1, w2, tw, gi)


def _hy_post_kernel(z_ref, g_ref, o_ref):
    x = z_ref[0]
    ms = jnp.mean(x * x, axis=0, keepdims=True)
    o_ref[...] = (x * lax.rsqrt(ms + EPS) * g_ref[...]).T.astype(BF16)


def hy_post(z3, g, tm=512):
    b, _, seq_len = z3.shape
    nb = seq_len // tm
    return pl.pallas_call(
        _hy_post_kernel,
        grid=(b, nb),
        in_specs=[pl.BlockSpec((1, HY_C, tm), lambda bi, i: (bi, 0, i)),
                  pl.BlockSpec((HY_C, 1), lambda bi, i: (0, 0))],
        out_specs=pl.BlockSpec((tm, HY_C), lambda bi, i: (bi * nb + i, 0)),
        out_shape=jax.ShapeDtypeStruct((b * seq_len, HY_C), BF16),
        compiler_params=_cparams(2),
        name="hy_post",
    )(z3, g.reshape(HY_C, 1))


def _dft_consts(n2):
    n = HY_N1 * n2
    k1 = np.arange(HY_N1, dtype=np.float64)
    a1 = 2.0 * np.pi * np.outer(k1, np.arange(HY_ROWS)) / HY_N1
    f1 = np.concatenate([np.cos(a1), -np.sin(a1)], axis=0)
    m2 = np.arange(n2, dtype=np.float64)
    a2 = 2.0 * np.pi * np.outer(m2, m2) / n2
    w2 = np.concatenate([np.cos(a2), -np.sin(a2)], axis=1)
    at = 2.0 * np.pi * np.outer(k1, m2) / n
    tw = np.concatenate([np.cos(at), -np.sin(at)], axis=0)
    return (jnp.asarray(f1, BF16), jnp.asarray(w2, BF16), jnp.asarray(tw, F32), jnp.asarray(f1.T, BF16))


def _hy_taps_kernel(z_ref, w1_ref, b1_ref, w2_ref, b2_ref, fr_ref, w3_ref, dl_ref, o_ref, hid_ref):
    @pl.when((pl.program_id(0) == 0) & (pl.program_id(1) == 0))
    def _():
        fr = fr_ref[...]
        hid = jnp.sin(fr * (jnp.dot(w1_ref[...], z_ref[...], precision=HIGHEST,
                                    preferred_element_type=F32) + b1_ref[...]))
        hid_ref[...] = jnp.sin(fr * (jnp.dot(w2_ref[...], hid, precision=HIGHEST,
                                             preferred_element_type=F32) + b2_ref[...]))

    hid = hid_ref[...]
    dec = jnp.exp(-z_ref[0:1, :] * jnp.abs(dl_ref[...]))
    h = [jnp.dot(w3_ref[0, d], hid, precision=HIGHEST, preferred_element_type=F32) * dec for d in range(2)]
    norm = (jnp.sum(jnp.abs(h[0]), axis=1, keepdims=True) + jnp.sum(jnp.abs(h[1]), axis=1, keepdims=True))
    for d in range(2):
        o_ref[0, d] = h[d] / norm


def hy_taps(L, w1, b1, w2, b2, w3, freq, rb=32):
    t = jnp.linspace(0.0, 1.0, L, dtype=F32)[:, None]
    w = 2.0 * math.pi * jnp.arange(L, dtype=F32) / L
    bands = jnp.linspace(1e-4, HY_BANDS - 1, HY_BANDS, dtype=F32)
    ang = w[:, None] * bands
    zt = jnp.concatenate([t, jnp.cos(ang), -jnp.sin(ang)], axis=-1).T
    deltas = jnp.linspace(HY_MIN_DECAY, HY_MAX_DECAY, HY_C, dtype=F32).reshape(HY_C, 1)
    hid_n = w2.shape[0]
    emb = zt.shape[0]
    col = lambda v: v.reshape(hid_n, 1)
    w3t = w3.T.reshape(HY_ORDER, 2, HY_C, hid_n)
    const = lambda o, j: (0, 0)
    return pl.pallas_call(
        _hy_taps_kernel,
        grid=(HY_ORDER, HY_C // rb),
        in_specs=[pl.BlockSpec((emb, L), const),
                  pl.BlockSpec((hid_n, emb), const), pl.BlockSpec((hid_n, 1), const),
                  pl.BlockSpec((hid_n, hid_n), const), pl.BlockSpec((hid_n, 1), const),
                  pl.BlockSpec((hid_n, 1), const),
                  pl.BlockSpec((1, 2, rb, hid_n), lambda o, j: (o, 0, j, 0)),
                  pl.BlockSpec((rb, 1), lambda o, j: (j, 0))],
        out_specs=pl.BlockSpec((1, 2, rb, L), lambda o, j: (o, 0, j, 0)),
        out_shape=jax.ShapeDtypeStruct((HY_ORDER, 2, HY_C, L), F32),
        scratch_shapes=[pltpu.VMEM((hid_n, L), F32)],
        compiler_params=_cparams(2),
        name="hy_taps",
    )(zt, w1.T, col(b1), w2.T, col(b2), col(freq), w3t, deltas)


def _hy_spec_kernel(h_ref, f1_ref, w2_ref, t_ref, o_ref, *, cb, n2):
    ct = h_ref.shape[2]
    f1 = f1_ref[...]
    w2 = w2_ref[...]
    tre, tim = t_ref[0:HY_N1], t_ref[HY_N1:]
    inv_n = 1.0 / (HY_N1 * n2)

    def body(it, carry):
        c0 = it * cb
        planes = []
        for u in range(cb):
            for d in range(2):
                a = jnp.dot(f1, h_ref[0, d, c0 + u].astype(BF16), preferred_element_type=F32)
                are, aim = a[0:HY_N1], a[HY_N1:]
                planes += [(are * tre - aim * tim).astype(BF16), (are * tim + aim * tre).astype(BF16)]
        spec = jnp.dot(jnp.concatenate(planes, axis=0), w2, preferred_element_type=F32)
        for u in range(cb):
            re, im = [], []
            for d in range(2):
                blk = spec[(2 * u + d) * 2 * HY_N1:(2 * u + d + 1) * 2 * HY_N1]
                tl, tr, bl, br = blk[0:HY_N1, 0:n2], blk[0:HY_N1, n2:], blk[HY_N1:, 0:n2], blk[HY_N1:, n2:]
                re.append(tl - br)
                im.append(tr + bl)
            o_ref[0, c0 + u, 0:HY_N1] = (re[0] + re[1]) * inv_n
            o_ref[0, c0 + u, HY_N1:] = (im[0] - im[1]) * inv_n
        return carry

    lax.fori_loop(0, ct // cb, body, 0)


def hy_spec(taps, consts, ct=32, cb=8):
    L = taps.shape[-1]
    n2 = 2 * L // HY_N1
    f1, w2, tw, _ = consts
    const = lambda o, j: (0, 0)
    return pl.pallas_call(
        functools.partial(_hy_spec_kernel, cb=cb, n2=n2),
        grid=(HY_ORDER, HY_C // ct),
        in_specs=[pl.BlockSpec((1, 2, ct, HY_ROWS, n2), lambda o, j: (o, 0, j, 0, 0)),
                  pl.BlockSpec((2 * HY_N1, HY_ROWS), const),
                  pl.BlockSpec((n2, 2 * n2), const),
                  pl.BlockSpec((2 * HY_N1, n2), const)],
        out_specs=pl.BlockSpec((1, ct, 2 * HY_N1, n2), lambda o, j: (o, j, 0, 0)),
        out_shape=jax.ShapeDtypeStruct((HY_ORDER, HY_C, 2 * HY_N1, n2), F32),
        compiler_params=_cparams(2),
        name="hy_spec",
    )(taps.reshape(HY_ORDER, 2, HY_C, HY_ROWS, n2), f1, w2, tw)


def _hyena_filter_freq(L, w1, b1, w2, b2, w3, freq):
    return hy_spec(hy_taps(L, w1, b1, w2, b2, w3, freq), _dft_consts(2 * L // HY_N1))


def hyena_mixer(u_hy, batch, seq_len, p, h_spec):
    n2 = 2 * seq_len // HY_N1
    consts = _dft_consts(n2)
    uct = hy_pre(u_hy, batch, seq_len, p['hy_conv_w'], p['hy_conv_b'])
    uc4 = uct.reshape(batch, HY_IN, HY_ROWS, n2)
    z = hy_conv(uc4, 0, uc4, HY_C, p['hy_skip'][0], h_spec, 0, consts)
    z = hy_conv(z, 0, uc4, 2 * HY_C, p['hy_skip'][1], h_spec, 1, consts)
    return hy_post(z.reshape(batch, HY_C, seq_len), p['hy_out_g'])


_DEINTERLEAVE = np.concatenate([np.arange(0, ATT_HEAD_DIM, 2), np.arange(1, ATT_HEAD_DIM, 2)])


def _pack_w_in(w):
    c0 = 0
    w_hy = w[:, c0:c0 + HY_IN]; c0 += HY_IN
    w_ssd = w[:, c0:c0 + SSD_IN]; c0 += SSD_IN
    w_gla = w[:, c0:c0 + GLA_IN]; c0 += GLA_IN
    w_att = w[:, c0:c0 + ATT_IN]
    zeros = lambda n: jnp.zeros((D_MODEL, n), w.dtype)
    cols = [w_hy, w_ssd[:, :SSD_INNER + SSD_XBC],
            jnp.repeat(w_ssd[:, SSD_INNER + SSD_XBC:], SSD_HEAD_DIM, axis=1),
            w_gla, zeros(GLA_PAD - GLA_IN)]
    for h in range(ATT_HEADS):
        wh = w_att[:, h * ATT_HEAD_DIM:(h + 1) * ATT_HEAD_DIM][:, _DEINTERLEAVE]
        cols += [wh, zeros(64)] if h < 2 else [zeros(64), wh]
    kq = ATT_HEADS * ATT_HEAD_DIM
    for g in range(ATT_KV_HEADS):
        cols.append(w_att[:, kq + g * ATT_HEAD_DIM:kq + (g + 1) * ATT_HEAD_DIM][:, _DEINTERLEAVE])
    cols.append(w_att[:, kq + ATT_KV_HEADS * ATT_HEAD_DIM:])
    return jnp.concatenate(cols, axis=1).astype(BF16)


def _rope_tables(seq_len):
    rows = seq_len // GRID_W
    row = jnp.repeat(jnp.arange(rows, dtype=F32), GRID_W)
    col = jnp.tile(jnp.arange(GRID_W, dtype=F32), rows)
    inv_freq = 1.0 / (ROPE_THETA ** (jnp.arange(0, ROPE_AXIS_DIM, 2, dtype=F32) / ROPE_AXIS_DIM))
    ang = jnp.concatenate([row[:, None] * inv_freq, col[:, None] * inv_freq], axis=-1)
    cos, sin = jnp.cos(ang), jnp.sin(ang)
    return jnp.tile(cos, (1, 4)), jnp.tile(jnp.concatenate([-sin, sin], axis=-1), (1, 2))


def _layer_params(i, s):
    p = {name: val[i] for name, val in s.items()}
    p['w_in_p'] = _pack_w_in(p['w_in'])
    p['w_out_b'] = p['w_out'].astype(BF16)
    p['w_up_b'] = p['w_up'].astype(BF16)
    p['w_down_b'] = p['w_down'].astype(BF16)
    p['ssd_dt_bias_x'] = jnp.repeat(p['ssd_dt_bias'].reshape(-1), SSD_HEAD_DIM).reshape(1, SSD_DTX)
    p['ssd_a_x'] = jnp.repeat(-jnp.exp(p['ssd_A_log']).reshape(-1), SSD_HEAD_DIM).reshape(1, SSD_DTX)
    p['ssd_d_x'] = jnp.repeat(p['ssd_D'], SSD_HEAD_DIM).reshape(1, SSD_INNER)
    wg = jnp.zeros((GLA_PAD - GLA_LR0, 2 * GLA_QK), F32)
    wg = wg.at[0:GLA_RANK, 0:GLA_QK].set(p['gla_gk_w'][0])
    wg = wg.at[GLA_RANK:2 * GLA_RANK, GLA_QK:].set(p['gla_gk_w'][1])
    p['gla_wg'] = wg
    p['gla_bg'] = p['gla_gk_b'].reshape(1, 2 * GLA_QK)
    p['gla_norm_g_x'] = jnp.tile(p['gla_norm_g'], GLA_HEADS).reshape(1, GLA_V)
    p['att_gq'] = jnp.tile(p['att_q_norm_g'][_DEINTERLEAVE], 2).reshape(1, LANES)
    p['att_gk'] = jnp.tile(p['att_k_norm_g'][_DEINTERLEAVE], 2).reshape(1, LANES)
    return p


def _layer(x, tables, p):
    b, L, _ = x.shape
    x2d = x.reshape(b * L, D_MODEL)
    u_hy, u_ssd, u_gla, u_att = in_proj(x2d, p['ln1_g'], p['w_in_p'])

    hf = _hyena_filter_freq(L, p['hy_ffn_w1'], p['hy_ffn_b1'], p['hy_ffn_w2'], p['hy_ffn_b2'],
                            p['hy_ffn_w3'], p['hy_sin_freq'])
    y_hy = hyena_mixer(u_hy, b, L, p, hf)
    y_ssd = ssd_mixer(u_ssd, b, L, p['ssd_conv_w'], p['ssd_conv_b'], p['ssd_dt_bias_x'], p['ssd_a_x'],
                      p['ssd_d_x'], p['ssd_norm_g'])
    y_gla = gla_mixer(u_gla, b, L, p['gla_wg'], p['gla_bg'], p['gla_norm_g_x'])
    q, k, vt = att_prep(u_att, tables[0], tables[1], p['att_gq'], p['att_gk'], b, L)
    y_att = flash_attention(q.reshape(b, L, ATT_Q_PAD), k.reshape(b, L, LANES), vt, p['att_out_g'])

    y = out_ffn(x2d, (y_hy, y_ssd, y_gla, y_att.reshape(b * L, GROUP_W)), p['w_out_b'], p['ln2_g'],
                p['w_up_b'], p['w_down_b'])
    return y.reshape(b, L, D_MODEL)


def kernel(x_prompt, x_sample, ln1_g, w_in, hy_conv_w, hy_conv_b, hy_ffn_w1, hy_ffn_b1, hy_ffn_w2, hy_ffn_b2, hy_ffn_w3, hy_sin_freq, hy_skip, hy_out_g, ssd_conv_w, ssd_conv_b, ssd_A_log, ssd_dt_bias, ssd_D, ssd_norm_g, gla_gk_w, gla_gk_b, gla_norm_g, att_q_norm_g, att_k_norm_g, att_out_g, w_out, ln2_g, w_up, w_down):
    stacked = dict(ln1_g=ln1_g, w_in=w_in, hy_conv_w=hy_conv_w, hy_conv_b=hy_conv_b, hy_ffn_w1=hy_ffn_w1,
                   hy_ffn_b1=hy_ffn_b1, hy_ffn_w2=hy_ffn_w2, hy_ffn_b2=hy_ffn_b2, hy_ffn_w3=hy_ffn_w3,
                   hy_sin_freq=hy_sin_freq, hy_skip=hy_skip, hy_out_g=hy_out_g, ssd_conv_w=ssd_conv_w,
                   ssd_conv_b=ssd_conv_b, ssd_A_log=ssd_A_log, ssd_dt_bias=ssd_dt_bias, ssd_D=ssd_D,
                   ssd_norm_g=ssd_norm_g, gla_gk_w=gla_gk_w, gla_gk_b=gla_gk_b, gla_norm_g=gla_norm_g,
                   att_q_norm_g=att_q_norm_g, att_k_norm_g=att_k_norm_g, att_out_g=att_out_g,
                   w_out=w_out, ln2_g=ln2_g, w_up=w_up, w_down=w_down)
    tables_p = _rope_tables(x_prompt.shape[1])
    tables_s = _rope_tables(x_sample.shape[1])
    y_prompt, y_sample = x_prompt, x_sample
    for i in range(DEPTH):
        p = _layer_params(i, stacked)
        y_prompt = _layer(y_prompt, tables_p, p)
        y_sample = _layer(y_sample, tables_s, p)
    return (y_prompt, y_sample)
```
